```python
import math
import jax
import jax.numpy as jnp
from jax import lax
import numpy as np

D_MODEL = 1024
BATCH = 32
SEQ = 2048
DEPTH = 4
DEC_BATCH = 8
DEC_SEQ = 64
PAST_LEN = 1024

CHUNK = 64
N_MIXERS = 4
N_HEADS = 16
HEAD_DIM = 64
ATTN_WIDTH = N_HEADS * HEAD_DIM
NUM_BUCKETS = 32
T5_MAX_DISTANCE = 128
A_PAST_CHUNKS = 8
A_PAST = A_PAST_CHUNKS * CHUNK
A_CLIP = 64
B_WINDOW = 128
B_KV_HEADS = 4
C_KV_HEADS = 4
C_TOPK = 256
C_IDX_HEADS = 8
C_IDX_DIM = 64
C_Q_BLOCK = 32
D_HEADS = 8
D_HEAD_DIM = 64
Q_BLOCK = 128
D_FF = 2816
CONV_W = 3
RMS_EPS = 1e-6
NEG_INF = -1e30

kernel_name = 'hybrid_streaming_encoder_step'


def rmsnorm(x, g):
    xf = x.astype(jnp.float32)
    y = xf * lax.rsqrt(jnp.mean(xf * xf, axis=-1, keepdims=True) + RMS_EPS)
    return (y * g.astype(jnp.float32)).astype(x.dtype)


def split_heads(x, n, d):
    return x.reshape(x.shape[0], x.shape[1], n, d)


def t5_bucket(rel):
    half = NUM_BUCKETS // 2
    max_exact = half // 2
    base = jnp.where(rel > 0, half, 0)
    n = jnp.abs(rel)
    nf = jnp.maximum(n, 1).astype(jnp.float32)
    large = max_exact + (jnp.log(nf / max_exact) / math.log(T5_MAX_DISTANCE / max_exact)
                         * (half - max_exact)).astype(jnp.int32)
    large = jnp.minimum(large, half - 1)
    return base + jnp.where(n < max_exact, n, large)


def t5_bias(table, rel):
    return table.astype(jnp.float32)[t5_bucket(rel)]


def gqa_attend(q, k, v, bias, mask, sinks):
    b, tq, h, dh = q.shape
    kvh = k.shape[2]
    g = h // kvh
    qg = q.reshape(b, tq, kvh, g, dh)
    logits = jnp.einsum('bqkgd,bskd->bkgqs', qg, k).astype(jnp.float32) * (dh ** -0.5)
    logits = logits.reshape(b, h, tq, -1) + bias
    logits = jnp.where(mask, logits, NEG_INF)
    if sinks is None:
        p = jax.nn.softmax(logits, axis=-1)
    else:
        s = jnp.broadcast_to(sinks.astype(jnp.float32)[None, :, None, None], (b, h, tq, 1))
        p = jax.nn.softmax(jnp.concatenate([logits, s], axis=-1), axis=-1)[..., :-1]
    p = p.reshape(b, kvh, g, tq, -1).astype(v.dtype)
    out = jnp.einsum('bkgqs,bskd->bqkgd', p, v)
    return out.reshape(b, tq, h, v.shape[-1])


def band_attention_prompt(q, k, v, n_past, bias_fn, sinks):
    b, s, h, dh = q.shape
    pad = ((0, 0), (n_past, 0), (0, 0), (0, 0))
    kp = jnp.pad(k, pad)
    vp = jnp.pad(v, pad)
    band = n_past + CHUNK
    offs = jnp.arange(band)

    def one_chunk(c):
        start = c * CHUNK
        qc = lax.dynamic_slice_in_dim(q, start, CHUNK, axis=1)
        kc = lax.dynamic_slice_in_dim(kp, start, band, axis=1)
        vc = lax.dynamic_slice_in_dim(vp, start, band, axis=1)
        qpos = start + jnp.arange(CHUNK)
        kpos = start - n_past + offs
        mask = (kpos >= 0)[None, None, None, :]
        bias = jnp.moveaxis(bias_fn(kpos[None, :] - qpos[:, None]), -1, 0)[None]
        return gqa_attend(qc, kc, vc, bias, mask, sinks)

    out = lax.map(one_chunk, jnp.arange(s // CHUNK))
    return jnp.moveaxis(out, 0, 1).reshape(b, s, h, -1)


def band_attention_sample(q, k, v, cache_k, cache_v, n_past, bias_fn, sinks):
    t = q.shape[1]
    lc = cache_k.shape[1]
    k_all = jnp.concatenate([cache_k, k], axis=1)
    v_all = jnp.concatenate([cache_v, v], axis=1)
    qpos = PAST_LEN + jnp.arange(t)
    kpos = jnp.concatenate([PAST_LEN - lc + jnp.arange(lc), qpos])
    cq = qpos[:, None] // CHUNK
    ck = kpos[None, :] // CHUNK
    mask = (ck <= cq) & (ck >= cq - n_past // CHUNK)
    bias = jnp.moveaxis(bias_fn(kpos[None, :] - qpos[:, None]), -1, 0)[None]
    y = gqa_attend(q, k_all, v_all, bias, mask[None, None], sinks)
    keep = min(n_past, lc + t)
    return y, k_all[:, -keep:], v_all[:, -keep:]


def mixer_a(hp, hs, cache_k, cache_v, w_qkv, w_o, rel_table):
    def bias_fn(rel):
        return rel_table.astype(jnp.float32)[jnp.clip(rel, -A_CLIP, A_CLIP) + A_CLIP]

    def proj(h):
        q, k, v = jnp.split(h @ w_qkv, 3, axis=-1)
        return (split_heads(q, N_HEADS, HEAD_DIM), split_heads(k, N_HEADS, HEAD_DIM),
                split_heads(v, N_HEADS, HEAD_DIM))

    qp, kp, vp = proj(hp)
    yp = band_attention_prompt(qp, kp, vp, A_PAST, bias_fn, None)
    qs, ks, vs = proj(hs)
    ys, k_keep, v_keep = band_attention_sample(qs, ks, vs, cache_k, cache_v, A_PAST, bias_fn, None)
    keep = min(A_PAST, hp.shape[1])
    mp = yp.reshape(hp.shape[0], hp.shape[1], -1) @ w_o
    ms = ys.reshape(hs.shape[0], hs.shape[1], -1) @ w_o
    return mp, ms, kp[:, -keep:], vp[:, -keep:], k_keep, v_keep


def mixer_b(hp, hs, cache_k, cache_v, t5_table, w_qkv, w_o, sinks):
    kv_w = B_KV_HEADS * HEAD_DIM

    def bias_fn(rel):
        return t5_bias(t5_table, rel)

    def proj(h):
        q, k, v = jnp.split(h @ w_qkv, [ATTN_WIDTH, ATTN_WIDTH + kv_w], axis=-1)
        return (split_heads(q, N_HEADS, HEAD_DIM), split_heads(k, B_KV_HEADS, HEAD_DIM),
                split_heads(v, B_KV_HEADS, HEAD_DIM))

    qp, kp, vp = proj(hp)
    yp = band_attention_prompt(qp, kp, vp, B_WINDOW, bias_fn, sinks)
    qs, ks, vs = proj(hs)
    ys, k_keep, v_keep = band_attention_sample(qs, ks, vs, cache_k, cache_v, B_WINDOW, bias_fn, sinks)
    keep = min(B_WINDOW, hp.shape[1])
    mp = yp.reshape(hp.shape[0], hp.shape[1], -1) @ w_o
    ms = ys.reshape(hs.shape[0], hs.shape[1], -1) @ w_o
    return mp, ms, kp[:, -keep:], vp[:, -keep:], k_keep, v_keep


def dsa_attend(q, qi, wi, qpos, k_all, v_all, ki_all, kpos, t5_table, topk):
    b, tq, h, dh = q.shape
    kvh = k_all.shape[2]
    g = h // kvh
    dots = jax.nn.relu(jnp.einsum('bqhd,bsd->bqhs', qi, ki_all).astype(jnp.float32))
    index = jnp.einsum('bqh,bqhs->bqs', wi.astype(jnp.float32), dots)
    admissible = (kpos[None, :] // CHUNK) <= (qpos[:, None] // CHUNK)
    index = jnp.where(admissible[None], index, NEG_INF)
    top_val, top_idx = lax.top_k(index, topk)
    valid = top_val > 0.5 * NEG_INF
    gather = jax.vmap(lambda rows, idx: rows[idx])
    k_sel = gather(k_all, top_idx)
    v_sel = gather(v_all, top_idx)
    rel = kpos[top_idx] - qpos[None, :, None]
    bias = t5_bias(t5_table, rel).reshape(b, tq, topk, kvh, g).transpose(0, 1, 3, 4, 2)
    qg = q.reshape(b, tq, kvh, g, dh)
    logits = jnp.einsum('bqkgd,bqskd->bqkgs', qg, k_sel).astype(jnp.float32) * (dh ** -0.5) + bias
    logits = jnp.where(valid[:, :, None, None, :], logits, NEG_INF)
    p = jax.nn.softmax(logits, axis=-1).astype(v_all.dtype)
    out = jnp.einsum('bqkgs,bqskd->bqkgd', p, v_sel)
    return out.reshape(b, tq, h * v_all.shape[-1])


def mixer_c(hp, hs, cache_k, cache_v, cache_kidx, t5_table, w_qkv, w_o,
            w_idx_q, w_idx_k, idx_k_norm, w_idx_w):
    kv_w = C_KV_HEADS * HEAD_DIM

    def proj(h):
        q, k, v = jnp.split(h @ w_qkv, [ATTN_WIDTH, ATTN_WIDTH + kv_w], axis=-1)
        qi = split_heads(h @ w_idx_q, C_IDX_HEADS, C_IDX_DIM) * (C_IDX_DIM ** -0.5)
        ki = rmsnorm(h @ w_idx_k, idx_k_norm)
        wi = (h @ w_idx_w) * (C_IDX_HEADS ** -0.5)
        return (split_heads(q, N_HEADS, HEAD_DIM), split_heads(k, C_KV_HEADS, HEAD_DIM),
                split_heads(v, C_KV_HEADS, HEAD_DIM), qi, ki, wi)

    qp, kp, vp, qip, kip, wip = proj(hp)
    s = hp.shape[1]
    topk_p = min(C_TOPK, s // 4)
    kpos_p = jnp.arange(s)

    def one_block(i):
        st = i * C_Q_BLOCK
        sl = lambda a: lax.dynamic_slice_in_dim(a, st, C_Q_BLOCK, axis=1)
        return dsa_attend(sl(qp), sl(qip), sl(wip), st + jnp.arange(C_Q_BLOCK),
                          kp, vp, kip, kpos_p, t5_table, topk_p)

    yp = lax.map(one_block, jnp.arange(s // C_Q_BLOCK))
    yp = jnp.moveaxis(yp, 0, 1).reshape(hp.shape[0], s, -1)

    qs, ks, vs, qis, kis, wis = proj(hs)
    t = hs.shape[1]
    k_all = jnp.concatenate([cache_k, ks], axis=1)
    v_all = jnp.concatenate([cache_v, vs], axis=1)
    ki_all = jnp.concatenate([cache_kidx, kis], axis=1)
    n_keys = k_all.shape[1]
    ys = dsa_attend(qs, qis, wis, PAST_LEN + jnp.arange(t), k_all, v_all, ki_all,
                    jnp.arange(n_keys), t5_table, min(C_TOPK, n_keys // 4))
    return yp @ w_o, ys @ w_o, kp, vp, kip, ks, vs, kis


def diff_attend(q, k, v, qpos, kpos, t5_table, lam, subln, lambda_init):
    b, tq, h2, dh = q.shape
    tk = k.shape[1]
    logits = jnp.einsum('bqhd,bshd->bhqs', q, k).astype(jnp.float32) * (dh ** -0.5)
    logits = logits + jnp.moveaxis(t5_bias(t5_table, kpos[None, :] - qpos[:, None]), -1, 0)[None]
    mask = (kpos[None, :] // CHUNK) <= (qpos[:, None] // CHUNK)
    logits = jnp.where(mask[None, None], logits, NEG_INF)
    p = jax.nn.softmax(logits, axis=-1).reshape(b, h2 // 2, 2, tq, tk)
    a = (p[:, :, 0] - lam * p[:, :, 1]).astype(v.dtype)
    out = jnp.einsum('bhqs,bshd->bqhd', a, v)
    out = rmsnorm(out, subln) * (1.0 - lambda_init)
    return out.reshape(b, tq, -1)


def mixer_d(hp, hs, cache_k, cache_v, t5_table, w_qkv, w_o,
            lq1, lk1, lq2, lk2, subln, lambda_init):
    f32 = jnp.float32
    lam = (jnp.exp(jnp.sum(lq1.astype(f32) * lk1.astype(f32)))
           - jnp.exp(jnp.sum(lq2.astype(f32) * lk2.astype(f32))) + lambda_init)

    def proj(h):
        q, k, v = jnp.split(h @ w_qkv, 3, axis=-1)
        return (split_heads(q, 2 * D_HEADS, D_HEAD_DIM), split_heads(k, 2 * D_HEADS, D_HEAD_DIM),
                split_heads(v, D_HEADS, 2 * D_HEAD_DIM))

    qp, kp, vp = proj(hp)
    s = hp.shape[1]
    kpos_p = jnp.arange(s)

    def one_block(i):
        st = i * Q_BLOCK
        qb = lax.dynamic_slice_in_dim(qp, st, Q_BLOCK, axis=1)
        return diff_attend(qb, kp, vp, st + jnp.arange(Q_BLOCK), kpos_p, t5_table, lam, subln, lambda_init)

    yp = lax.map(one_block, jnp.arange(s // Q_BLOCK))
    yp = jnp.moveaxis(yp, 0, 1).reshape(hp.shape[0], s, -1)

    qs, ks, vs = proj(hs)
    t = hs.shape[1]
    k_all = jnp.concatenate([cache_k, ks], axis=1)
    v_all = jnp.concatenate([cache_v, vs], axis=1)
    ys = diff_attend(qs, k_all, v_all, PAST_LEN + jnp.arange(t), jnp.arange(k_all.shape[1]),
                     t5_table, lam, subln, lambda_init)
    return yp @ w_o, ys @ w_o, kp, vp, ks, vs


def conv_ffn(h, left, w_in, conv_w, conv_b, w_out):
    t = h.shape[1]
    a, g = jnp.split(h @ w_in, 2, axis=-1)
    ap = jnp.concatenate([left, a], axis=1)
    u = conv_b + sum(conv_w[j] * ap[:, j:j + t] for j in range(CONV_W))
    y = (jax.nn.silu(u) * g) @ w_out
    return y, ap[:, -(CONV_W - 1):]


def setup_inputs(seed: int = 0) -> dict:
    key = jax.random.key(seed)
    keys = iter(jax.random.split(key, 48))

    def nrm(shape, scale):
        return jax.random.normal(next(keys), shape, jnp.float32) * scale

    def gain(n):
        return 1.0 + nrm(n, 0.02)

    a_cache = min(A_PAST, PAST_LEN)
    b_cache = min(B_WINDOW, PAST_LEN)
    d = D_MODEL
    return {
        'x_prompt': nrm((BATCH, SEQ, d), 1.0),
        'x_sample': nrm((DEC_BATCH, DEC_SEQ, d), 1.0),
        'cache_a_k': nrm((DEC_BATCH, a_cache, N_HEADS, HEAD_DIM), 1.0),
        'cache_a_v': nrm((DEC_BATCH, a_cache, N_HEADS, HEAD_DIM), 1.0),
        'cache_b_k': nrm((DEC_BATCH, b_cache, B_KV_HEADS, HEAD_DIM), 1.0),
        'cache_b_v': nrm((DEC_BATCH, b_cache, B_KV_HEADS, HEAD_DIM), 1.0),
        'cache_c_k': nrm((DEC_BATCH, PAST_LEN, C_KV_HEADS, HEAD_DIM), 1.0),
        'cache_c_v': nrm((DEC_BATCH, PAST_LEN, C_KV_HEADS, HEAD_DIM), 1.0),
        'cache_c_kidx': nrm((DEC_BATCH, PAST_LEN, C_IDX_DIM), 1.0),
        'cache_d_k': nrm((DEC_BATCH, PAST_LEN, 2 * D_HEADS, D_HEAD_DIM), 1.0),
        'cache_d_v': nrm((DEC_BATCH, PAST_LEN, D_HEADS, 2 * D_HEAD_DIM), 1.0),
        'state_ffn_conv': nrm((DEPTH, DEC_BATCH, CONV_W - 1, D_FF), 1.0),
        't5_table': nrm((NUM_BUCKETS, N_HEADS), 0.5),
        'norm_mix': gain((DEPTH, d)),
        'norm_ffn': gain((DEPTH, d)),
        'norm_final': gain((d,)),
        'a_w_qkv': nrm((d, 3 * ATTN_WIDTH), d ** -0.5),
        'a_w_o': nrm((ATTN_WIDTH, d), ATTN_WIDTH ** -0.5),
        'a_rel_bias': nrm((2 * A_CLIP + 1, N_HEADS), 0.5),
        'b_w_qkv': nrm((d, ATTN_WIDTH + 2 * B_KV_HEADS * HEAD_DIM), d ** -0.5),
        'b_w_o': nrm((ATTN_WIDTH, d), ATTN_WIDTH ** -0.5),
        'b_sinks': nrm((N_HEADS,), 0.5),
        'c_w_qkv': nrm((d, ATTN_WIDTH + 2 * C_KV_HEADS * HEAD_DIM), d ** -0.5),
        'c_w_o': nrm((ATTN_WIDTH, d), ATTN_WIDTH ** -0.5),
        'c_w_idx_q': nrm((d, C_IDX_HEADS * C_IDX_DIM), d ** -0.5),
        'c_w_idx_k': nrm((d, C_IDX_DIM), d ** -0.5),
        'c_idx_k_norm': gain((C_IDX_DIM,)),
        'c_w_idx_w': nrm((d, C_IDX_HEADS), d ** -0.5),
        'd_w_qkv': nrm((d, 3 * ATTN_WIDTH), d ** -0.5),
        'd_w_o': nrm((ATTN_WIDTH, d), ATTN_WIDTH ** -0.5),
        'd_lambda_q1': nrm((D_HEAD_DIM,), 0.1),
        'd_lambda_k1': nrm((D_HEAD_DIM,), 0.1),
        'd_lambda_q2': nrm((D_HEAD_DIM,), 0.1),
        'd_lambda_k2': nrm((D_HEAD_DIM,), 0.1),
        'd_subln': gain((2 * D_HEAD_DIM,)),
        'ffn_w_in': nrm((DEPTH, d, 2 * D_FF), d ** -0.5),
        'ffn_conv_w': nrm((DEPTH, CONV_W, D_FF), CONV_W ** -0.5),
        'ffn_conv_b': nrm((DEPTH, D_FF), 0.01),
        'ffn_w_out': nrm((DEPTH, D_FF, d), D_FF ** -0.5),
    }


def reference(x_prompt, x_sample,
              cache_a_k, cache_a_v, cache_b_k, cache_b_v,
              cache_c_k, cache_c_v, cache_c_kidx, cache_d_k, cache_d_v, state_ffn_conv,
              t5_table, norm_mix, norm_ffn, norm_final,
              a_w_qkv, a_w_o, a_rel_bias,
              b_w_qkv, b_w_o, b_sinks,
              c_w_qkv, c_w_o, c_w_idx_q, c_w_idx_k, c_idx_k_norm, c_w_idx_w,
              d_w_qkv, d_w_o, d_lambda_q1, d_lambda_k1, d_lambda_q2, d_lambda_k2, d_subln,
              ffn_w_in, ffn_conv_w, ffn_conv_b, ffn_w_out):
    xp, xs = x_prompt, x_sample
    conv_prompt, conv_sample = [], []
    for layer in range(DEPTH):
        kind = layer % N_MIXERS
        hp = rmsnorm(xp, norm_mix[layer])
        hs = rmsnorm(xs, norm_mix[layer])
        if kind == 0:
            mp, ms, a_k_prompt, a_v_prompt, a_k_sample, a_v_sample = mixer_a(
                hp, hs, cache_a_k, cache_a_v, a_w_qkv, a_w_o, a_rel_bias)
        elif kind == 1:
            mp, ms, b_k_prompt, b_v_prompt, b_k_sample, b_v_sample = mixer_b(
                hp, hs, cache_b_k, cache_b_v, t5_table, b_w_qkv, b_w_o, b_sinks)
        elif kind == 2:
            (mp, ms, c_k_prompt, c_v_prompt, c_kidx_prompt,
             c_k_sample, c_v_sample, c_kidx_sample) = mixer_c(
                hp, hs, cache_c_k, cache_c_v, cache_c_kidx, t5_table, c_w_qkv, c_w_o,
                c_w_idx_q, c_w_idx_k, c_idx_k_norm, c_w_idx_w)
        else:
            lambda_init = 0.8 - 0.6 * math.exp(-0.3 * layer)
            mp, ms, d_k_prompt, d_v_prompt, d_k_sample, d_v_sample = mixer_d(
                hp, hs, cache_d_k, cache_d_v, t5_table, d_w_qkv, d_w_o,
                d_lambda_q1, d_lambda_k1, d_lambda_q2, d_lambda_k2, d_subln, lambda_init)
        xp = xp + mp
        xs = xs + ms
        hp = rmsnorm(xp, norm_ffn[layer])
        hs = rmsnorm(xs, norm_ffn[layer])
        left0 = jnp.zeros((hp.shape[0], CONV_W - 1, D_FF), hp.dtype)
        fp, cp = conv_ffn(hp, left0, ffn_w_in[layer], ffn_conv_w[layer], ffn_conv_b[layer], ffn_w_out[layer])
        fs, cs = conv_ffn(hs, state_ffn_conv[layer], ffn_w_in[layer], ffn_conv_w[layer],
                          ffn_conv_b[layer], ffn_w_out[layer])
        xp = xp + fp
        xs = xs + fs
        conv_prompt.append(cp)
        conv_sample.append(cs)
    y_prompt = rmsnorm(xp, norm_final)
    y_sample = rmsnorm(xs, norm_final)
    ffn_conv_prompt = jnp.stack(conv_prompt)
    ffn_conv_sample = jnp.stack(conv_sample)
    return (y_prompt, y_sample,
            a_k_prompt, a_v_prompt, a_k_sample, a_v_sample,
            b_k_prompt, b_v_prompt, b_k_sample, b_v_sample,
            c_k_prompt, c_v_prompt, c_kidx_prompt, c_k_sample, c_v_sample, c_kidx_sample,
            d_k_prompt, d_v_prompt, d_k_sample, d_v_sample,
            ffn_conv_prompt, ffn_conv_sample)
```

```python
import functools
import math

import jax
import jax.numpy as jnp
from jax import lax
from jax.experimental import pallas as pl
from jax.experimental.pallas import tpu as pltpu

F32 = jnp.float32
BF16 = jnp.bfloat16

D_MODEL = 1024
CHUNK = 64
N_HEADS = 16
HEAD_DIM = 64
ATTN_WIDTH = N_HEADS * HEAD_DIM
NUM_BUCKETS = 32
T5_MAX_DISTANCE = 128
A_PAST = 512
A_CLIP = 64
B_WINDOW = 128
KV_HEADS = 4
C_TOPK = 256
C_IDX_HEADS = 8
C_IDX_DIM = 64
D_HEADS = 8
D_FF = 2816
CONV_W = 3
RMS_EPS = 1e-6
NEG_INF = -1e30

LANES = 128
SUBLANES = 8
ROW_TILE = 512
FLASH_TILE = 256
SELECT_ROWS = FLASH_TILE
FF_CHUNK = 256
VMEM_LIMIT = 56 * 1024 * 1024

GQA_PERM = tuple(8 * j + 4 * half + t for j in range(2) for t in range(4) for half in range(2))


def _params(n_grid_dims):
    return pltpu.CompilerParams(
        dimension_semantics=("arbitrary",) * n_grid_dims, vmem_limit_bytes=VMEM_LIMIT)


def _resident(shape, index_map):
    return pl.BlockSpec(shape, index_map, pipeline_mode=pl.Buffered(1))


def _rms(x, gain):
    ms = jnp.mean(x * x, axis=-1, keepdims=True)
    return (x * lax.rsqrt(ms + RMS_EPS)) * gain


def _dot_nt(a, b):
    return lax.dot_general(a, b, (((1,), (1,)), ((), ())), preferred_element_type=F32)


def _norm_proj_body(sections, seq_tiles, n_out, x_ref, g_ref, w_ref, aux_ref, *out_refs):
    tm = x_ref.shape[0]
    h = _rms(x_ref[...], g_ref[...]).astype(BF16)
    is_seq_end = (pl.program_id(0) % seq_tiles) == seq_tiles - 1
    for col0, width, dests in sections:
        y = jnp.dot(h, w_ref[:, col0:col0 + width], preferred_element_type=F32)
        for dest in dests:
            kind = dest[0]
            if kind == "cast":
                _, o, oc, scale = dest
                val = y if scale is None else y * scale
                out_refs[o][:, oc:oc + width] = val.astype(out_refs[o].dtype)
            elif kind == "tail":
                _, o, oc, nrows = dest

                @pl.when(is_seq_end)
                def _():
                    out_refs[o][:, oc:oc + width] = y[tm - nrows:, :]
            elif kind == "kidx":
                _, o_dup, o_f32 = dest
                yn = _rms(y, aux_ref[...])
                out_refs[o_dup][...] = yn.astype(BF16)
                out_refs[o_f32][...] = yn[:, :C_IDX_DIM]
            else:
                raise ValueError(kind)


def _norm_proj(x, gain, w, aux, sections, outs, seq_tiles, tm):
    n = x.shape[0]
    grid = (n // tm,)
    out_shape, out_specs = [], []
    for rows, width, dtype, blk_rows, mode in outs:
        out_shape.append(jax.ShapeDtypeStruct((rows, width), dtype))
        if mode == "all":
            out_specs.append(pl.BlockSpec((blk_rows, width), lambda i: (i, 0)))
        else:
            out_specs.append(pl.BlockSpec((blk_rows, width), lambda i: (i // seq_tiles, 0)))
    body = functools.partial(_norm_proj_body, sections, seq_tiles, len(outs))
    return pl.pallas_call(
        body,
        grid=grid,
        in_specs=[
            pl.BlockSpec((tm, D_MODEL), lambda i: (i, 0)),
            _resident((1, D_MODEL), lambda i: (0, 0)),
            _resident(w.shape, lambda i: (0, 0)),
            _resident(aux.shape, lambda i: (0, 0)),
        ],
        out_specs=out_specs,
        out_shape=out_shape,
        compiler_params=_params(1),
        name="norm_proj",
    )(x, gain.reshape(1, D_MODEL), w, aux)


def _split_sections(col0, total, width, dests_fn):
    return [(col0 + c, width, dests_fn(c)) for c in range(0, total, width)]


def _softmax_rows(s, sink):
    m = jnp.max(s, axis=-1, keepdims=True)
    if sink is not None:
        m = jnp.maximum(m, sink)
    p = jnp.exp(s - m)
    l = jnp.sum(p, axis=-1, keepdims=True)
    if sink is not None:
        l = l + jnp.exp(sink - m)
    return p, l


def _band_body(cfg, *refs):
    n_pairs, win, n_past, n_chunks, mode, lambda_init, has_sel = cfg
    it = iter(refs)
    q_ref, k_ref, v_ref, bias_ref = next(it), next(it), next(it), next(it)
    sel_ref = next(it) if has_sel else None
    sink_ref = next(it) if mode == "sink" else None
    if mode == "diff":
        lam_ref, subln_ref = next(it), next(it)
    o_ref = next(it)
    if n_past:
        kp_ref, vp_ref = next(it), next(it)
        kp_ref[0:n_past, :] = jnp.zeros((n_past, LANES), BF16)
        vp_ref[0:n_past, :] = jnp.zeros((n_past, LANES), BF16)
        kp_ref[n_past:, :] = k_ref[...]
        vp_ref[n_past:, :] = v_ref[...]
    else:
        kp_ref, vp_ref = k_ref, v_ref

    lane = lax.broadcasted_iota(jnp.int32, (CHUNK, LANES), 1)
    col = lax.broadcasted_iota(jnp.int32, (CHUNK, win), 1)
    low = lane < HEAD_DIM
    head0 = pl.program_id(1) * (2 * n_pairs)
    if mode == "diff":
        lv = lam_ref[...]
        lam = (jnp.exp(jnp.sum(lv[0:1] * lv[1:2], axis=-1, keepdims=True))
               - jnp.exp(jnp.sum(lv[2:3] * lv[3:4], axis=-1, keepdims=True)) + lambda_init)

    def chunk(c, carry):
        r0 = pl.multiple_of(c * CHUNK, CHUNK)
        kwin = kp_ref[pl.ds(r0, win), :]
        vwin = vp_ref[pl.ds(r0, win), :]
        for p in range(n_pairs):
            qp = q_ref[pl.ds(r0, CHUNK), p * LANES:(p + 1) * LANES]
            probs, outs = [], []
            for half in range(2):
                qh = jnp.where(low if half == 0 else ~low, qp, jnp.zeros_like(qp))
                s = _dot_nt(qh, kwin) + bias_ref[2 * p + half]
                if n_past:
                    s = jnp.where(col >= n_past - c * CHUNK, s, NEG_INF)
                if has_sel:
                    s = s + sel_ref[...].astype(F32)
                sink = sink_ref[head0 + 2 * p + half] if mode == "sink" else None
                pr, l = _softmax_rows(s, sink)
                if mode == "diff":
                    probs.append(pr / l)
                else:
                    o = jnp.dot(pr.astype(BF16), vwin, preferred_element_type=F32)
                    outs.append(o / l)
            if mode == "diff":
                a = (probs[0] - lam * probs[1]).astype(BF16)
                o = jnp.dot(a, vwin, preferred_element_type=F32)
                o = _rms(o, subln_ref[...]) * (1.0 - lambda_init)
            else:
                o = jnp.where(low, outs[0], outs[1])
            o_ref[pl.ds(r0, CHUNK), p * LANES:(p + 1) * LANES] = o.astype(o_ref.dtype)
        return carry

    if n_chunks == 1:
        chunk(0, 0)
    else:
        lax.fori_loop(0, n_chunks, chunk, 0)


def _band_attention(q_arr, q_blk0, k_arr, k_blk0, v_arr, v_blk0, bias, *, batch, q_rows, k_rows,
                    n_kv_pairs, n_pairs, win, n_past, mode="plain", sel=None, sinks=None,
                    lam_vecs=None, subln=None, lambda_init=0.0):
    n_chunks = q_rows // CHUNK
    qw = n_pairs * LANES
    cfg = (n_pairs, win, n_past, n_chunks, mode, lambda_init, sel is not None)
    in_specs = [
        pl.BlockSpec((q_rows, qw), lambda b, j: (b, q_blk0 + j)),
        pl.BlockSpec((k_rows, LANES), lambda b, j: (b, k_blk0 + j)),
        pl.BlockSpec((k_rows, LANES), lambda b, j: (b, v_blk0 + j)),
        pl.BlockSpec((2 * n_pairs, CHUNK, win), lambda b, j: (j, 0, 0)),
    ]
    args = [q_arr, k_arr, v_arr, bias]
    if sel is not None:
        in_specs.append(pl.BlockSpec((None, CHUNK, win), lambda b, j: (b, 0, 0)))
        args.append(sel)
    if mode == "sink":
        in_specs.append(pl.BlockSpec(memory_space=pltpu.SMEM))
        args.append(sinks)
    if mode == "diff":
        in_specs.append(pl.BlockSpec((4, HEAD_DIM), lambda b, j: (0, 0)))
        in_specs.append(pl.BlockSpec((1, LANES), lambda b, j: (0, 0)))
        args += [lam_vecs, subln.reshape(1, LANES)]
    scratch = []
    if n_past:
        scratch = [pltpu.VMEM((n_past + k_rows, LANES), BF16)] * 2
    return pl.pallas_call(
        functools.partial(_band_body, cfg),
        grid=(batch, n_kv_pairs),
        in_specs=in_specs,
        out_specs=pl.BlockSpec((q_rows, qw), lambda b, j: (b, j)),
        out_shape=jax.ShapeDtypeStruct((batch * q_rows, n_kv_pairs * qw), BF16),
        scratch_shapes=scratch,
        compiler_params=_params(2),
        name="band_attention",
    )(*args)


def _flash_body(cfg, *refs):
    n_pairs, mode, lambda_init, has_sel = cfg
    it = iter(refs)
    q_ref, k_ref, v_ref, bias_ref = next(it), next(it), next(it), next(it)
    sel_ref = next(it) if has_sel else None
    if mode == "diff":
        lam_ref, subln_ref = next(it), next(it)
    o_ref = next(it)
    t = FLASH_TILE
    i = pl.program_id(2)
    lane = lax.broadcasted_iota(jnp.int32, (t, LANES), 1)
    low = lane < HEAD_DIM
    row = lax.broadcasted_iota(jnp.int32, (t, t), 0)
    col = lax.broadcasted_iota(jnp.int32, (t, t), 1)
    diag_ok = lax.shift_right_arithmetic(col, 6) <= lax.shift_right_arithmetic(row, 6)
    if mode == "diff":
        lv = lam_ref[...]
        lam = (jnp.exp(jnp.sum(lv[0:1] * lv[1:2], axis=-1, keepdims=True))
               - jnp.exp(jnp.sum(lv[2:3] * lv[3:4], axis=-1, keepdims=True)) + lambda_init)

    for p in range(n_pairs):
        qp = q_ref[:, p * LANES:(p + 1) * LANES]
        res = []
        for half in range(2):
            qh = jnp.where(low if half == 0 else ~low, qp, jnp.zeros_like(qp))

            def kstep(j, carry, qh=qh, half=half, p=p):
                m, l, acc = carry
                r0 = pl.multiple_of(j * t, t)
                kb = k_ref[pl.ds(r0, t), :]
                vb = v_ref[pl.ds(r0, t), :]
                s = _dot_nt(qh, kb) + bias_ref[jnp.minimum(i - j, 2), 2 * p + half]
                s = jnp.where(jnp.logical_or(j < i, diag_ok), s, NEG_INF)
                if has_sel:
                    s = s + sel_ref[j].astype(F32)
                m_new = jnp.maximum(m, jnp.max(s, axis=-1, keepdims=True))
                alpha = jnp.exp(m - m_new)
                pr = jnp.exp(s - m_new)
                l = alpha * l + jnp.sum(pr, axis=-1, keepdims=True)
                acc = alpha * acc + jnp.dot(pr.astype(BF16), vb, preferred_element_type=F32)
                return m_new, l, acc

            init = (jnp.full((t, 1), NEG_INF, F32), jnp.zeros((t, 1), F32),
                    jnp.zeros((t, LANES), F32))
            _, l, acc = lax.fori_loop(0, i + 1, kstep, init)
            res.append(acc / l)
        if mode == "diff":
            o = res[0] - lam * res[1]
            o = _rms(o, subln_ref[...]) * (1.0 - lambda_init)
        else:
            o = jnp.where(low, res[0], res[1])
        o_ref[:, p * LANES:(p + 1) * LANES] = o.astype(o_ref.dtype)


def _flash_attention(q_arr, q_blk0, k_arr, k_blk0, v_arr, v_blk0, bias, *, batch, seq, n_kv_pairs,
                     n_pairs, mode="plain", sel=None, lam_vecs=None, subln=None, lambda_init=0.0):
    t = FLASH_TILE
    nq = seq // t
    qw = n_pairs * LANES
    cfg = (n_pairs, mode, lambda_init, sel is not None)
    in_specs = [
        pl.BlockSpec((t, qw), lambda b, j, i: (b * nq + i, q_blk0 + j)),
        pl.BlockSpec((seq, LANES), lambda b, j, i: (b, k_blk0 + j)),
        pl.BlockSpec((seq, LANES), lambda b, j, i: (b, v_blk0 + j)),
        pl.BlockSpec((3, 2 * n_pairs, t, t), lambda b, j, i: (0, j, 0, 0)),
    ]
    args = [q_arr, k_arr, v_arr, bias]
    if sel is not None:
        in_specs.append(pl.BlockSpec((None, nq, t, t), lambda b, j, i: (b * nq + i, 0, 0, 0)))
        args.append(sel)
    if mode == "diff":
        in_specs.append(pl.BlockSpec((4, HEAD_DIM), lambda b, j, i: (0, 0)))
        in_specs.append(pl.BlockSpec((1, LANES), lambda b, j, i: (0, 0)))
        args += [lam_vecs, subln.reshape(1, LANES)]
    return pl.pallas_call(
        functools.partial(_flash_body, cfg),
        grid=(batch, n_kv_pairs, nq),
        in_specs=in_specs,
        out_specs=pl.BlockSpec((t, qw), lambda b, j, i: (b * nq + i, j)),
        out_shape=jax.ShapeDtypeStruct((batch * seq, n_kv_pairs * qw), BF16),
        compiler_params=_params(3),
        name="flash_attention",
    )(*args)


def _select_body(cfg, qi_ref, wi_ref, ki_ref, o_ref, key_ref):
    n_keys, kpos0, qpos0, topk = cfg
    r = qi_ref.shape[0]
    t = FLASH_TILE
    qrow0 = qpos0 + pl.program_id(1) * r
    lane = lax.broadcasted_iota(jnp.int32, (r, LANES), 1)
    low = lane < C_IDX_DIM
    wi = wi_ref[...]
    qchunk = lax.shift_right_arithmetic(qrow0 + lax.broadcasted_iota(jnp.int32, (r, t), 0), 6)
    col_t = lax.broadcasted_iota(jnp.int32, (r, t), 1)

    def admissible(kb):
        kpos = kpos0 + kb * t + col_t
        return jnp.logical_and(kpos >= 0, lax.shift_right_arithmetic(kpos, 6) <= qchunk)

    for kb in range(n_keys // t):
        kblk = ki_ref[kb * t:(kb + 1) * t, :]
        score = jnp.zeros((r, t), F32)
        for h in range(C_IDX_HEADS):
            qp = qi_ref[:, (h // 2) * LANES:(h // 2 + 1) * LANES]
            qh = jnp.where(low if h % 2 == 0 else ~low, qp, jnp.zeros_like(qp))
            score = score + wi[:, h:h + 1] * jnp.maximum(_dot_nt(qh, kblk), 0.0)
        score = jnp.where(admissible(kb), score, NEG_INF)
        bits = pltpu.bitcast(score, jnp.int32)
        key_ref[:, kb * t:(kb + 1) * t] = jnp.where(bits >= 0, bits, bits ^ jnp.int32(0x7FFFFFFF))

    def count(pred):
        return jnp.sum(jnp.where(pred, 1.0, 0.0), axis=-1, keepdims=True)

    int_min = jnp.int32(-2 ** 31)
    thr = jnp.full((r, 1), int_min, jnp.int32)
    cand = jnp.zeros((r, 1), jnp.int32)
    thr = jnp.where(count(key_ref[...] >= cand) >= topk, cand, thr)

    def value_bit(it, thr):
        cand = thr | lax.shift_left(jnp.int32(1), 30 - it)
        return jnp.where(count(key_ref[...] >= cand) >= topk, cand, thr)

    thr = lax.fori_loop(0, 31, value_bit, thr)

    keys = key_ref[...]
    n_above = count(keys > thr)
    ties_wanted = topk - n_above
    idx = lax.broadcasted_iota(jnp.int32, (r, n_keys), 1)
    tie = keys == thr
    n_bits = max(1, (n_keys - 1).bit_length())

    def index_bit(it, cut):
        cand = cut | lax.shift_left(jnp.int32(1), n_bits - 1 - it)
        before = count(jnp.logical_and(tie, idx < cand))
        return jnp.where(before <= ties_wanted - 1.0, cand, cut)

    cut = lax.fori_loop(0, n_bits, index_bit, jnp.zeros((r, 1), jnp.int32))
    for kb in range(n_keys // t):
        kk = key_ref[:, kb * t:(kb + 1) * t]
        idx_t = kb * t + col_t
        chosen = jnp.logical_or(kk > thr, jnp.logical_and(kk == thr, idx_t <= cut))
        valid = jnp.logical_and(chosen, admissible(kb))
        o_ref[kb] = jnp.where(valid, 0.0, NEG_INF).astype(BF16)


def _select_mask(qi_arr, qi_blk, wi_arr, ki_arr, *, batch, q_rows, n_keys, kpos0, qpos0, topk):
    r = min(SELECT_ROWS, q_rows)
    t = FLASH_TILE
    nq = q_rows // r
    cfg = (n_keys, kpos0, qpos0, topk)
    return pl.pallas_call(
        functools.partial(_select_body, cfg),
        grid=(batch, nq),
        in_specs=[
            pl.BlockSpec((r, C_IDX_HEADS * C_IDX_DIM), lambda b, i: (b * nq + i, qi_blk)),
            pl.BlockSpec((r, LANES), lambda b, i: (b * nq + i, 0)),
            pl.BlockSpec((n_keys, LANES), lambda b, i: (b, 0)),
        ],
        out_specs=pl.BlockSpec((None, n_keys // t, r, t), lambda b, i: (b * nq + i, 0, 0, 0)),
        out_shape=jax.ShapeDtypeStruct((batch * nq, n_keys // t, r, t), BF16),
        scratch_shapes=[pltpu.VMEM((r, n_keys), jnp.int32)],
        compiler_params=_params(2),
        name="select_mask",
    )(qi_arr, wi_arr, ki_arr)


def _oproj_body(x_ref, a_ref, w_ref, o_ref):
    o_ref[...] = x_ref[...] + jnp.dot(a_ref[...], w_ref[...], preferred_element_type=F32)


def _oproj_residual(x, attn, w_o, tm):
    n = x.shape[0]
    return pl.pallas_call(
        _oproj_body,
        grid=(n // tm,),
        in_specs=[
            pl.BlockSpec((tm, D_MODEL), lambda i: (i, 0)),
            pl.BlockSpec((tm, ATTN_WIDTH), lambda i: (i, 0)),
            _resident((ATTN_WIDTH, D_MODEL), lambda i: (0, 0)),
        ],
        out_specs=pl.BlockSpec((tm, D_MODEL), lambda i: (i, 0)),
        out_shape=jax.ShapeDtypeStruct((n, D_MODEL), F32),
        compiler_params=_params(1),
        name="oproj_residual",
    )(x, attn, w_o)


def _ffn_body(cfg, *refs):
    seq_tiles, seq_rows, stream, final_norm = cfg
    it = iter(refs)
    x_ref = next(it)
    if stream:
        xprev_ref = next(it)
    else:
        fix1_ref, fix2_ref = next(it), next(it)
    g_ref, win_ref, cw_ref, cb_ref, wout_ref = next(it), next(it), next(it), next(it), next(it)
    gfin_ref = next(it) if final_norm else None
    o_ref, st_ref = next(it), next(it)
    hext_ref, act_ref = next(it), next(it)
    tm = x_ref.shape[0]
    pad = SUBLANES
    x = x_ref[...]
    gain = g_ref[...]
    hext_ref[pad:, :] = _rms(x, gain).astype(BF16)
    if stream:
        not_start = ((pl.program_id(0) % seq_tiles) != 0).astype(F32)
        hext_ref[0:pad, :] = (_rms(xprev_ref[...], gain) * not_start).astype(BF16)
    else:
        hext_ref[0:pad, :] = jnp.zeros((pad, D_MODEL), BF16)
    hext = hext_ref[...]
    h = hext[pad:, :]
    is_seq_end = (pl.program_id(0) % seq_tiles) == seq_tiles - 1
    if not stream:
        assert seq_rows & (seq_rows - 1) == 0
        rmod = lax.broadcasted_iota(jnp.int32, (tm, FF_CHUNK), 0) & (seq_rows - 1)
    for c0 in range(0, D_FF, FF_CHUNK):
        a_ext = jnp.dot(hext, win_ref[:, c0:c0 + FF_CHUNK], preferred_element_type=F32)
        gate = jnp.dot(h, win_ref[:, D_FF + c0:D_FF + c0 + FF_CHUNK], preferred_element_type=F32)
        a = a_ext[pad:, :]
        a1 = pltpu.roll(a_ext, 1, 0)[pad:, :]
        a2 = pltpu.roll(a_ext, 2, 0)[pad:, :]
        if not stream:
            a1 = jnp.where(rmod == 0, fix1_ref[:, c0:c0 + FF_CHUNK], a1)
            a2 = jnp.where(rmod <= 1, fix2_ref[:, c0:c0 + FF_CHUNK], a2)
        cw = cw_ref[:, c0:c0 + FF_CHUNK]
        u = cb_ref[:, c0:c0 + FF_CHUNK] + (cw[0:1] * a2 + cw[1:2] * a1 + cw[2:3] * a)
        act_ref[:, c0:c0 + FF_CHUNK] = (jax.nn.silu(u) * gate).astype(BF16)
        if stream:
            @pl.when(is_seq_end)
            def _():
                st_ref[:, c0:c0 + FF_CHUNK] = a[tm - pad:, :]
        else:
            st_ref[:, c0:c0 + FF_CHUNK] = a
    y = x + jnp.dot(act_ref[...], wout_ref[...], preferred_element_type=F32)
    if final_norm:
        y = _rms(y, gfin_ref[...])
    o_ref[...] = y


def _conv_ffn(x, gain, w_in, conv_w, conv_b, w_out, *, tm, seq_tiles, seq_rows, fixes=None,
              final_gain=None):
    n = x.shape[0]
    stream = fixes is None
    final_norm = final_gain is not None
    cfg = (seq_tiles, seq_rows, stream, final_norm)
    in_specs = [pl.BlockSpec((tm, D_MODEL), lambda i: (i, 0))]
    args = [x]
    if stream:
        per = tm // SUBLANES
        in_specs.append(pl.BlockSpec((SUBLANES, D_MODEL), lambda i: (jnp.maximum(i * per - 1, 0), 0)))
        args.append(x)
    else:
        in_specs += [pl.BlockSpec((tm, D_FF), lambda i: (i, 0))] * 2
        args += list(fixes)
    in_specs += [
        _resident((1, D_MODEL), lambda i: (0, 0)),
        _resident((D_MODEL, 2 * D_FF), lambda i: (0, 0)),
        _resident((CONV_W, D_FF), lambda i: (0, 0)),
        _resident((1, D_FF), lambda i: (0, 0)),
        _resident((D_FF, D_MODEL), lambda i: (0, 0)),
    ]
    args += [gain.reshape(1, D_MODEL), w_in, conv_w, conv_b.reshape(1, D_FF), w_out]
    if final_norm:
        in_specs.append(_resident((1, D_MODEL), lambda i: (0, 0)))
        args.append(final_gain.reshape(1, D_MODEL))
    if stream:
        st_shape = jax.ShapeDtypeStruct((n // (tm * seq_tiles) * SUBLANES, D_FF), F32)
        st_spec = pl.BlockSpec((SUBLANES, D_FF), lambda i: (i // seq_tiles, 0))
    else:
        st_shape = jax.ShapeDtypeStruct((n, D_FF), F32)
        st_spec = pl.BlockSpec((tm, D_FF), lambda i: (i, 0))
    return pl.pallas_call(
        functools.partial(_ffn_body, cfg),
        grid=(n // tm,),
        in_specs=in_specs,
        out_specs=[pl.BlockSpec((tm, D_MODEL), lambda i: (i, 0)), st_spec],
        out_shape=[jax.ShapeDtypeStruct((n, D_MODEL), F32), st_shape],
        scratch_shapes=[pltpu.VMEM((tm + SUBLANES, D_MODEL), BF16), pltpu.VMEM((tm, D_FF), BF16)],
        compiler_params=_params(1),
        name="conv_ffn",
    )(*args)


def _t5_bucket(rel):
    half = NUM_BUCKETS // 2
    max_exact = half // 2
    base = jnp.where(rel > 0, half, 0)
    n = jnp.abs(rel)
    nf = jnp.maximum(n, 1).astype(F32)
    large = max_exact + (jnp.log(nf / max_exact) / math.log(T5_MAX_DISTANCE / max_exact)
                         * (half - max_exact)).astype(jnp.int32)
    large = jnp.minimum(large, half - 1)
    return base + jnp.where(n < max_exact, n, large)


def _t5_bias(table, rel):
    return jnp.moveaxis(table.astype(F32)[_t5_bucket(rel)], -1, 0)


def _band_rel(win, n_past_eff):
    i = jnp.arange(CHUNK)[:, None]
    j = jnp.arange(win)[None, :]
    return (j - n_past_eff) - i


def _flash_rel():
    t = FLASH_TILE
    r = jnp.arange(t)[:, None]
    c = jnp.arange(t)[None, :]
    return jnp.stack([c - r - d * t for d in range(3)])


def _perm_cols(w, perm):
    return w.reshape(w.shape[0], len(perm), HEAD_DIM)[:, perm, :].reshape(w.shape[0], -1)


def _perm_rows(w, perm):
    return w.reshape(len(perm), HEAD_DIM, w.shape[1])[perm, :, :].reshape(-1, w.shape[1])


def _pad_keys(cache, new, pad, width):
    b = cache.shape[0]
    allk = jnp.concatenate([cache.reshape(b, -1, width), new.reshape(b, -1, width)], axis=1)
    padded = jnp.pad(allk, ((0, 0), (pad, 0), (0, 0))).astype(BF16)
    return allk, padded.reshape(-1, width)


def kernel(x_prompt, x_sample, cache_a_k, cache_a_v, cache_b_k, cache_b_v, cache_c_k, cache_c_v,
           cache_c_kidx, cache_d_k, cache_d_v, state_ffn_conv, t5_table, norm_mix, norm_ffn,
           norm_final, a_w_qkv, a_w_o, a_rel_bias, b_w_qkv, b_w_o, b_sinks, c_w_qkv, c_w_o,
           c_w_idx_q, c_w_idx_k, c_idx_k_norm, c_w_idx_w, d_w_qkv, d_w_o, d_lambda_q1,
           d_lambda_k1, d_lambda_q2, d_lambda_k2, d_subln, ffn_w_in, ffn_conv_w, ffn_conv_b,
           ffn_w_out):
    bp, seq, d = x_prompt.shape
    bs, ts, _ = x_sample.shape
    past = cache_c_k.shape[1]
    assert d == D_MODEL and ts == CHUNK and seq % ROW_TILE == 0 and past % CHUNK == 0
    n_p, n_s = bp * seq, bs * ts
    seq_tiles = seq // ROW_TILE
    depth = norm_mix.shape[0]
    scale = HEAD_DIM ** -0.5
    perm = jnp.array(GQA_PERM)
    none_aux = jnp.zeros((1, LANES), F32)
    kvw = KV_HEADS * HEAD_DIM

    xp = x_prompt.reshape(n_p, d)
    xs = x_sample.reshape(n_s, d)

    def cast_dests(col_dests):
        secs = []
        for c0, c1, fn in col_dests:
            for c in range(c0, c1, 2 * LANES):
                secs.append((c, min(2 * LANES, c1 - c), fn(c)))
        return secs

    layer = 0
    w = a_w_qkv.astype(BF16)
    aw = ATTN_WIDTH
    a_keep = min(A_PAST, seq)
    secs_p = cast_dests([
        (0, aw, lambda c: [("cast", 0, c, scale)]),
        (aw, 2 * aw, lambda c: [("cast", 0, c, None), ("tail", 1, c - aw, a_keep)]),
        (2 * aw, 3 * aw, lambda c: [("cast", 0, c, None), ("tail", 2, c - 2 * aw, a_keep)]),
    ])
    qkv, a_k_p, a_v_p = _norm_proj(
        xp, norm_mix[layer], w, none_aux, secs_p,
        [(n_p, 3 * aw, BF16, ROW_TILE, "all"), (bp * a_keep, aw, F32, a_keep, "tail"),
         (bp * a_keep, aw, F32, a_keep, "tail")], seq_tiles, ROW_TILE)
    secs_s = cast_dests([
        (0, aw, lambda c: [("cast", 0, c, scale)]),
        (aw, 3 * aw, lambda c: [("cast", 1, c - aw, None)]),
    ])
    q_s, kv_s = _norm_proj(xs, norm_mix[layer], w, none_aux, secs_s,
                           [(n_s, aw, BF16, n_s, "all"), (n_s, 2 * aw, F32, n_s, "all")], 1, n_s)
    a_pad = CHUNK
    a_win = A_PAST + CHUNK + a_pad
    rel = _band_rel(a_win, A_PAST + a_pad)
    bias_a = jnp.moveaxis(a_rel_bias.astype(F32)[jnp.clip(rel, -A_CLIP, A_CLIP) + A_CLIP], -1, 0)
    bias_a = jnp.where((jnp.arange(a_win) >= a_pad)[None, None, :], bias_a, NEG_INF)
    att_p = _band_attention(qkv, 0, qkv, aw // LANES, qkv, 2 * aw // LANES, bias_a, batch=bp,
                            q_rows=seq, k_rows=seq, n_kv_pairs=N_HEADS // 2, n_pairs=1,
                            win=a_win, n_past=A_PAST + a_pad)
    ks = kv_s[:, :aw].reshape(bs, ts, aw)
    vs = kv_s[:, aw:].reshape(bs, ts, aw)
    k_all, k_in = _pad_keys(cache_a_k, ks, a_pad, aw)
    v_all, v_in = _pad_keys(cache_a_v, vs, a_pad, aw)
    assert k_all.shape[1] + a_pad == a_win
    att_s = _band_attention(q_s, 0, k_in, 0, v_in, 0, bias_a, batch=bs, q_rows=ts, k_rows=a_win,
                            n_kv_pairs=N_HEADS // 2, n_pairs=1, win=a_win, n_past=0)
    a_keep_s = min(A_PAST, k_all.shape[1])
    a_k_prompt = a_k_p.reshape(bp, a_keep, N_HEADS, HEAD_DIM)
    a_v_prompt = a_v_p.reshape(bp, a_keep, N_HEADS, HEAD_DIM)
    a_k_sample = k_all[:, -a_keep_s:].reshape(bs, a_keep_s, N_HEADS, HEAD_DIM)
    a_v_sample = v_all[:, -a_keep_s:].reshape(bs, a_keep_s, N_HEADS, HEAD_DIM)
    w_o = a_w_o.astype(BF16)
    xp = _oproj_residual(xp, att_p, w_o, ROW_TILE)
    xs = _oproj_residual(xs, att_s, w_o, n_s)
    xp, xs, conv_p0, conv_s0 = _ffn_layer(xp, xs, layer, bp, bs, seq_tiles, ts, norm_ffn,
                                          ffn_w_in, ffn_conv_w, ffn_conv_b, ffn_w_out,
                                          state_ffn_conv, None)

    layer = 1
    w = jnp.concatenate([_perm_cols(b_w_qkv[:, :aw], perm), b_w_qkv[:, aw:]], axis=1).astype(BF16)
    b_keep = min(B_WINDOW, seq)
    secs_p = cast_dests([
        (0, aw, lambda c: [("cast", 0, c, scale)]),
        (aw, aw + kvw, lambda c: [("cast", 0, c, None), ("tail", 1, c - aw, b_keep)]),
        (aw + kvw, aw + 2 * kvw, lambda c: [("cast", 0, c, None), ("tail", 2, c - aw - kvw, b_keep)]),
    ])
    qkv, b_k_p, b_v_p = _norm_proj(
        xp, norm_mix[layer], w, none_aux, secs_p,
        [(n_p, aw + 2 * kvw, BF16, ROW_TILE, "all"), (bp * b_keep, kvw, F32, b_keep, "tail"),
         (bp * b_keep, kvw, F32, b_keep, "tail")], seq_tiles, ROW_TILE)
    secs_s = cast_dests([
        (0, aw, lambda c: [("cast", 0, c, scale)]),
        (aw, aw + 2 * kvw, lambda c: [("cast", 1, c - aw, None)]),
    ])
    q_s, kv_s = _norm_proj(xs, norm_mix[layer], w, none_aux, secs_s,
                           [(n_s, aw, BF16, n_s, "all"), (n_s, 2 * kvw, F32, n_s, "all")], 1, n_s)
    b_pad = CHUNK
    b_win = B_WINDOW + CHUNK + b_pad
    bias_b = _t5_bias(t5_table, _band_rel(b_win, B_WINDOW + b_pad))[perm]
    bias_b = jnp.where((jnp.arange(b_win) >= b_pad)[None, None, :], bias_b, NEG_INF)
    sinks = b_sinks.astype(F32)[perm]
    n_kvp = KV_HEADS // 2
    gq = N_HEADS // KV_HEADS
    att_p = _band_attention(qkv, 0, qkv, aw // LANES, qkv, (aw + kvw) // LANES, bias_b, batch=bp,
                            q_rows=seq, k_rows=seq, n_kv_pairs=n_kvp, n_pairs=gq, win=b_win,
                            n_past=B_WINDOW + b_pad, mode="sink", sinks=sinks)
    ks = kv_s[:, :kvw].reshape(bs, ts, kvw)
    vs = kv_s[:, kvw:].reshape(bs, ts, kvw)
    k_all, k_in = _pad_keys(cache_b_k, ks, b_pad, kvw)
    v_all, v_in = _pad_keys(cache_b_v, vs, b_pad, kvw)
    assert k_all.shape[1] + b_pad == b_win
    att_s = _band_attention(q_s, 0, k_in, 0, v_in, 0, bias_b, batch=bs, q_rows=ts, k_rows=b_win,
                            n_kv_pairs=n_kvp, n_pairs=gq, win=b_win, n_past=0, mode="sink",
                            sinks=sinks)
    b_keep_s = min(B_WINDOW, k_all.shape[1])
    b_k_prompt = b_k_p.reshape(bp, b_keep, KV_HEADS, HEAD_DIM)
    b_v_prompt = b_v_p.reshape(bp, b_keep, KV_HEADS, HEAD_DIM)
    b_k_sample = k_all[:, -b_keep_s:].reshape(bs, b_keep_s, KV_HEADS, HEAD_DIM)
    b_v_sample = v_all[:, -b_keep_s:].reshape(bs, b_keep_s, KV_HEADS, HEAD_DIM)
    w_o = _perm_rows(b_w_o, perm).astype(BF16)
    xp = _oproj_residual(xp, att_p, w_o, ROW_TILE)
    xs = _oproj_residual(xs, att_s, w_o, n_s)
    xp, xs, conv_p1, conv_s1 = _ffn_layer(xp, xs, layer, bp, bs, seq_tiles, ts, norm_ffn,
                                          ffn_w_in, ffn_conv_w, ffn_conv_b, ffn_w_out,
                                          state_ffn_conv, None)

    layer = 2
    iw = C_IDX_HEADS * C_IDX_DIM
    w_idx_w = jnp.pad(c_w_idx_w, ((0, 0), (0, LANES - C_IDX_HEADS)))
    w = jnp.concatenate([_perm_cols(c_w_qkv[:, :aw], perm), c_w_qkv[:, aw:], c_w_idx_q,
                         c_w_idx_k, c_w_idx_k, w_idx_w], axis=1).astype(BF16)
    c_qkv_w = aw + 2 * kvw
    col_ki = c_qkv_w + iw
    col_wi = col_ki + LANES
    knorm = jnp.concatenate([c_idx_k_norm, c_idx_k_norm]).astype(F32).reshape(1, LANES)
    wi_scale = C_IDX_HEADS ** -0.5

    def c_sections(k_out, v_out, qkv_out):
        secs = cast_dests([
            (0, aw, lambda c: [("cast", qkv_out, c, scale)]),
            (aw, aw + kvw, lambda c: [("cast", qkv_out, c, None), ("cast", k_out, c - aw, None)]),
            (aw + kvw, c_qkv_w,
             lambda c: [("cast", qkv_out, c, None), ("cast", v_out, c - aw - kvw, None)]),
            (c_qkv_w, col_ki, lambda c: [("cast", qkv_out, c, C_IDX_DIM ** -0.5)]),
        ])
        secs.append((col_ki, LANES, [("kidx", 3, 4)]))
        secs.append((col_wi, LANES, [("cast", 5, 0, wi_scale)]))
        return secs

    def c_outs(n, tm):
        return [(n, c_qkv_w + iw, BF16, tm, "all"), (n, kvw, F32, tm, "all"),
                (n, kvw, F32, tm, "all"), (n, LANES, BF16, tm, "all"),
                (n, C_IDX_DIM, F32, tm, "all"), (n, LANES, F32, tm, "all")]

    qkv, c_k_p, c_v_p, ki_p, kidx_p, wi_p = _norm_proj(
        xp, norm_mix[layer], w, knorm, c_sections(1, 2, 0), c_outs(n_p, ROW_TILE), seq_tiles,
        ROW_TILE)
    qkv_s, c_k_s, c_v_s, ki_s, kidx_s, wi_s = _norm_proj(
        xs, norm_mix[layer], w, knorm, c_sections(1, 2, 0), c_outs(n_s, n_s), 1, n_s)
    qi_blk = c_qkv_w // iw
    assert qi_blk * iw == c_qkv_w
    t = FLASH_TILE
    assert SELECT_ROWS == t
    sel_p = _select_mask(qkv, qi_blk, wi_p, ki_p, batch=bp, q_rows=seq, n_keys=seq, kpos0=0,
                         qpos0=0, topk=min(C_TOPK, seq // 4))
    bias_c = _t5_bias(t5_table, _flash_rel())[perm]
    bias_c = jnp.moveaxis(bias_c, 0, 1)
    att_p = _flash_attention(qkv, 0, qkv, aw // LANES, qkv, (aw + kvw) // LANES, bias_c, batch=bp,
                             seq=seq, n_kv_pairs=n_kvp, n_pairs=gq, sel=sel_p)
    n_keys_s = past + ts
    c_pad = (-n_keys_s) % t
    c_win = n_keys_s + c_pad
    k_all, k_in = _pad_keys(cache_c_k, c_k_s.reshape(bs, ts, kvw), c_pad, kvw)
    v_all, v_in = _pad_keys(cache_c_v, c_v_s.reshape(bs, ts, kvw), c_pad, kvw)
    del ki_s
    _, ki_in = _pad_keys(cache_c_kidx, kidx_s.reshape(bs, ts, C_IDX_DIM), c_pad, C_IDX_DIM)
    ki_in = jnp.concatenate([ki_in, ki_in], axis=1)
    sel_s = _select_mask(qkv_s, qi_blk, wi_s, ki_in, batch=bs, q_rows=ts, n_keys=c_win,
                         kpos0=-c_pad, qpos0=past, topk=min(C_TOPK, n_keys_s // 4))
    sel_s = sel_s.transpose(0, 2, 1, 3).reshape(bs, ts, c_win)
    rel_s = (jnp.arange(c_win)[None, :] - c_pad) - (past + jnp.arange(ts)[:, None])
    pad_ok = (jnp.arange(c_win) >= c_pad)[None, None, :]
    bias_s_t5 = jnp.where(pad_ok, _t5_bias(t5_table, rel_s), NEG_INF)
    att_s = _band_attention(qkv_s, 0, k_in, 0, v_in, 0, bias_s_t5[perm], batch=bs, q_rows=ts,
                            k_rows=c_win, n_kv_pairs=n_kvp, n_pairs=gq, win=c_win, n_past=0,
                            sel=sel_s)
    c_k_prompt = c_k_p.reshape(bp, seq, KV_HEADS, HEAD_DIM)
    c_v_prompt = c_v_p.reshape(bp, seq, KV_HEADS, HEAD_DIM)
    c_kidx_prompt = kidx_p.reshape(bp, seq, C_IDX_DIM)
    c_k_sample = c_k_s.reshape(bs, ts, KV_HEADS, HEAD_DIM)
    c_v_sample = c_v_s.reshape(bs, ts, KV_HEADS, HEAD_DIM)
    c_kidx_sample = kidx_s.reshape(bs, ts, C_IDX_DIM)
    w_o = _perm_rows(c_w_o, perm).astype(BF16)
    xp = _oproj_residual(xp, att_p, w_o, ROW_TILE)
    xs = _oproj_residual(xs, att_s, w_o, n_s)
    xp, xs, conv_p2, conv_s2 = _ffn_layer(xp, xs, layer, bp, bs, seq_tiles, ts, norm_ffn,
                                          ffn_w_in, ffn_conv_w, ffn_conv_b, ffn_w_out,
                                          state_ffn_conv, None)

    layer = 3
    lambda_init = 0.8 - 0.6 * math.exp(-0.3 * layer)
    w = d_w_qkv.astype(BF16)
    lam_vecs = jnp.stack([d_lambda_q1, d_lambda_k1, d_lambda_q2, d_lambda_k2]).astype(F32)
    secs = cast_dests([
        (0, aw, lambda c: [("cast", 0, c, scale)]),
        (aw, 2 * aw, lambda c: [("cast", 0, c, None), ("cast", 1, c - aw, None)]),
        (2 * aw, 3 * aw, lambda c: [("cast", 0, c, None), ("cast", 2, c - 2 * aw, None)]),
    ])

    def d_outs(n, tm):
        return [(n, 3 * aw, BF16, tm, "all"), (n, aw, F32, tm, "all"), (n, aw, F32, tm, "all")]

    qkv, d_k_p, d_v_p = _norm_proj(xp, norm_mix[layer], w, none_aux, secs, d_outs(n_p, ROW_TILE),
                                   seq_tiles, ROW_TILE)
    qkv_s, d_k_s, d_v_s = _norm_proj(xs, norm_mix[layer], w, none_aux, secs, d_outs(n_s, n_s), 1,
                                     n_s)
    bias_d = jnp.moveaxis(_t5_bias(t5_table, _flash_rel()), 0, 1)
    att_p = _flash_attention(qkv, 0, qkv, aw // LANES, qkv, 2 * aw // LANES, bias_d, batch=bp,
                             seq=seq, n_kv_pairs=D_HEADS, n_pairs=1, mode="diff",
                             lam_vecs=lam_vecs, subln=d_subln.astype(F32), lambda_init=lambda_init)
    k_all, k_in = _pad_keys(cache_d_k, d_k_s.reshape(bs, ts, aw), c_pad, aw)
    v_all, v_in = _pad_keys(cache_d_v, d_v_s.reshape(bs, ts, aw), c_pad, aw)
    att_s = _band_attention(qkv_s, 0, k_in, 0, v_in, 0, bias_s_t5, batch=bs, q_rows=ts,
                            k_rows=c_win, n_kv_pairs=D_HEADS, n_pairs=1, win=c_win, n_past=0,
                            mode="diff", lam_vecs=lam_vecs, subln=d_subln.astype(F32),
                            lambda_init=lambda_init)
    d_k_prompt = d_k_p.reshape(bp, seq, 2 * D_HEADS, HEAD_DIM)
    d_v_prompt = d_v_p.reshape(bp, seq, D_HEADS, 2 * HEAD_DIM)
    d_k_sample = d_k_s.reshape(bs, ts, 2 * D_HEADS, HEAD_DIM)
    d_v_sample = d_v_s.reshape(bs, ts, D_HEADS, 2 * HEAD_DIM)
    w_o = d_w_o.astype(BF16)
    xp = _oproj_residual(xp, att_p, w_o, ROW_TILE)
    xs = _oproj_residual(xs, att_s, w_o, n_s)
    xp, xs, conv_p3, conv_s3 = _ffn_layer(xp, xs, layer, bp, bs, seq_tiles, ts, norm_ffn,
                                          ffn_w_in, ffn_conv_w, ffn_conv_b, ffn_w_out,
                                          state_ffn_conv, norm_final)
    assert depth == 4

    y_prompt = xp.reshape(bp, seq, d)
    y_sample = xs.reshape(bs, ts, d)
    ffn_conv_prompt = jnp.stack([conv_p0, conv_p1, conv_p2, conv_p3])
    ffn_conv_sample = jnp.stack([conv_s0, conv_s1, conv_s2, conv_s3])
    return (y_prompt, y_sample,
            a_k_prompt, a_v_prompt, a_k_sample, a_v_sample,
            b_k_prompt, b_v_prompt, b_k_sample, b_v_sample,
            c_k_prompt, c_v_prompt, c_kidx_prompt, c_k_sample, c_v_sample, c_kidx_sample,
            d_k_prompt, d_v_prompt, d_k_sample, d_v_sample,
            ffn_conv_prompt, ffn_conv_sample)


def _ffn_layer(xp, xs, layer, bp, bs, seq_tiles, ts, norm_ffn, ffn_w_in, ffn_conv_w, ffn_conv_b,
               ffn_w_out, state, final_gain):
    w_in = ffn_w_in[layer].astype(BF16)
    w_out = ffn_w_out[layer].astype(BF16)
    cw, cb, gain = ffn_conv_w[layer], ffn_conv_b[layer], norm_ffn[layer]
    xp, tail = _conv_ffn(xp, gain, w_in, cw, cb, w_out, tm=ROW_TILE, seq_tiles=seq_tiles,
                         seq_rows=ROW_TILE * seq_tiles, final_gain=final_gain)
    conv_p = tail.reshape(bp, SUBLANES, D_FF)[:, SUBLANES - (CONV_W - 1):, :]
    n_s = bs * ts
    st = state[layer]
    zeros = jnp.zeros((bs, ts - 2, D_FF), F32)
    fix1 = jnp.concatenate([st[:, 1:2], jnp.zeros((bs, 1, D_FF), F32), zeros], axis=1)
    fix2 = jnp.concatenate([st, zeros], axis=1)
    xs, a_s = _conv_ffn(xs, gain, w_in, cw, cb, w_out, tm=n_s, seq_tiles=1, seq_rows=ts,
                        fixes=(fix1.reshape(n_s, D_FF), fix2.reshape(n_s, D_FF)),
                        final_gain=final_gain)
    conv_s = a_s.reshape(bs, ts, D_FF)[:, ts - (CONV_W - 1):, :]
    return xp, xs, conv_p, conv_s
```

```python
import functools
import math

import jax
import jax.numpy as jnp
from jax import lax
from jax.experimental import pallas as pl
from jax.experimental.pallas import tpu as pltpu

F32 = jnp.float32
BF16 = jnp.bfloat16

D_MODEL = 1024
CHUNK = 64
N_HEADS = 16
HEAD_DIM = 64
ATTN_WIDTH = N_HEADS * HEAD_DIM
NUM_BUCKETS = 32
T5_MAX_DISTANCE = 128
A_PAST = 512
A_CLIP = 64
B_WINDOW = 128
KV_HEADS = 4
C_TOPK = 256
C_IDX_HEADS = 8
C_IDX_DIM = 64
D_HEADS = 8
D_FF = 2816
CONV_W = 3
RMS_EPS = 1e-6
NEG_INF = -1e30

LANES = 128
SUBLANES = 8
ROW_TILE = 512
FLASH_TILE = 256
BAND_UNROLL = 4
SELECT_ROWS = FLASH_TILE
FF_CHUNK = 256
VMEM_LIMIT = 56 * 1024 * 1024

GQA_PERM = tuple(8 * j + 4 * half + t for j in range(2) for t in range(4) for half in range(2))


def _params(n_grid_dims):
    return pltpu.CompilerParams(
        dimension_semantics=("arbitrary",) * n_grid_dims, vmem_limit_bytes=VMEM_LIMIT)


def _resident(shape, index_map):
    return pl.BlockSpec(shape, index_map, pipeline_mode=pl.Buffered(1))


def _rms(x, gain):
    ms = jnp.mean(x * x, axis=-1, keepdims=True)
    return (x * lax.rsqrt(ms + RMS_EPS)) * gain


def _dot_nt(a, b):
    return lax.dot_general(a, b, (((1,), (1,)), ((), ())), preferred_element_type=F32)


def _norm_proj_body(sections, seq_tiles, n_out, x_ref, g_ref, w_ref, aux_ref, *out_refs):
    tm = x_ref.shape[0]
    h = _rms(x_ref[...], g_ref[...]).astype(BF16)
    is_seq_end = (pl.program_id(0) % seq_tiles) == seq_tiles - 1
    for col0, width, dests in sections:
        y = jnp.dot(h, w_ref[:, col0:col0 + width], preferred_element_type=F32)
        for dest in dests:
            kind = dest[0]
            if kind == "cast":
                _, o, oc, scale = dest
                val = y if scale is None else y * scale
                out_refs[o][:, oc:oc + width] = val.astype(out_refs[o].dtype)
            elif kind == "tail":
                _, o, oc, nrows = dest

                @pl.when(is_seq_end)
                def _():
                    out_refs[o][:, oc:oc + width] = y[tm - nrows:, :]
            elif kind == "kidx":
                _, o_dup, o_f32 = dest
                yn = _rms(y, aux_ref[...])
                out_refs[o_dup][...] = yn.astype(BF16)
                out_refs[o_f32][...] = yn[:, :C_IDX_DIM]
            else:
                raise ValueError(kind)


def _norm_proj(x, gain, w, aux, sections, outs, seq_tiles, tm):
    n = x.shape[0]
    grid = (n // tm,)
    out_shape, out_specs = [], []
    for rows, width, dtype, blk_rows, mode in outs:
        out_shape.append(jax.ShapeDtypeStruct((rows, width), dtype))
        if mode == "all":
            out_specs.append(pl.BlockSpec((blk_rows, width), lambda i: (i, 0)))
        else:
            out_specs.append(pl.BlockSpec((blk_rows, width), lambda i: (i // seq_tiles, 0)))
    body = functools.partial(_norm_proj_body, sections, seq_tiles, len(outs))
    return pl.pallas_call(
        body,
        grid=grid,
        in_specs=[
            pl.BlockSpec((tm, D_MODEL), lambda i: (i, 0)),
            _resident((1, D_MODEL), lambda i: (0, 0)),
            _resident(w.shape, lambda i: (0, 0)),
            _resident(aux.shape, lambda i: (0, 0)),
        ],
        out_specs=out_specs,
        out_shape=out_shape,
        compiler_params=_params(1),
        name="norm_proj",
    )(x, gain.reshape(1, D_MODEL), w, aux)


def _split_sections(col0, total, width, dests_fn):
    return [(col0 + c, width, dests_fn(c)) for c in range(0, total, width)]


def _softmax_rows(s, sink):
    m = jnp.max(s, axis=-1, keepdims=True)
    if sink is not None:
        m = jnp.maximum(m, sink)
    p = jnp.exp(s - m)
    l = jnp.sum(p, axis=-1, keepdims=True)
    if sink is not None:
        l = l + jnp.exp(sink - m)
    return p, l


def _band_body(cfg, *refs):
    n_pairs, win, n_past, n_chunks, unroll, mode, lambda_init, has_sel = cfg
    it = iter(refs)
    q_ref, k_ref, v_ref, bias_ref = next(it), next(it), next(it), next(it)
    sel_ref = next(it) if has_sel else None
    sink_ref = next(it) if mode == "sink" else None
    if mode == "diff":
        lam_ref, subln_ref = next(it), next(it)
    o_ref = next(it)
    if n_past:
        kp_ref, vp_ref = next(it), next(it)
        kp_ref[0:n_past, :] = jnp.zeros((n_past, LANES), BF16)
        vp_ref[0:n_past, :] = jnp.zeros((n_past, LANES), BF16)
        kp_ref[n_past:, :] = k_ref[...]
        vp_ref[n_past:, :] = v_ref[...]
    else:
        kp_ref, vp_ref = k_ref, v_ref

    n_heads = 2 * n_pairs
    rows = n_heads * CHUNK
    lane = lax.broadcasted_iota(jnp.int32, (CHUNK, LANES), 1)
    col = lax.broadcasted_iota(jnp.int32, (rows, win), 1)
    low = lane < HEAD_DIM
    if mode == "diff":
        lv = lam_ref[...]
        lam = (jnp.exp(jnp.sum(lv[0:1] * lv[1:2], axis=-1, keepdims=True))
               - jnp.exp(jnp.sum(lv[2:3] * lv[3:4], axis=-1, keepdims=True)) + lambda_init)

    def logits(c):
        r0 = pl.multiple_of(c * CHUNK, CHUNK)
        kwin = kp_ref[pl.ds(r0, win), :]
        qs = []
        for p in range(n_pairs):
            qp = q_ref[pl.ds(r0, CHUNK), p * LANES:(p + 1) * LANES]
            qs += [jnp.where(low, qp, jnp.zeros_like(qp)), jnp.where(low, jnp.zeros_like(qp), qp)]
        s = _dot_nt(jnp.concatenate(qs, axis=0), kwin) + bias_ref[...].reshape(rows, win)
        if n_past:
            s = jnp.where(col >= n_past - c * CHUNK, s, NEG_INF)
        if has_sel:
            s = (s.reshape(n_heads, CHUNK, win) + sel_ref[...].astype(F32)[None]).reshape(rows, win)
        return s

    def attend(c, pr, l):
        r0 = pl.multiple_of(c * CHUNK, CHUNK)
        vwin = vp_ref[pl.ds(r0, win), :]
        if mode == "diff":
            pn = pr / l
            a = (pn[0:CHUNK] - lam * pn[CHUNK:2 * CHUNK]).astype(BF16)
            o = jnp.dot(a, vwin, preferred_element_type=F32)
            o_ref[pl.ds(r0, CHUNK), :] = (_rms(o, subln_ref[...]) * (1.0 - lambda_init)).astype(BF16)
            return
        o = jnp.dot(pr.astype(BF16), vwin, preferred_element_type=F32) / l
        for p in range(n_pairs):
            lo = o[2 * p * CHUNK:(2 * p + 1) * CHUNK]
            hi = o[(2 * p + 1) * CHUNK:(2 * p + 2) * CHUNK]
            o_ref[pl.ds(r0, CHUNK), p * LANES:(p + 1) * LANES] = jnp.where(low, lo, hi).astype(BF16)

    sink = sink_ref[...] if mode == "sink" else None

    def chunks(cc, carry):
        ids = [cc * unroll + u for u in range(unroll)]
        ss = [logits(c) for c in ids]
        pls = [_softmax_rows(s, sink) for s in ss]
        for c, (pr, l) in zip(ids, pls):
            attend(c, pr, l)
        return carry

    if n_chunks == unroll:
        chunks(0, 0)
    else:
        lax.fori_loop(0, n_chunks // unroll, chunks, 0)


def _band_attention(q_arr, q_blk0, k_arr, k_blk0, v_arr, v_blk0, bias, *, batch, q_rows, k_rows,
                    n_kv_pairs, n_pairs, win, n_past, mode="plain", sel=None, sinks=None,
                    lam_vecs=None, subln=None, lambda_init=0.0):
    n_chunks = q_rows // CHUNK
    qw = n_pairs * LANES
    rows = 2 * n_pairs * CHUNK
    unroll = math.gcd(n_chunks, BAND_UNROLL)
    cfg = (n_pairs, win, n_past, n_chunks, unroll, mode, lambda_init, sel is not None)
    in_specs = [
        pl.BlockSpec((q_rows, qw), lambda b, j: (b, q_blk0 + j)),
        pl.BlockSpec((k_rows, LANES), lambda b, j: (b, k_blk0 + j)),
        pl.BlockSpec((k_rows, LANES), lambda b, j: (b, v_blk0 + j)),
        pl.BlockSpec((2 * n_pairs, CHUNK, win), lambda b, j: (j, 0, 0)),
    ]
    args = [q_arr, k_arr, v_arr, bias]
    if sel is not None:
        in_specs.append(pl.BlockSpec((None, CHUNK, win), lambda b, j: (b, 0, 0)))
        args.append(sel)
    if mode == "sink":
        in_specs.append(pl.BlockSpec((rows, 1), lambda b, j: (j, 0)))
        args.append(jnp.repeat(sinks, CHUNK).reshape(n_kv_pairs * rows, 1))
    if mode == "diff":
        in_specs.append(pl.BlockSpec((4, HEAD_DIM), lambda b, j: (0, 0)))
        in_specs.append(pl.BlockSpec((1, LANES), lambda b, j: (0, 0)))
        args += [lam_vecs, subln.reshape(1, LANES)]
    scratch = []
    if n_past:
        scratch = [pltpu.VMEM((n_past + k_rows, LANES), BF16)] * 2
    return pl.pallas_call(
        functools.partial(_band_body, cfg),
        grid=(batch, n_kv_pairs),
        in_specs=in_specs,
        out_specs=pl.BlockSpec((q_rows, qw), lambda b, j: (b, j)),
        out_shape=jax.ShapeDtypeStruct((batch * q_rows, n_kv_pairs * qw), BF16),
        scratch_shapes=scratch,
        compiler_params=_params(2),
        name="band_attention",
    )(*args)


def _flash_t_body(cfg, *refs):
    n_groups, n_pairs, mode, lambda_init, has_sel = cfg
    it = iter(refs)
    q_ref, k_ref, v_ref, bias_ref = next(it), next(it), next(it), next(it)
    sel_ref = next(it) if has_sel else None
    if mode == "diff":
        lam_ref, subln_ref = next(it), next(it)
    o_ref = next(it)
    qs_ref, vt_ref, m_ref, l_ref, acc_ref = next(it), next(it), next(it), next(it), next(it)
    t = FLASH_TILE
    n_heads = 2 * n_pairs
    rows = n_heads * t
    seq = k_ref.shape[0]
    i = pl.program_id(2)
    lane = lax.broadcasted_iota(jnp.int32, (t, LANES), 1)
    low = lane < HEAD_DIM
    krow = lax.broadcasted_iota(jnp.int32, (t, rows), 0)
    qcol = lax.broadcasted_iota(jnp.int32, (t, rows), 1) & (t - 1)
    diag_ok = lax.shift_right_arithmetic(krow, 6) <= lax.shift_right_arithmetic(qcol, 6)

    @pl.when(i == 0)
    def _():
        for g in range(n_groups):
            for r0 in range(0, seq, t):
                vb = v_ref[r0:r0 + t, g * LANES:(g + 1) * LANES].astype(F32)
                vt_ref[g, :, r0:r0 + t] = vb.T.astype(BF16)

    for g in range(n_groups):
        qs = []
        for p in range(n_pairs):
            qp = q_ref[:, (g * n_pairs + p) * LANES:(g * n_pairs + p + 1) * LANES]
            qs += [jnp.where(low, qp, jnp.zeros_like(qp)), jnp.where(low, jnp.zeros_like(qp), qp)]
        qs_ref[g] = jnp.concatenate(qs, axis=0)
    m_ref[...] = jnp.full(m_ref.shape, NEG_INF, F32)
    l_ref[...] = jnp.zeros(l_ref.shape, F32)
    acc_ref[...] = jnp.zeros(acc_ref.shape, F32)

    def kstep(j, carry):
        r0 = pl.multiple_of(j * t, t)
        tt = jnp.minimum(i - j, 2)
        off_diag = j < i
        ss = []
        for g in range(n_groups):
            kb = k_ref[pl.ds(r0, t), g * LANES:(g + 1) * LANES]
            s = _dot_nt(kb, qs_ref[g]) + bias_ref[tt, :, g * rows:(g + 1) * rows]
            s = jnp.where(jnp.logical_or(off_diag, diag_ok), s, NEG_INF)
            if has_sel:
                sel = sel_ref[j].astype(F32)
                s = s + jnp.concatenate([sel] * n_heads, axis=1)
            ss.append(s)
        prs = []
        for g in range(n_groups):
            m_old = m_ref[g]
            m_new = jnp.maximum(m_old, jnp.max(ss[g], axis=0, keepdims=True))
            alpha = jnp.exp(m_old - m_new)
            pr = jnp.exp(ss[g] - m_new)
            l_ref[g] = alpha * l_ref[g] + jnp.sum(pr, axis=0, keepdims=True)
            m_ref[g] = m_new
            prs.append((pr.astype(BF16), alpha))
        for g in range(n_groups):
            pr, alpha = prs[g]
            vtb = vt_ref[g, :, pl.ds(r0, t)]
            acc_ref[g] = alpha * acc_ref[g] + jnp.dot(vtb, pr, preferred_element_type=F32)
        return carry

    lax.fori_loop(0, i + 1, kstep, 0)

    if mode == "diff":
        lv = lam_ref[...]
        lam = (jnp.exp(jnp.sum(lv[0:1] * lv[1:2], axis=-1, keepdims=True))
               - jnp.exp(jnp.sum(lv[2:3] * lv[3:4], axis=-1, keepdims=True)) + lambda_init)
    drow_low = lax.broadcasted_iota(jnp.int32, (LANES, t), 0) < HEAD_DIM
    for g in range(n_groups):
        res = acc_ref[g] / l_ref[g]
        for p in range(n_pairs):
            lo = res[:, 2 * p * t:(2 * p + 1) * t]
            hi = res[:, (2 * p + 1) * t:(2 * p + 2) * t]
            if mode == "diff":
                o = _rms((lo - lam * hi).T, subln_ref[...]) * (1.0 - lambda_init)
            else:
                o = jnp.where(drow_low, lo, hi).T
            o_ref[:, (g * n_pairs + p) * LANES:(g * n_pairs + p + 1) * LANES] = o.astype(BF16)


def _flash_attention(q_arr, q_blk0, k_arr, k_blk0, v_arr, v_blk0, bias, *, batch, seq, n_kv_pairs,
                     n_groups, n_pairs, mode="plain", sel=None, lam_vecs=None, subln=None,
                     lambda_init=0.0):
    t = FLASH_TILE
    nq = seq // t
    qw = n_groups * n_pairs * LANES
    kw = n_groups * LANES
    rows = 2 * n_pairs * t
    n_steps = n_kv_pairs // n_groups
    cfg = (n_groups, n_pairs, mode, lambda_init, sel is not None)
    in_specs = [
        pl.BlockSpec((t, qw), lambda b, j, i: (b * nq + i, q_blk0 + j)),
        pl.BlockSpec((seq, kw), lambda b, j, i: (b, k_blk0 + j)),
        pl.BlockSpec((seq, kw), lambda b, j, i: (b, v_blk0 + j)),
        pl.BlockSpec((3, t, n_groups * rows), lambda b, j, i: (0, 0, j)),
    ]
    args = [q_arr, k_arr, v_arr, bias]
    if sel is not None:
        in_specs.append(pl.BlockSpec((None, nq, t, t), lambda b, j, i: (b * nq + i, 0, 0, 0)))
        args.append(sel)
    if mode == "diff":
        in_specs.append(pl.BlockSpec((4, HEAD_DIM), lambda b, j, i: (0, 0)))
        in_specs.append(pl.BlockSpec((1, LANES), lambda b, j, i: (0, 0)))
        args += [lam_vecs, subln.reshape(1, LANES)]
    return pl.pallas_call(
        functools.partial(_flash_t_body, cfg),
        grid=(batch, n_steps, nq),
        in_specs=in_specs,
        out_specs=pl.BlockSpec((t, qw), lambda b, j, i: (b * nq + i, j)),
        out_shape=jax.ShapeDtypeStruct((batch * seq, n_steps * qw), BF16),
        scratch_shapes=[pltpu.VMEM((n_groups, rows, LANES), BF16),
                        pltpu.VMEM((n_groups, LANES, seq), BF16),
                        pltpu.VMEM((n_groups, 1, rows), F32),
                        pltpu.VMEM((n_groups, 1, rows), F32),
                        pltpu.VMEM((n_groups, LANES, rows), F32)],
        compiler_params=_params(3),
        name="flash_attention",
    )(*args)


def _select_t_body(cfg, qi_ref, wi_ref, ki_ref, o_ref, key_ref, qs_ref, cut_ref):
    n_keys, kpos0, qpos0, topk = cfg
    r = qi_ref.shape[0]
    t = FLASH_TILE
    n_tiles = n_keys // t
    qrow0 = qpos0 + pl.program_id(1) * r
    last_kpos = (lax.shift_right_arithmetic(qrow0 + r - 1, 6) + 1) * CHUNK - 1
    n_adm = jnp.minimum(n_tiles,
                        lax.shift_right_arithmetic(last_kpos - kpos0, t.bit_length() - 1) + 1)
    lane = lax.broadcasted_iota(jnp.int32, (r, LANES), 1)
    low = lane < C_IDX_DIM
    wit = wi_ref[...].T
    qchunk = lax.shift_right_arithmetic(qrow0 + lax.broadcasted_iota(jnp.int32, (t, r), 1), 6)
    krow = lax.broadcasted_iota(jnp.int32, (t, r), 0)

    def admissible(kb):
        kpos = kpos0 + kb * t + krow
        return jnp.logical_and(kpos >= 0, lax.shift_right_arithmetic(kpos, 6) <= qchunk)

    qs = []
    for h in range(C_IDX_HEADS):
        qp = qi_ref[:, (h // 2) * LANES:(h // 2 + 1) * LANES]
        zero = jnp.zeros_like(qp)
        qs.append(jnp.where(low, qp, zero) if h % 2 == 0 else jnp.where(low, zero, qp))
    qs_ref[...] = jnp.concatenate(qs, axis=0)

    def score_tile(kb, carry):
        r0 = pl.multiple_of(kb * t, t)
        dots = jnp.maximum(_dot_nt(ki_ref[pl.ds(r0, t), :], qs_ref[...]), 0.0)
        score = jnp.zeros((t, r), F32)
        for h in range(C_IDX_HEADS):
            score = score + wit[h:h + 1, :] * dots[:, h * r:(h + 1) * r]
        score = jnp.where(admissible(kb), score, NEG_INF)
        bits = pltpu.bitcast(score, jnp.int32)
        key_ref[kb] = jnp.where(bits >= 0, bits, bits ^ jnp.int32(0x7FFFFFFF))
        return carry

    lax.fori_loop(0, n_adm, score_tile, 0)

    def count(pred):
        def tile(kb, acc):
            hit = jnp.where(pred(key_ref[kb], kb), 1.0, 0.0)
            return acc + jnp.sum(hit.reshape(t // SUBLANES, SUBLANES, r), axis=0)
        acc = lax.fori_loop(0, n_adm, tile, jnp.zeros((SUBLANES, r), F32))
        return jnp.sum(acc, axis=0, keepdims=True)

    thr = jnp.full((1, r), jnp.int32(-2 ** 31), jnp.int32)
    cand0 = jnp.zeros((1, r), jnp.int32)
    thr = jnp.where(count(lambda k, kb: k >= cand0) >= topk, cand0, thr)

    def value_bit(it, thr):
        cand = thr | lax.shift_left(jnp.int32(1), 30 - it)
        return jnp.where(count(lambda k, kb: k >= cand) >= topk, cand, thr)

    thr = lax.fori_loop(0, 31, value_bit, thr)

    n_ge = count(lambda k, kb: k >= thr)
    cut_ref[...] = jnp.full((1, r), n_keys, jnp.int32)
    n_bits = max(1, (n_keys - 1).bit_length())

    @pl.when(jnp.max(n_ge) > topk)
    def _():
        ties_wanted = topk - count(lambda k, kb: k > thr)

        def index_bit(it, cut):
            cand = cut | lax.shift_left(jnp.int32(1), n_bits - 1 - it)
            before = count(lambda k, kb: jnp.logical_and(k == thr, kb * t + krow < cand))
            return jnp.where(before <= ties_wanted - 1.0, cand, cut)

        cut_ref[...] = lax.fori_loop(0, n_bits, index_bit, jnp.zeros((1, r), jnp.int32))

    cut = cut_ref[...]
    for kb in range(n_tiles):
        @pl.when(kb < n_adm)
        def _():
            kk = key_ref[kb]
            chosen = jnp.logical_or(kk > thr, jnp.logical_and(kk == thr, kb * t + krow <= cut))
            valid = jnp.logical_and(chosen, admissible(kb))
            o_ref[kb] = jnp.where(valid, 0.0, NEG_INF).astype(BF16)

        @pl.when(kb >= n_adm)
        def _():
            o_ref[kb] = jnp.full((t, r), NEG_INF, BF16)


def _select_mask(qi_arr, qi_blk, wi_arr, ki_arr, *, batch, q_rows, n_keys, kpos0, qpos0, topk):
    r = min(SELECT_ROWS, q_rows)
    t = FLASH_TILE
    nq = q_rows // r
    cfg = (n_keys, kpos0, qpos0, topk)
    return pl.pallas_call(
        functools.partial(_select_t_body, cfg),
        grid=(batch, nq),
        in_specs=[
            pl.BlockSpec((r, C_IDX_HEADS * C_IDX_DIM), lambda b, i: (b * nq + i, qi_blk)),
            pl.BlockSpec((r, LANES), lambda b, i: (b * nq + i, 0)),
            pl.BlockSpec((n_keys, LANES), lambda b, i: (b, 0)),
        ],
        out_specs=pl.BlockSpec((None, n_keys // t, t, r), lambda b, i: (b * nq + i, 0, 0, 0)),
        out_shape=jax.ShapeDtypeStruct((batch * nq, n_keys // t, t, r), BF16),
        scratch_shapes=[pltpu.VMEM((n_keys // t, t, r), jnp.int32),
                        pltpu.VMEM((C_IDX_HEADS * r, LANES), BF16),
                        pltpu.VMEM((1, r), jnp.int32)],
        compiler_params=_params(2),
        name="select_mask",
    )(qi_arr, wi_arr, ki_arr)


def _flash_body(cfg, *refs):
    n_groups, n_pairs, mode, lambda_init, has_sel = cfg
    it = iter(refs)
    q_ref, k_ref, v_ref, bias_ref = next(it), next(it), next(it), next(it)
    sel_ref = next(it) if has_sel else None
    if mode == "diff":
        lam_ref, subln_ref = next(it), next(it)
    o_ref = next(it)
    qs_ref, m_ref, l_ref, acc_ref = next(it), next(it), next(it), next(it)
    t = FLASH_TILE
    n_heads = 2 * n_pairs
    rows = n_heads * t
    i = pl.program_id(2)
    lane = lax.broadcasted_iota(jnp.int32, (t, LANES), 1)
    low = lane < HEAD_DIM
    row = lax.broadcasted_iota(jnp.int32, (rows, t), 0) & (t - 1)
    col = lax.broadcasted_iota(jnp.int32, (rows, t), 1)
    diag_ok = lax.shift_right_arithmetic(col, 6) <= lax.shift_right_arithmetic(row, 6)

    for g in range(n_groups):
        qs = []
        for p in range(n_pairs):
            qp = q_ref[:, (g * n_pairs + p) * LANES:(g * n_pairs + p + 1) * LANES]
            qs += [jnp.where(low, qp, jnp.zeros_like(qp)), jnp.where(low, jnp.zeros_like(qp), qp)]
        qs_ref[g] = jnp.concatenate(qs, axis=0)
    m_ref[...] = jnp.full(m_ref.shape, NEG_INF, F32)
    l_ref[...] = jnp.zeros(l_ref.shape, F32)
    acc_ref[...] = jnp.zeros(acc_ref.shape, F32)

    def kstep(j, carry):
        r0 = pl.multiple_of(j * t, t)
        tt = jnp.minimum(i - j, 2)
        on_diag = j >= i
        ss = []
        for g in range(n_groups):
            kb = k_ref[pl.ds(r0, t), g * LANES:(g + 1) * LANES]
            s = _dot_nt(qs_ref[g], kb) + bias_ref[tt, g * n_heads:(g + 1) * n_heads].reshape(rows, t)
            s = jnp.where(jnp.logical_or(jnp.logical_not(on_diag), diag_ok), s, NEG_INF)
            if has_sel:
                s = (s.reshape(n_heads, t, t) + sel_ref[j].astype(F32)[None]).reshape(rows, t)
            ss.append(s)
        prs = []
        for g in range(n_groups):
            m_old = m_ref[g]
            m_new = jnp.maximum(m_old, jnp.max(ss[g], axis=-1, keepdims=True))
            alpha = jnp.exp(m_old - m_new)
            pr = jnp.exp(ss[g] - m_new)
            l_ref[g] = alpha * l_ref[g] + jnp.sum(pr, axis=-1, keepdims=True)
            m_ref[g] = m_new
            prs.append((pr.astype(BF16), alpha))
        for g in range(n_groups):
            pr, alpha = prs[g]
            vb = v_ref[pl.ds(r0, t), g * LANES:(g + 1) * LANES]
            acc_ref[g] = alpha * acc_ref[g] + jnp.dot(pr, vb, preferred_element_type=F32)
        return carry

    lax.fori_loop(0, i + 1, kstep, 0)

    if mode == "diff":
        lv = lam_ref[...]
        lam = (jnp.exp(jnp.sum(lv[0:1] * lv[1:2], axis=-1, keepdims=True))
               - jnp.exp(jnp.sum(lv[2:3] * lv[3:4], axis=-1, keepdims=True)) + lambda_init)
    for g in range(n_groups):
        res = acc_ref[g] / l_ref[g]
        for p in range(n_pairs):
            lo = res[2 * p * t:(2 * p + 1) * t]
            hi = res[(2 * p + 1) * t:(2 * p + 2) * t]
            if mode == "diff":
                o = _rms(lo - lam * hi, subln_ref[...]) * (1.0 - lambda_init)
            else:
                o = jnp.where(low, lo, hi)
            o_ref[:, (g * n_pairs + p) * LANES:(g * n_pairs + p + 1) * LANES] = o.astype(BF16)


def _flash_attention_rowmajor(q_arr, q_blk0, k_arr, k_blk0, v_arr, v_blk0, bias, *, batch, seq,
                              n_kv_pairs, n_groups, n_pairs, mode="plain", sel=None,
                              lam_vecs=None, subln=None, lambda_init=0.0):
    t = FLASH_TILE
    nq = seq // t
    qw = n_groups * n_pairs * LANES
    kw = n_groups * LANES
    rows = 2 * n_pairs * t
    n_steps = n_kv_pairs // n_groups
    cfg = (n_groups, n_pairs, mode, lambda_init, sel is not None)
    in_specs = [
        pl.BlockSpec((t, qw), lambda b, j, i: (b * nq + i, q_blk0 + j)),
        pl.BlockSpec((seq, kw), lambda b, j, i: (b, k_blk0 + j)),
        pl.BlockSpec((seq, kw), lambda b, j, i: (b, v_blk0 + j)),
        pl.BlockSpec((3, 2 * n_pairs * n_groups, t, t), lambda b, j, i: (0, j, 0, 0)),
    ]
    args = [q_arr, k_arr, v_arr, bias]
    if sel is not None:
        in_specs.append(pl.BlockSpec((None, nq, t, t), lambda b, j, i: (b * nq + i, 0, 0, 0)))
        args.append(sel)
    if mode == "diff":
        in_specs.append(pl.BlockSpec((4, HEAD_DIM), lambda b, j, i: (0, 0)))
        in_specs.append(pl.BlockSpec((1, LANES), lambda b, j, i: (0, 0)))
        args += [lam_vecs, subln.reshape(1, LANES)]
    return pl.pallas_call(
        functools.partial(_flash_body, cfg),
        grid=(batch, n_steps, nq),
        in_specs=in_specs,
        out_specs=pl.BlockSpec((t, qw), lambda b, j, i: (b * nq + i, j)),
        out_shape=jax.ShapeDtypeStruct((batch * seq, n_steps * qw), BF16),
        scratch_shapes=[pltpu.VMEM((n_groups, rows, LANES), BF16),
                        pltpu.VMEM((n_groups, rows, 1), F32),
                        pltpu.VMEM((n_groups, rows, 1), F32),
                        pltpu.VMEM((n_groups, rows, LANES), F32)],
        compiler_params=_params(3),
        name="flash_attention",
    )(*args)


def _select_body(cfg, qi_ref, wi_ref, ki_ref, o_ref, key_ref, qs_ref, cut_ref):
    n_keys, kpos0, qpos0, topk = cfg
    r = qi_ref.shape[0]
    t = FLASH_TILE
    n_tiles = n_keys // t
    qrow0 = qpos0 + pl.program_id(1) * r
    last_kpos = (lax.shift_right_arithmetic(qrow0 + r - 1, 6) + 1) * CHUNK - 1
    n_adm = jnp.minimum(n_tiles,
                        lax.shift_right_arithmetic(last_kpos - kpos0, t.bit_length() - 1) + 1)
    lane = lax.broadcasted_iota(jnp.int32, (r, LANES), 1)
    low = lane < C_IDX_DIM
    wi = wi_ref[...]
    qchunk = lax.shift_right_arithmetic(qrow0 + lax.broadcasted_iota(jnp.int32, (r, t), 0), 6)
    col_t = lax.broadcasted_iota(jnp.int32, (r, t), 1)

    def admissible(kb):
        kpos = kpos0 + kb * t + col_t
        return jnp.logical_and(kpos >= 0, lax.shift_right_arithmetic(kpos, 6) <= qchunk)

    qs = []
    for h in range(C_IDX_HEADS):
        qp = qi_ref[:, (h // 2) * LANES:(h // 2 + 1) * LANES]
        zero = jnp.zeros_like(qp)
        qs.append(jnp.where(low, qp, zero) if h % 2 == 0 else jnp.where(low, zero, qp))
    qs_ref[...] = jnp.concatenate(qs, axis=0)

    def score_tile(kb, carry):
        r0 = pl.multiple_of(kb * t, t)
        dots = jnp.maximum(_dot_nt(qs_ref[...], ki_ref[pl.ds(r0, t), :]), 0.0)
        score = jnp.zeros((r, t), F32)
        for h in range(C_IDX_HEADS):
            score = score + wi[:, h:h + 1] * dots[h * r:(h + 1) * r]
        score = jnp.where(admissible(kb), score, NEG_INF)
        bits = pltpu.bitcast(score, jnp.int32)
        key_ref[kb] = jnp.where(bits >= 0, bits, bits ^ jnp.int32(0x7FFFFFFF))
        return carry

    lax.fori_loop(0, n_adm, score_tile, 0)

    def count(pred):
        def tile(kb, acc):
            hit = jnp.where(pred(key_ref[kb], kb), 1.0, 0.0)
            for c in range(0, t, LANES):
                acc = acc + hit[:, c:c + LANES]
            return acc
        acc = lax.fori_loop(0, n_adm, tile, jnp.zeros((r, LANES), F32))
        return jnp.sum(acc, axis=-1, keepdims=True)

    thr = jnp.full((r, 1), jnp.int32(-2 ** 31), jnp.int32)
    cand0 = jnp.zeros((r, 1), jnp.int32)
    thr = jnp.where(count(lambda k, kb: k >= cand0) >= topk, cand0, thr)

    def value_bit(it, thr):
        cand = thr | lax.shift_left(jnp.int32(1), 30 - it)
        return jnp.where(count(lambda k, kb: k >= cand) >= topk, cand, thr)

    thr = lax.fori_loop(0, 31, value_bit, thr)

    n_ge = count(lambda k, kb: k >= thr)
    cut_ref[...] = jnp.full((r, 1), n_keys, jnp.int32)
    n_bits = max(1, (n_keys - 1).bit_length())

    @pl.when(jnp.max(n_ge) > topk)
    def _():
        ties_wanted = topk - count(lambda k, kb: k > thr)

        def index_bit(it, cut):
            cand = cut | lax.shift_left(jnp.int32(1), n_bits - 1 - it)
            before = count(lambda k, kb: jnp.logical_and(k == thr, kb * t + col_t < cand))
            return jnp.where(before <= ties_wanted - 1.0, cand, cut)

        cut_ref[...] = lax.fori_loop(0, n_bits, index_bit, jnp.zeros((r, 1), jnp.int32))

    cut = cut_ref[...]
    for kb in range(n_tiles):
        @pl.when(kb < n_adm)
        def _():
            kk = key_ref[kb]
            chosen = jnp.logical_or(kk > thr, jnp.logical_and(kk == thr, kb * t + col_t <= cut))
            valid = jnp.logical_and(chosen, admissible(kb))
            o_ref[kb] = jnp.where(valid, 0.0, NEG_INF).astype(BF16)

        @pl.when(kb >= n_adm)
        def _():
            o_ref[kb] = jnp.full((r, t), NEG_INF, BF16)


def _select_mask_rowmajor(qi_arr, qi_blk, wi_arr, ki_arr, *, batch, q_rows, n_keys, kpos0, qpos0,
                          topk):
    r = min(SELECT_ROWS, q_rows)
    t = FLASH_TILE
    nq = q_rows // r
    cfg = (n_keys, kpos0, qpos0, topk)
    return pl.pallas_call(
        functools.partial(_select_body, cfg),
        grid=(batch, nq),
        in_specs=[
            pl.BlockSpec((r, C_IDX_HEADS * C_IDX_DIM), lambda b, i: (b * nq + i, qi_blk)),
            pl.BlockSpec((r, LANES), lambda b, i: (b * nq + i, 0)),
            pl.BlockSpec((n_keys, LANES), lambda b, i: (b, 0)),
        ],
        out_specs=pl.BlockSpec((None, n_keys // t, r, t), lambda b, i: (b * nq + i, 0, 0, 0)),
        out_shape=jax.ShapeDtypeStruct((batch * nq, n_keys // t, r, t), BF16),
        scratch_shapes=[pltpu.VMEM((n_keys // t, r, t), jnp.int32),
                        pltpu.VMEM((C_IDX_HEADS * r, LANES), BF16),
                        pltpu.VMEM((r, 1), jnp.int32)],
        compiler_params=_params(2),
        name="select_mask",
    )(qi_arr, wi_arr, ki_arr)


def _oproj_body(x_ref, a_ref, w_ref, o_ref):
    o_ref[...] = x_ref[...] + jnp.dot(a_ref[...], w_ref[...], preferred_element_type=F32)


def _oproj_residual(x, attn, w_o, tm):
    n = x.shape[0]
    return pl.pallas_call(
        _oproj_body,
        grid=(n // tm,),
        in_specs=[
            pl.BlockSpec((tm, D_MODEL), lambda i: (i, 0)),
            pl.BlockSpec((tm, ATTN_WIDTH), lambda i: (i, 0)),
            _resident((ATTN_WIDTH, D_MODEL), lambda i: (0, 0)),
        ],
        out_specs=pl.BlockSpec((tm, D_MODEL), lambda i: (i, 0)),
        out_shape=jax.ShapeDtypeStruct((n, D_MODEL), F32),
        compiler_params=_params(1),
        name="oproj_residual",
    )(x, attn, w_o)


def _ffn_body(cfg, *refs):
    seq_tiles, seq_rows, stream, final_norm = cfg
    it = iter(refs)
    x_ref = next(it)
    if stream:
        xprev_ref = next(it)
    else:
        fix1_ref, fix2_ref = next(it), next(it)
    g_ref, win_ref, cw_ref, cb_ref, wout_ref = next(it), next(it), next(it), next(it), next(it)
    gfin_ref = next(it) if final_norm else None
    o_ref, st_ref = next(it), next(it)
    hext_ref, act_ref = next(it), next(it)
    tm = x_ref.shape[0]
    pad = SUBLANES
    x = x_ref[...]
    gain = g_ref[...]
    hext_ref[pad:, :] = _rms(x, gain).astype(BF16)
    if stream:
        not_start = ((pl.program_id(0) % seq_tiles) != 0).astype(F32)
        hext_ref[0:pad, :] = (_rms(xprev_ref[...], gain) * not_start).astype(BF16)
    else:
        hext_ref[0:pad, :] = jnp.zeros((pad, D_MODEL), BF16)
    hext = hext_ref[...]
    h = hext[pad:, :]
    is_seq_end = (pl.program_id(0) % seq_tiles) == seq_tiles - 1
    if not stream:
        assert seq_rows & (seq_rows - 1) == 0
        rmod = lax.broadcasted_iota(jnp.int32, (tm, FF_CHUNK), 0) & (seq_rows - 1)
    for c0 in range(0, D_FF, FF_CHUNK):
        a_ext = jnp.dot(hext, win_ref[:, c0:c0 + FF_CHUNK], preferred_element_type=F32)
        gate = jnp.dot(h, win_ref[:, D_FF + c0:D_FF + c0 + FF_CHUNK], preferred_element_type=F32)
        a = a_ext[pad:, :]
        a1 = pltpu.roll(a_ext, 1, 0)[pad:, :]
        a2 = pltpu.roll(a_ext, 2, 0)[pad:, :]
        if not stream:
            a1 = jnp.where(rmod == 0, fix1_ref[:, c0:c0 + FF_CHUNK], a1)
            a2 = jnp.where(rmod <= 1, fix2_ref[:, c0:c0 + FF_CHUNK], a2)
        cw = cw_ref[:, c0:c0 + FF_CHUNK]
        u = cb_ref[:, c0:c0 + FF_CHUNK] + (cw[0:1] * a2 + cw[1:2] * a1 + cw[2:3] * a)
        act_ref[:, c0:c0 + FF_CHUNK] = (jax.nn.silu(u) * gate).astype(BF16)
        if stream:
            @pl.when(is_seq_end)
            def _():
                st_ref[:, c0:c0 + FF_CHUNK] = a[tm - pad:, :]
        else:
            st_ref[:, c0:c0 + FF_CHUNK] = a
    y = x + jnp.dot(act_ref[...], wout_ref[...], preferred_element_type=F32)
    if final_norm:
        y = _rms(y, gfin_ref[...])
    o_ref[...] = y


def _conv_ffn(x, gain, w_in, conv_w, conv_b, w_out, *, tm, seq_tiles, seq_rows, fixes=None,
              final_gain=None):
    n = x.shape[0]
    stream = fixes is None
    final_norm = final_gain is not None
    cfg = (seq_tiles, seq_rows, stream, final_norm)
    in_specs = [pl.BlockSpec((tm, D_MODEL), lambda i: (i, 0))]
    args = [x]
    if stream:
        per = tm // SUBLANES
        in_specs.append(pl.BlockSpec((SUBLANES, D_MODEL), lambda i: (jnp.maximum(i * per - 1, 0), 0)))
        args.append(x)
    else:
        in_specs += [pl.BlockSpec((tm, D_FF), lambda i: (i, 0))] * 2
        args += list(fixes)
    in_specs += [
        _resident((1, D_MODEL), lambda i: (0, 0)),
        _resident((D_MODEL, 2 * D_FF), lambda i: (0, 0)),
        _resident((CONV_W, D_FF), lambda i: (0, 0)),
        _resident((1, D_FF), lambda i: (0, 0)),
        _resident((D_FF, D_MODEL), lambda i: (0, 0)),
    ]
    args += [gain.reshape(1, D_MODEL), w_in, conv_w, conv_b.reshape(1, D_FF), w_out]
    if final_norm:
        in_specs.append(_resident((1, D_MODEL), lambda i: (0, 0)))
        args.append(final_gain.reshape(1, D_MODEL))
    if stream:
        st_shape = jax.ShapeDtypeStruct((n // (tm * seq_tiles) * SUBLANES, D_FF), F32)
        st_spec = pl.BlockSpec((SUBLANES, D_FF), lambda i: (i // seq_tiles, 0))
    else:
        st_shape = jax.ShapeDtypeStruct((n, D_FF), F32)
        st_spec = pl.BlockSpec((tm, D_FF), lambda i: (i, 0))
    return pl.pallas_call(
        functools.partial(_ffn_body, cfg),
        grid=(n // tm,),
        in_specs=in_specs,
        out_specs=[pl.BlockSpec((tm, D_MODEL), lambda i: (i, 0)), st_spec],
        out_shape=[jax.ShapeDtypeStruct((n, D_MODEL), F32), st_shape],
        scratch_shapes=[pltpu.VMEM((tm + SUBLANES, D_MODEL), BF16), pltpu.VMEM((tm, D_FF), BF16)],
        compiler_params=_params(1),
        name="conv_ffn",
    )(*args)


def _t5_bucket(rel):
    half = NUM_BUCKETS // 2
    max_exact = half // 2
    base = jnp.where(rel > 0, half, 0)
    n = jnp.abs(rel)
    nf = jnp.maximum(n, 1).astype(F32)
    large = max_exact + (jnp.log(nf / max_exact) / math.log(T5_MAX_DISTANCE / max_exact)
                         * (half - max_exact)).astype(jnp.int32)
    large = jnp.minimum(large, half - 1)
    return base + jnp.where(n < max_exact, n, large)


def _t5_bias(table, rel):
    return jnp.moveaxis(table.astype(F32)[_t5_bucket(rel)], -1, 0)


def _band_rel(win, n_past_eff):
    i = jnp.arange(CHUNK)[:, None]
    j = jnp.arange(win)[None, :]
    return (j - n_past_eff) - i


def _flash_rel():
    t = FLASH_TILE
    r = jnp.arange(t)[:, None]
    c = jnp.arange(t)[None, :]
    return jnp.stack([c - r - d * t for d in range(3)])


def _flash_bias(bias):
    h, kinds, tq, tk = bias.shape
    return bias.transpose(1, 3, 0, 2).reshape(kinds, tk, h * tq)


def _perm_cols(w, perm):
    return w.reshape(w.shape[0], len(perm), HEAD_DIM)[:, perm, :].reshape(w.shape[0], -1)


def _perm_rows(w, perm):
    return w.reshape(len(perm), HEAD_DIM, w.shape[1])[perm, :, :].reshape(-1, w.shape[1])


def _pad_keys(cache, new, pad, width):
    b = cache.shape[0]
    allk = jnp.concatenate([cache.reshape(b, -1, width), new.reshape(b, -1, width)], axis=1)
    padded = jnp.pad(allk, ((0, 0), (pad, 0), (0, 0))).astype(BF16)
    return allk, padded.reshape(-1, width)


def kernel(x_prompt, x_sample, cache_a_k, cache_a_v, cache_b_k, cache_b_v, cache_c_k, cache_c_v,
           cache_c_kidx, cache_d_k, cache_d_v, state_ffn_conv, t5_table, norm_mix, norm_ffn,
           norm_final, a_w_qkv, a_w_o, a_rel_bias, b_w_qkv, b_w_o, b_sinks, c_w_qkv, c_w_o,
           c_w_idx_q, c_w_idx_k, c_idx_k_norm, c_w_idx_w, d_w_qkv, d_w_o, d_lambda_q1,
           d_lambda_k1, d_lambda_q2, d_lambda_k2, d_subln, ffn_w_in, ffn_conv_w, ffn_conv_b,
           ffn_w_out):
    bp, seq, d = x_prompt.shape
    bs, ts, _ = x_sample.shape
    past = cache_c_k.shape[1]
    assert d == D_MODEL and ts == CHUNK and seq % ROW_TILE == 0 and past % CHUNK == 0
    n_p, n_s = bp * seq, bs * ts
    seq_tiles = seq // ROW_TILE
    depth = norm_mix.shape[0]
    scale = HEAD_DIM ** -0.5
    perm = jnp.array(GQA_PERM)
    none_aux = jnp.zeros((1, LANES), F32)
    kvw = KV_HEADS * HEAD_DIM

    xp = x_prompt.reshape(n_p, d)
    xs = x_sample.reshape(n_s, d)

    def cast_dests(col_dests):
        secs = []
        for c0, c1, fn in col_dests:
            for c in range(c0, c1, 2 * LANES):
                secs.append((c, min(2 * LANES, c1 - c), fn(c)))
        return secs

    layer = 0
    w = a_w_qkv.astype(BF16)
    aw = ATTN_WIDTH
    a_keep = min(A_PAST, seq)
    secs_p = cast_dests([
        (0, aw, lambda c: [("cast", 0, c, scale)]),
        (aw, 2 * aw, lambda c: [("cast", 0, c, None), ("tail", 1, c - aw, a_keep)]),
        (2 * aw, 3 * aw, lambda c: [("cast", 0, c, None), ("tail", 2, c - 2 * aw, a_keep)]),
    ])
    qkv, a_k_p, a_v_p = _norm_proj(
        xp, norm_mix[layer], w, none_aux, secs_p,
        [(n_p, 3 * aw, BF16, ROW_TILE, "all"), (bp * a_keep, aw, F32, a_keep, "tail"),
         (bp * a_keep, aw, F32, a_keep, "tail")], seq_tiles, ROW_TILE)
    secs_s = cast_dests([
        (0, aw, lambda c: [("cast", 0, c, scale)]),
        (aw, 3 * aw, lambda c: [("cast", 1, c - aw, None)]),
    ])
    q_s, kv_s = _norm_proj(xs, norm_mix[layer], w, none_aux, secs_s,
                           [(n_s, aw, BF16, n_s, "all"), (n_s, 2 * aw, F32, n_s, "all")], 1, n_s)
    a_pad = CHUNK
    a_win = A_PAST + CHUNK + a_pad
    rel = _band_rel(a_win, A_PAST + a_pad)
    bias_a = jnp.moveaxis(a_rel_bias.astype(F32)[jnp.clip(rel, -A_CLIP, A_CLIP) + A_CLIP], -1, 0)
    bias_a = jnp.where((jnp.arange(a_win) >= a_pad)[None, None, :], bias_a, NEG_INF)
    att_p = _band_attention(qkv, 0, qkv, aw // LANES, qkv, 2 * aw // LANES, bias_a, batch=bp,
                            q_rows=seq, k_rows=seq, n_kv_pairs=N_HEADS // 2, n_pairs=1,
                            win=a_win, n_past=A_PAST + a_pad)
    ks = kv_s[:, :aw].reshape(bs, ts, aw)
    vs = kv_s[:, aw:].reshape(bs, ts, aw)
    k_all, k_in = _pad_keys(cache_a_k, ks, a_pad, aw)
    v_all, v_in = _pad_keys(cache_a_v, vs, a_pad, aw)
    assert k_all.shape[1] + a_pad == a_win
    att_s = _band_attention(q_s, 0, k_in, 0, v_in, 0, bias_a, batch=bs, q_rows=ts, k_rows=a_win,
                            n_kv_pairs=N_HEADS // 2, n_pairs=1, win=a_win, n_past=0)
    a_keep_s = min(A_PAST, k_all.shape[1])
    a_k_prompt = a_k_p.reshape(bp, a_keep, N_HEADS, HEAD_DIM)
    a_v_prompt = a_v_p.reshape(bp, a_keep, N_HEADS, HEAD_DIM)
    a_k_sample = k_all[:, -a_keep_s:].reshape(bs, a_keep_s, N_HEADS, HEAD_DIM)
    a_v_sample = v_all[:, -a_keep_s:].reshape(bs, a_keep_s, N_HEADS, HEAD_DIM)
    w_o = a_w_o.astype(BF16)
    xp = _oproj_residual(xp, att_p, w_o, ROW_TILE)
    xs = _oproj_residual(xs, att_s, w_o, n_s)
    xp, xs, conv_p0, conv_s0 = _ffn_layer(xp, xs, layer, bp, bs, seq_tiles, ts, norm_ffn,
                                          ffn_w_in, ffn_conv_w, ffn_conv_b, ffn_w_out,
                                          state_ffn_conv, None)

    layer = 1
    w = jnp.concatenate([_perm_cols(b_w_qkv[:, :aw], perm), b_w_qkv[:, aw:]], axis=1).astype(BF16)
    b_keep = min(B_WINDOW, seq)
    secs_p = cast_dests([
        (0, aw, lambda c: [("cast", 0, c, scale)]),
        (aw, aw + kvw, lambda c: [("cast", 0, c, None), ("tail", 1, c - aw, b_keep)]),
        (aw + kvw, aw + 2 * kvw, lambda c: [("cast", 0, c, None), ("tail", 2, c - aw - kvw, b_keep)]),
    ])
    qkv, b_k_p, b_v_p = _norm_proj(
        xp, norm_mix[layer], w, none_aux, secs_p,
        [(n_p, aw + 2 * kvw, BF16, ROW_TILE, "all"), (bp * b_keep, kvw, F32, b_keep, "tail"),
         (bp * b_keep, kvw, F32, b_keep, "tail")], seq_tiles, ROW_TILE)
    secs_s = cast_dests([
        (0, aw, lambda c: [("cast", 0, c, scale)]),
        (aw, aw + 2 * kvw, lambda c: [("cast", 1, c - aw, None)]),
    ])
    q_s, kv_s = _norm_proj(xs, norm_mix[layer], w, none_aux, secs_s,
                           [(n_s, aw, BF16, n_s, "all"), (n_s, 2 * kvw, F32, n_s, "all")], 1, n_s)
    b_pad = CHUNK
    b_win = B_WINDOW + CHUNK + b_pad
    bias_b = _t5_bias(t5_table, _band_rel(b_win, B_WINDOW + b_pad))[perm]
    bias_b = jnp.where((jnp.arange(b_win) >= b_pad)[None, None, :], bias_b, NEG_INF)
    sinks = b_sinks.astype(F32)[perm]
    n_kvp = KV_HEADS // 2
    gq = N_HEADS // KV_HEADS
    att_p = _band_attention(qkv, 0, qkv, aw // LANES, qkv, (aw + kvw) // LANES, bias_b, batch=bp,
                            q_rows=seq, k_rows=seq, n_kv_pairs=n_kvp, n_pairs=gq, win=b_win,
                            n_past=B_WINDOW + b_pad, mode="sink", sinks=sinks)
    ks = kv_s[:, :kvw].reshape(bs, ts, kvw)
    vs = kv_s[:, kvw:].reshape(bs, ts, kvw)
    k_all, k_in = _pad_keys(cache_b_k, ks, b_pad, kvw)
    v_all, v_in = _pad_keys(cache_b_v, vs, b_pad, kvw)
    assert k_all.shape[1] + b_pad == b_win
    att_s = _band_attention(q_s, 0, k_in, 0, v_in, 0, bias_b, batch=bs, q_rows=ts, k_rows=b_win,
                            n_kv_pairs=n_kvp, n_pairs=gq, win=b_win, n_past=0, mode="sink",
                            sinks=sinks)
    b_keep_s = min(B_WINDOW, k_all.shape[1])
    b_k_prompt = b_k_p.reshape(bp, b_keep, KV_HEADS, HEAD_DIM)
    b_v_prompt = b_v_p.reshape(bp, b_keep, KV_HEADS, HEAD_DIM)
    b_k_sample = k_all[:, -b_keep_s:].reshape(bs, b_keep_s, KV_HEADS, HEAD_DIM)
    b_v_sample = v_all[:, -b_keep_s:].reshape(bs, b_keep_s, KV_HEADS, HEAD_DIM)
    w_o = _perm_rows(b_w_o, perm).astype(BF16)
    xp = _oproj_residual(xp, att_p, w_o, ROW_TILE)
    xs = _oproj_residual(xs, att_s, w_o, n_s)
    xp, xs, conv_p1, conv_s1 = _ffn_layer(xp, xs, layer, bp, bs, seq_tiles, ts, norm_ffn,
                                          ffn_w_in, ffn_conv_w, ffn_conv_b, ffn_w_out,
                                          state_ffn_conv, None)

    layer = 2
    iw = C_IDX_HEADS * C_IDX_DIM
    w_idx_w = jnp.pad(c_w_idx_w, ((0, 0), (0, LANES - C_IDX_HEADS)))
    w = jnp.concatenate([_perm_cols(c_w_qkv[:, :aw], perm), c_w_qkv[:, aw:], c_w_idx_q,
                         c_w_idx_k, c_w_idx_k, w_idx_w], axis=1).astype(BF16)
    c_qkv_w = aw + 2 * kvw
    col_ki = c_qkv_w + iw
    col_wi = col_ki + LANES
    knorm = jnp.concatenate([c_idx_k_norm, c_idx_k_norm]).astype(F32).reshape(1, LANES)
    wi_scale = C_IDX_HEADS ** -0.5

    def c_sections(k_out, v_out, qkv_out):
        secs = cast_dests([
            (0, aw, lambda c: [("cast", qkv_out, c, scale)]),
            (aw, aw + kvw, lambda c: [("cast", qkv_out, c, None), ("cast", k_out, c - aw, None)]),
            (aw + kvw, c_qkv_w,
             lambda c: [("cast", qkv_out, c, None), ("cast", v_out, c - aw - kvw, None)]),
            (c_qkv_w, col_ki, lambda c: [("cast", qkv_out, c, C_IDX_DIM ** -0.5)]),
        ])
        secs.append((col_ki, LANES, [("kidx", 3, 4)]))
        secs.append((col_wi, LANES, [("cast", 5, 0, wi_scale)]))
        return secs

    def c_outs(n, tm):
        return [(n, c_qkv_w + iw, BF16, tm, "all"), (n, kvw, F32, tm, "all"),
                (n, kvw, F32, tm, "all"), (n, LANES, BF16, tm, "all"),
                (n, C_IDX_DIM, F32, tm, "all"), (n, LANES, F32, tm, "all")]

    qkv, c_k_p, c_v_p, ki_p, kidx_p, wi_p = _norm_proj(
        xp, norm_mix[layer], w, knorm, c_sections(1, 2, 0), c_outs(n_p, ROW_TILE), seq_tiles,
        ROW_TILE)
    qkv_s, c_k_s, c_v_s, ki_s, kidx_s, wi_s = _norm_proj(
        xs, norm_mix[layer], w, knorm, c_sections(1, 2, 0), c_outs(n_s, n_s), 1, n_s)
    qi_blk = c_qkv_w // iw
    assert qi_blk * iw == c_qkv_w
    t = FLASH_TILE
    assert SELECT_ROWS == t
    sel_p = _select_mask(qkv, qi_blk, wi_p, ki_p, batch=bp, q_rows=seq, n_keys=seq, kpos0=0,
                         qpos0=0, topk=min(C_TOPK, seq // 4))
    bias_c = _flash_bias(_t5_bias(t5_table, _flash_rel())[perm])
    att_p = _flash_attention(qkv, 0, qkv, aw // LANES, qkv, (aw + kvw) // LANES, bias_c, batch=bp,
                             seq=seq, n_kv_pairs=n_kvp, n_groups=1, n_pairs=gq, sel=sel_p)
    n_keys_s = past + ts
    c_pad = (-n_keys_s) % t
    c_win = n_keys_s + c_pad
    k_all, k_in = _pad_keys(cache_c_k, c_k_s.reshape(bs, ts, kvw), c_pad, kvw)
    v_all, v_in = _pad_keys(cache_c_v, c_v_s.reshape(bs, ts, kvw), c_pad, kvw)
    del ki_s
    _, ki_in = _pad_keys(cache_c_kidx, kidx_s.reshape(bs, ts, C_IDX_DIM), c_pad, C_IDX_DIM)
    ki_in = jnp.concatenate([ki_in, ki_in], axis=1)
    qi_s = jnp.pad(qkv_s[:, c_qkv_w:].reshape(bs, ts, iw), ((0, 0), (0, LANES - ts), (0, 0)))
    wi_s = jnp.pad(wi_s.reshape(bs, ts, LANES), ((0, 0), (0, LANES - ts), (0, 0)))
    sel_s = _select_mask(qi_s.reshape(bs * LANES, iw), 0, wi_s.reshape(bs * LANES, LANES), ki_in,
                         batch=bs, q_rows=LANES, n_keys=c_win, kpos0=-c_pad, qpos0=past,
                         topk=min(C_TOPK, n_keys_s // 4))
    sel_s = sel_s[..., :ts].transpose(0, 3, 1, 2).reshape(bs, ts, c_win)
    rel_s = (jnp.arange(c_win)[None, :] - c_pad) - (past + jnp.arange(ts)[:, None])
    pad_ok = (jnp.arange(c_win) >= c_pad)[None, None, :]
    bias_s_t5 = jnp.where(pad_ok, _t5_bias(t5_table, rel_s), NEG_INF)
    att_s = _band_attention(qkv_s, 0, k_in, 0, v_in, 0, bias_s_t5[perm], batch=bs, q_rows=ts,
                            k_rows=c_win, n_kv_pairs=n_kvp, n_pairs=gq, win=c_win, n_past=0,
                            sel=sel_s)
    c_k_prompt = c_k_p.reshape(bp, seq, KV_HEADS, HEAD_DIM)
    c_v_prompt = c_v_p.reshape(bp, seq, KV_HEADS, HEAD_DIM)
    c_kidx_prompt = kidx_p.reshape(bp, seq, C_IDX_DIM)
    c_k_sample = c_k_s.reshape(bs, ts, KV_HEADS, HEAD_DIM)
    c_v_sample = c_v_s.reshape(bs, ts, KV_HEADS, HEAD_DIM)
    c_kidx_sample = kidx_s.reshape(bs, ts, C_IDX_DIM)
    w_o = _perm_rows(c_w_o, perm).astype(BF16)
    xp = _oproj_residual(xp, att_p, w_o, ROW_TILE)
    xs = _oproj_residual(xs, att_s, w_o, n_s)
    xp, xs, conv_p2, conv_s2 = _ffn_layer(xp, xs, layer, bp, bs, seq_tiles, ts, norm_ffn,
                                          ffn_w_in, ffn_conv_w, ffn_conv_b, ffn_w_out,
                                          state_ffn_conv, None)

    layer = 3
    lambda_init = 0.8 - 0.6 * math.exp(-0.3 * layer)
    w = d_w_qkv.astype(BF16)
    lam_vecs = jnp.stack([d_lambda_q1, d_lambda_k1, d_lambda_q2, d_lambda_k2]).astype(F32)
    secs = cast_dests([
        (0, aw, lambda c: [("cast", 0, c, scale)]),
        (aw, 2 * aw, lambda c: [("cast", 0, c, None), ("cast", 1, c - aw, None)]),
        (2 * aw, 3 * aw, lambda c: [("cast", 0, c, None), ("cast", 2, c - 2 * aw, None)]),
    ])

    def d_outs(n, tm):
        return [(n, 3 * aw, BF16, tm, "all"), (n, aw, F32, tm, "all"), (n, aw, F32, tm, "all")]

    qkv, d_k_p, d_v_p = _norm_proj(xp, norm_mix[layer], w, none_aux, secs, d_outs(n_p, ROW_TILE),
                                   seq_tiles, ROW_TILE)
    qkv_s, d_k_s, d_v_s = _norm_proj(xs, norm_mix[layer], w, none_aux, secs, d_outs(n_s, n_s), 1,
                                     n_s)
    bias_d = _flash_bias(_t5_bias(t5_table, _flash_rel()))
    d_grp = 2
    att_p = _flash_attention(qkv, 0, qkv, aw // (d_grp * LANES), qkv, 2 * aw // (d_grp * LANES),
                             bias_d, batch=bp, seq=seq, n_kv_pairs=D_HEADS, n_groups=d_grp,
                             n_pairs=1, mode="diff",
                             lam_vecs=lam_vecs, subln=d_subln.astype(F32), lambda_init=lambda_init)
    k_all, k_in = _pad_keys(cache_d_k, d_k_s.reshape(bs, ts, aw), c_pad, aw)
    v_all, v_in = _pad_keys(cache_d_v, d_v_s.reshape(bs, ts, aw), c_pad, aw)
    att_s = _band_attention(qkv_s, 0, k_in, 0, v_in, 0, bias_s_t5, batch=bs, q_rows=ts,
                            k_rows=c_win, n_kv_pairs=D_HEADS, n_pairs=1, win=c_win, n_past=0,
                            mode="diff", lam_vecs=lam_vecs, subln=d_subln.astype(F32),
                            lambda_init=lambda_init)
    d_k_prompt = d_k_p.reshape(bp, seq, 2 * D_HEADS, HEAD_DIM)
    d_v_prompt = d_v_p.reshape(bp, seq, D_HEADS, 2 * HEAD_DIM)
    d_k_sample = d_k_s.reshape(bs, ts, 2 * D_HEADS, HEAD_DIM)
    d_v_sample = d_v_s.reshape(bs, ts, D_HEADS, 2 * HEAD_DIM)
    w_o = d_w_o.astype(BF16)
    xp = _oproj_residual(xp, att_p, w_o, ROW_TILE)
    xs = _oproj_residual(xs, att_s, w_o, n_s)
    xp, xs, conv_p3, conv_s3 = _ffn_layer(xp, xs, layer, bp, bs, seq_tiles, ts, norm_ffn,
                                          ffn_w_in, ffn_conv_w, ffn_conv_b, ffn_w_out,
                                          state_ffn_conv, norm_final)
    assert depth == 4

    y_prompt = xp.reshape(bp, seq, d)
    y_sample = xs.reshape(bs, ts, d)
    ffn_conv_prompt = jnp.stack([conv_p0, conv_p1, conv_p2, conv_p3])
    ffn_conv_sample = jnp.stack([conv_s0, conv_s1, conv_s2, conv_s3])
    return (y_prompt, y_sample,
            a_k_prompt, a_v_prompt, a_k_sample, a_v_sample,
            b_k_prompt, b_v_prompt, b_k_sample, b_v_sample,
            c_k_prompt, c_v_prompt, c_kidx_prompt, c_k_sample, c_v_sample, c_kidx_sample,
            d_k_prompt, d_v_prompt, d_k_sample, d_v_sample,
            ffn_conv_prompt, ffn_conv_sample)


def _ffn_layer(xp, xs, layer, bp, bs, seq_tiles, ts, norm_ffn, ffn_w_in, ffn_conv_w, ffn_conv_b,
               ffn_w_out, state, final_gain):
    w_in = ffn_w_in[layer].astype(BF16)
    w_out = ffn_w_out[layer].astype(BF16)
    cw, cb, gain = ffn_conv_w[layer], ffn_conv_b[layer], norm_ffn[layer]
    xp, tail = _conv_ffn(xp, gain, w_in, cw, cb, w_out, tm=ROW_TILE, seq_tiles=seq_tiles,
                         seq_rows=ROW_TILE * seq_tiles, final_gain=final_gain)
    conv_p = tail.reshape(bp, SUBLANES, D_FF)[:, SUBLANES - (CONV_W - 1):, :]
    n_s = bs * ts
    st = state[layer]
    zeros = jnp.zeros((bs, ts - 2, D_FF), F32)
    fix1 = jnp.concatenate([st[:, 1:2], jnp.zeros((bs, 1, D_FF), F32), zeros], axis=1)
    fix2 = jnp.concatenate([st, zeros], axis=1)
    xs, a_s = _conv_ffn(xs, gain, w_in, cw, cb, w_out, tm=n_s, seq_tiles=1, seq_rows=ts,
                        fixes=(fix1.reshape(n_s, D_FF), fix2.reshape(n_s, D_FF)),
                        final_gain=final_gain)
    conv_s = a_s.reshape(bs, ts, D_FF)[:, ts - (CONV_W - 1):, :]
    return xp, xs, conv_p, conv_s
```

```python
import functools
import math

import jax
import jax.numpy as jnp
from jax import lax
from jax.experimental import pallas as pl
from jax.experimental.pallas import tpu as pltpu

F32 = jnp.float32
BF16 = jnp.bfloat16

D_MODEL = 1024
CHUNK = 64
N_HEADS = 16
HEAD_DIM = 64
ATTN_WIDTH = N_HEADS * HEAD_DIM
NUM_BUCKETS = 32
T5_MAX_DISTANCE = 128
A_PAST = 512
A_CLIP = 64
B_WINDOW = 128
KV_HEADS = 4
C_TOPK = 256
C_IDX_HEADS = 8
C_IDX_DIM = 64
D_HEADS = 8
D_FF = 2816
CONV_W = 3
RMS_EPS = 1e-6
NEG_INF = -1e30

LANES = 128
SUBLANES = 8
ROW_TILE = 512
FLASH_TILE = 256
BAND_UNROLL = 4
SELECT_ROWS = FLASH_TILE
FF_CHUNK = 256
PROJ_CHUNK = 512
VMEM_LIMIT = 56 * 1024 * 1024

GQA_PERM = tuple(8 * j + 4 * half + t for j in range(2) for t in range(4) for half in range(2))


def _params(n_grid_dims):
    return pltpu.CompilerParams(
        dimension_semantics=("arbitrary",) * n_grid_dims, vmem_limit_bytes=VMEM_LIMIT)


def _resident(shape, index_map):
    return pl.BlockSpec(shape, index_map, pipeline_mode=pl.Buffered(1))


def _rms(x, gain):
    ms = jnp.mean(x * x, axis=-1, keepdims=True)
    return (x * lax.rsqrt(ms + RMS_EPS)) * gain


def _dot_nt(a, b):
    return lax.dot_general(a, b, (((1,), (1,)), ((), ())), preferred_element_type=F32)


def _norm_proj_body(sections, seq_tiles, n_out, x_ref, g_ref, w_ref, aux_ref, *out_refs):
    tm = x_ref.shape[0]
    h = _rms(x_ref[...], g_ref[...]).astype(BF16)
    for col0, width, dests in sections:
        y = jnp.dot(h, w_ref[:, col0:col0 + width], preferred_element_type=F32)
        for dest in dests:
            kind = dest[0]
            if kind == "cast":
                _, o, oc, scale = dest
                val = y if scale is None else y * scale
                out_refs[o][:, oc:oc + width] = val.astype(out_refs[o].dtype)
            elif kind == "tail":
                _, o, oc, nrows = dest
                out_refs[o][:, oc:oc + width] = y[tm - nrows:, :]
            elif kind == "kidx":
                _, o_dup, o_f32 = dest
                yn = _rms(y, aux_ref[...])
                out_refs[o_dup][...] = yn.astype(BF16)
                out_refs[o_f32][...] = yn[:, :C_IDX_DIM]
            else:
                raise ValueError(kind)


def _norm_proj(x, gain, w, aux, sections, outs, seq_tiles, tm):
    n = x.shape[0]
    grid = (n // tm,)
    out_shape, out_specs = [], []
    for rows, width, dtype, blk_rows, mode in outs:
        out_shape.append(jax.ShapeDtypeStruct((rows, width), dtype))
        if mode == "all":
            out_specs.append(pl.BlockSpec((blk_rows, width), lambda i: (i, 0)))
        else:
            out_specs.append(pl.BlockSpec((blk_rows, width), lambda i: (i // seq_tiles, 0)))
    body = functools.partial(_norm_proj_body, sections, seq_tiles, len(outs))
    return pl.pallas_call(
        body,
        grid=grid,
        in_specs=[
            pl.BlockSpec((tm, D_MODEL), lambda i: (i, 0)),
            _resident((1, D_MODEL), lambda i: (0, 0)),
            _resident(w.shape, lambda i: (0, 0)),
            _resident(aux.shape, lambda i: (0, 0)),
        ],
        out_specs=out_specs,
        out_shape=out_shape,
        compiler_params=_params(1),
        name="norm_proj",
    )(x, gain.reshape(1, D_MODEL), w, aux)


def _split_sections(col0, total, width, dests_fn):
    return [(col0 + c, width, dests_fn(c)) for c in range(0, total, width)]


def _softmax_rows(s, sink):
    m = jnp.max(s, axis=-1, keepdims=True)
    if sink is not None:
        m = jnp.maximum(m, sink)
    p = jnp.exp(s - m)
    l = jnp.sum(p, axis=-1, keepdims=True)
    if sink is not None:
        l = l + jnp.exp(sink - m)
    return p, l


def _band_body(cfg, *refs):
    n_pairs, win, n_past, n_chunks, unroll, mode, lambda_init, has_sel = cfg
    it = iter(refs)
    q_ref, k_ref, v_ref, bias_ref = next(it), next(it), next(it), next(it)
    sel_ref = next(it) if has_sel else None
    sink_ref = next(it) if mode == "sink" else None
    if mode == "diff":
        lam_ref, subln_ref = next(it), next(it)
    o_ref = next(it)
    if n_past:
        kp_ref, vp_ref = next(it), next(it)
        kp_ref[0:n_past, :] = jnp.zeros((n_past, LANES), BF16)
        vp_ref[0:n_past, :] = jnp.zeros((n_past, LANES), BF16)
        kp_ref[n_past:, :] = k_ref[...]
        vp_ref[n_past:, :] = v_ref[...]
    else:
        kp_ref, vp_ref = k_ref, v_ref

    n_heads = 2 * n_pairs
    rows = n_heads * CHUNK
    lane = lax.broadcasted_iota(jnp.int32, (CHUNK, LANES), 1)
    col = lax.broadcasted_iota(jnp.int32, (rows, win), 1)
    low = lane < HEAD_DIM
    if mode == "diff":
        lv = lam_ref[...]
        lam = (jnp.exp(jnp.sum(lv[0:1] * lv[1:2], axis=-1, keepdims=True))
               - jnp.exp(jnp.sum(lv[2:3] * lv[3:4], axis=-1, keepdims=True)) + lambda_init)

    def logits(c):
        r0 = pl.multiple_of(c * CHUNK, CHUNK)
        kwin = kp_ref[pl.ds(r0, win), :]
        qs = []
        for p in range(n_pairs):
            qp = q_ref[pl.ds(r0, CHUNK), p * LANES:(p + 1) * LANES]
            qs += [jnp.where(low, qp, jnp.zeros_like(qp)), jnp.where(low, jnp.zeros_like(qp), qp)]
        s = _dot_nt(jnp.concatenate(qs, axis=0), kwin) + bias_ref[...].reshape(rows, win)
        if n_past:
            s = jnp.where(col >= n_past - c * CHUNK, s, NEG_INF)
        if has_sel:
            s = (s.reshape(n_heads, CHUNK, win) + sel_ref[...].astype(F32)[None]).reshape(rows, win)
        return s

    def attend(c, pr, l):
        r0 = pl.multiple_of(c * CHUNK, CHUNK)
        vwin = vp_ref[pl.ds(r0, win), :]
        if mode == "diff":
            pn = pr / l
            a = (pn[0:CHUNK] - lam * pn[CHUNK:2 * CHUNK]).astype(BF16)
            o = jnp.dot(a, vwin, preferred_element_type=F32)
            o_ref[pl.ds(r0, CHUNK), :] = (_rms(o, subln_ref[...]) * (1.0 - lambda_init)).astype(BF16)
            return
        o = jnp.dot(pr.astype(BF16), vwin, preferred_element_type=F32) / l
        for p in range(n_pairs):
            lo = o[2 * p * CHUNK:(2 * p + 1) * CHUNK]
            hi = o[(2 * p + 1) * CHUNK:(2 * p + 2) * CHUNK]
            o_ref[pl.ds(r0, CHUNK), p * LANES:(p + 1) * LANES] = jnp.where(low, lo, hi).astype(BF16)

    sink = sink_ref[...] if mode == "sink" else None

    def chunks(cc, carry):
        ids = [cc * unroll + u for u in range(unroll)]
        ss = [logits(c) for c in ids]
        pls = [_softmax_rows(s, sink) for s in ss]
        for c, (pr, l) in zip(ids, pls):
            attend(c, pr, l)
        return carry

    if n_chunks == unroll:
        chunks(0, 0)
    else:
        lax.fori_loop(0, n_chunks // unroll, chunks, 0)


def _band_attention(q_arr, q_blk0, k_arr, k_blk0, v_arr, v_blk0, bias, *, batch, q_rows, k_rows,
                    n_kv_pairs, n_pairs, win, n_past, mode="plain", sel=None, sinks=None,
                    lam_vecs=None, subln=None, lambda_init=0.0):
    n_chunks = q_rows // CHUNK
    qw = n_pairs * LANES
    rows = 2 * n_pairs * CHUNK
    unroll = math.gcd(n_chunks, BAND_UNROLL)
    cfg = (n_pairs, win, n_past, n_chunks, unroll, mode, lambda_init, sel is not None)
    in_specs = [
        pl.BlockSpec((q_rows, qw), lambda b, j: (b, q_blk0 + j)),
        pl.BlockSpec((k_rows, LANES), lambda b, j: (b, k_blk0 + j)),
        pl.BlockSpec((k_rows, LANES), lambda b, j: (b, v_blk0 + j)),
        pl.BlockSpec((2 * n_pairs, CHUNK, win), lambda b, j: (j, 0, 0)),
    ]
    args = [q_arr, k_arr, v_arr, bias]
    if sel is not None:
        in_specs.append(pl.BlockSpec((None, CHUNK, win), lambda b, j: (b, 0, 0)))
        args.append(sel)
    if mode == "sink":
        in_specs.append(pl.BlockSpec((rows, 1), lambda b, j: (j, 0)))
        args.append(jnp.repeat(sinks, CHUNK).reshape(n_kv_pairs * rows, 1))
    if mode == "diff":
        in_specs.append(pl.BlockSpec((4, HEAD_DIM), lambda b, j: (0, 0)))
        in_specs.append(pl.BlockSpec((1, LANES), lambda b, j: (0, 0)))
        args += [lam_vecs, subln.reshape(1, LANES)]
    scratch = []
    if n_past:
        scratch = [pltpu.VMEM((n_past + k_rows, LANES), BF16)] * 2
    return pl.pallas_call(
        functools.partial(_band_body, cfg),
        grid=(batch, n_kv_pairs),
        in_specs=in_specs,
        out_specs=pl.BlockSpec((q_rows, qw), lambda b, j: (b, j)),
        out_shape=jax.ShapeDtypeStruct((batch * q_rows, n_kv_pairs * qw), BF16),
        scratch_shapes=scratch,
        compiler_params=_params(2),
        name="band_attention",
    )(*args)


def _flash_t_body(cfg, *refs):
    n_groups, n_pairs, mode, lambda_init, has_sel = cfg
    it = iter(refs)
    q_ref, k_ref, v_ref, bias_ref = next(it), next(it), next(it), next(it)
    sel_ref = next(it) if has_sel else None
    if mode == "diff":
        lam_ref, subln_ref = next(it), next(it)
    o_ref = next(it)
    qs_ref, vt_ref, m_ref, l_ref, acc_ref = next(it), next(it), next(it), next(it), next(it)
    t = FLASH_TILE
    n_heads = 2 * n_pairs
    rows = n_heads * t
    seq = k_ref.shape[0]
    i = pl.program_id(2)
    lane = lax.broadcasted_iota(jnp.int32, (t, LANES), 1)
    low = lane < HEAD_DIM
    krow = lax.broadcasted_iota(jnp.int32, (t, rows), 0)
    qcol = lax.broadcasted_iota(jnp.int32, (t, rows), 1) & (t - 1)
    diag_ok = lax.shift_right_arithmetic(krow, 6) <= lax.shift_right_arithmetic(qcol, 6)

    @pl.when(i == 0)
    def _():
        for g in range(n_groups):
            for r0 in range(0, seq, t):
                vb = v_ref[r0:r0 + t, g * LANES:(g + 1) * LANES].astype(F32)
                vt_ref[g, :, r0:r0 + t] = vb.T.astype(BF16)

    for g in range(n_groups):
        qs = []
        for p in range(n_pairs):
            qp = q_ref[:, (g * n_pairs + p) * LANES:(g * n_pairs + p + 1) * LANES]
            qs += [jnp.where(low, qp, jnp.zeros_like(qp)), jnp.where(low, jnp.zeros_like(qp), qp)]
        qs_ref[g] = jnp.concatenate(qs, axis=0)
    m_ref[...] = jnp.full(m_ref.shape, NEG_INF, F32)
    l_ref[...] = jnp.zeros(l_ref.shape, F32)
    acc_ref[...] = jnp.zeros(acc_ref.shape, F32)

    def kstep(j, carry, on_diag=False):
        r0 = pl.multiple_of(j * t, t)
        tt = jnp.minimum(i - j, 2)
        ss = []
        for g in range(n_groups):
            kb = k_ref[pl.ds(r0, t), g * LANES:(g + 1) * LANES]
            s = _dot_nt(kb, qs_ref[g]) + bias_ref[tt, :, g * rows:(g + 1) * rows]
            if on_diag:
                s = jnp.where(diag_ok, s, NEG_INF)
            if has_sel:
                sel = sel_ref[j].astype(F32)
                s = s + jnp.concatenate([sel] * n_heads, axis=1)
            ss.append(s)
        prs = []
        for g in range(n_groups):
            m_old = m_ref[g]
            m_new = jnp.maximum(m_old, jnp.max(ss[g], axis=0, keepdims=True))
            alpha = jnp.exp(m_old - m_new)
            pr = jnp.exp(ss[g] - m_new)
            l_ref[g] = alpha * l_ref[g] + jnp.sum(pr, axis=0, keepdims=True)
            m_ref[g] = m_new
            prs.append((pr.astype(BF16), alpha))
        for g in range(n_groups):
            pr, alpha = prs[g]
            vtb = vt_ref[g, :, pl.ds(r0, t)]
            acc_ref[g] = alpha * acc_ref[g] + jnp.dot(vtb, pr, preferred_element_type=F32)
        return carry

    lax.fori_loop(0, i, kstep, 0)
    kstep(i, 0, on_diag=True)

    if mode == "diff":
        lv = lam_ref[...]
        lam = (jnp.exp(jnp.sum(lv[0:1] * lv[1:2], axis=-1, keepdims=True))
               - jnp.exp(jnp.sum(lv[2:3] * lv[3:4], axis=-1, keepdims=True)) + lambda_init)
    drow_low = lax.broadcasted_iota(jnp.int32, (LANES, t), 0) < HEAD_DIM
    for g in range(n_groups):
        res = acc_ref[g] / l_ref[g]
        for p in range(n_pairs):
            lo = res[:, 2 * p * t:(2 * p + 1) * t]
            hi = res[:, (2 * p + 1) * t:(2 * p + 2) * t]
            if mode == "diff":
                o = _rms((lo - lam * hi).T, subln_ref[...]) * (1.0 - lambda_init)
            else:
                o = jnp.where(drow_low, lo, hi).T
            o_ref[:, (g * n_pairs + p) * LANES:(g * n_pairs + p + 1) * LANES] = o.astype(BF16)


def _flash_attention(q_arr, q_blk0, k_arr, k_blk0, v_arr, v_blk0, bias, *, batch, seq, n_kv_pairs,
                     n_groups, n_pairs, mode="plain", sel=None, lam_vecs=None, subln=None,
                     lambda_init=0.0):
    t = FLASH_TILE
    nq = seq // t
    qw = n_groups * n_pairs * LANES
    kw = n_groups * LANES
    rows = 2 * n_pairs * t
    n_steps = n_kv_pairs // n_groups
    cfg = (n_groups, n_pairs, mode, lambda_init, sel is not None)
    in_specs = [
        pl.BlockSpec((t, qw), lambda b, j, i: (b * nq + i, q_blk0 + j)),
        pl.BlockSpec((seq, kw), lambda b, j, i: (b, k_blk0 + j)),
        pl.BlockSpec((seq, kw), lambda b, j, i: (b, v_blk0 + j)),
        pl.BlockSpec((3, t, n_groups * rows), lambda b, j, i: (0, 0, j)),
    ]
    args = [q_arr, k_arr, v_arr, bias]
    if sel is not None:
        in_specs.append(pl.BlockSpec((None, nq, t, t), lambda b, j, i: (b * nq + i, 0, 0, 0)))
        args.append(sel)
    if mode == "diff":
        in_specs.append(pl.BlockSpec((4, HEAD_DIM), lambda b, j, i: (0, 0)))
        in_specs.append(pl.BlockSpec((1, LANES), lambda b, j, i: (0, 0)))
        args += [lam_vecs, subln.reshape(1, LANES)]
    return pl.pallas_call(
        functools.partial(_flash_t_body, cfg),
        grid=(batch, n_steps, nq),
        in_specs=in_specs,
        out_specs=pl.BlockSpec((t, qw), lambda b, j, i: (b * nq + i, j)),
        out_shape=jax.ShapeDtypeStruct((batch * seq, n_steps * qw), BF16),
        scratch_shapes=[pltpu.VMEM((n_groups, rows, LANES), BF16),
                        pltpu.VMEM((n_groups, LANES, seq), BF16),
                        pltpu.VMEM((n_groups, 1, rows), F32),
                        pltpu.VMEM((n_groups, 1, rows), F32),
                        pltpu.VMEM((n_groups, LANES, rows), F32)],
        compiler_params=_params(3),
        name="flash_attention",
    )(*args)


def _select_t_body(cfg, qi_ref, wi_ref, ki_ref, o_ref, key_ref, qs_ref, cut_ref):
    n_keys, kpos0, qpos0, topk = cfg
    r = qi_ref.shape[0]
    t = FLASH_TILE
    n_tiles = n_keys // t
    qrow0 = qpos0 + pl.program_id(1) * r
    last_kpos = (lax.shift_right_arithmetic(qrow0 + r - 1, 6) + 1) * CHUNK - 1
    n_adm = jnp.minimum(n_tiles,
                        lax.shift_right_arithmetic(last_kpos - kpos0, t.bit_length() - 1) + 1)
    lane = lax.broadcasted_iota(jnp.int32, (r, LANES), 1)
    low = lane < C_IDX_DIM
    wit = wi_ref[...].T
    qchunk = lax.shift_right_arithmetic(qrow0 + lax.broadcasted_iota(jnp.int32, (t, r), 1), 6)
    krow = lax.broadcasted_iota(jnp.int32, (t, r), 0)

    def admissible(kb):
        kpos = kpos0 + kb * t + krow
        return jnp.logical_and(kpos >= 0, lax.shift_right_arithmetic(kpos, 6) <= qchunk)

    qs = []
    for h in range(C_IDX_HEADS):
        qp = qi_ref[:, (h // 2) * LANES:(h // 2 + 1) * LANES]
        zero = jnp.zeros_like(qp)
        qs.append(jnp.where(low, qp, zero) if h % 2 == 0 else jnp.where(low, zero, qp))
    qs_ref[...] = jnp.concatenate(qs, axis=0)

    def score_tile(kb, carry):
        r0 = pl.multiple_of(kb * t, t)
        dots = jnp.maximum(_dot_nt(ki_ref[pl.ds(r0, t), :], qs_ref[...]), 0.0)
        score = jnp.zeros((t, r), F32)
        for h in range(C_IDX_HEADS):
            score = score + wit[h:h + 1, :] * dots[:, h * r:(h + 1) * r]
        score = jnp.where(admissible(kb), score, NEG_INF)
        bits = pltpu.bitcast(score, jnp.int32)
        key_ref[kb] = jnp.where(bits >= 0, bits, bits ^ jnp.int32(0x7FFFFFFF))
        return carry

    lax.fori_loop(0, n_adm, score_tile, 0)

    def count(pred):
        def tile(kb, acc):
            hit = jnp.where(pred(key_ref[kb], kb), 1.0, 0.0)
            return acc + jnp.sum(hit.reshape(t // SUBLANES, SUBLANES, r), axis=0)
        acc = lax.fori_loop(0, n_adm, tile, jnp.zeros((SUBLANES, r), F32))
        return jnp.sum(acc, axis=0, keepdims=True)

    thr = jnp.full((1, r), jnp.int32(-2 ** 31), jnp.int32)
    cand0 = jnp.zeros((1, r), jnp.int32)
    thr = jnp.where(count(lambda k, kb: k >= cand0) >= topk, cand0, thr)

    def value_bit(it, thr):
        cand = thr | lax.shift_left(jnp.int32(1), 30 - it)
        return jnp.where(count(lambda k, kb: k >= cand) >= topk, cand, thr)

    thr = lax.fori_loop(0, 31, value_bit, thr)

    n_ge = count(lambda k, kb: k >= thr)
    cut_ref[...] = jnp.full((1, r), n_keys, jnp.int32)
    n_bits = max(1, (n_keys - 1).bit_length())

    @pl.when(jnp.max(n_ge) > topk)
    def _():
        ties_wanted = topk - count(lambda k, kb: k > thr)

        def index_bit(it, cut):
            cand = cut | lax.shift_left(jnp.int32(1), n_bits - 1 - it)
            before = count(lambda k, kb: jnp.logical_and(k == thr, kb * t + krow < cand))
            return jnp.where(before <= ties_wanted - 1.0, cand, cut)

        cut_ref[...] = lax.fori_loop(0, n_bits, index_bit, jnp.zeros((1, r), jnp.int32))

    cut = cut_ref[...]
    for kb in range(n_tiles):
        @pl.when(kb < n_adm)
        def _():
            kk = key_ref[kb]
            chosen = jnp.logical_or(kk > thr, jnp.logical_and(kk == thr, kb * t + krow <= cut))
            valid = jnp.logical_and(chosen, admissible(kb))
            o_ref[kb] = jnp.where(valid, 0.0, NEG_INF).astype(BF16)

        @pl.when(kb >= n_adm)
        def _():
            o_ref[kb] = jnp.full((t, r), NEG_INF, BF16)


def _select_mask(qi_arr, qi_blk, wi_arr, ki_arr, *, batch, q_rows, n_keys, kpos0, qpos0, topk):
    r = min(SELECT_ROWS, q_rows)
    t = FLASH_TILE
    nq = q_rows // r
    cfg = (n_keys, kpos0, qpos0, topk)
    return pl.pallas_call(
        functools.partial(_select_t_body, cfg),
        grid=(batch, nq),
        in_specs=[
            pl.BlockSpec((r, C_IDX_HEADS * C_IDX_DIM), lambda b, i: (b * nq + i, qi_blk)),
            pl.BlockSpec((r, LANES), lambda b, i: (b * nq + i, 0)),
            pl.BlockSpec((n_keys, LANES), lambda b, i: (b, 0)),
        ],
        out_specs=pl.BlockSpec((None, n_keys // t, t, r), lambda b, i: (b * nq + i, 0, 0, 0)),
        out_shape=jax.ShapeDtypeStruct((batch * nq, n_keys // t, t, r), BF16),
        scratch_shapes=[pltpu.VMEM((n_keys // t, t, r), jnp.int32),
                        pltpu.VMEM((C_IDX_HEADS * r, LANES), BF16),
                        pltpu.VMEM((1, r), jnp.int32)],
        compiler_params=_params(2),
        name="select_mask",
    )(qi_arr, wi_arr, ki_arr)


def _flash_body(cfg, *refs):
    n_groups, n_pairs, mode, lambda_init, has_sel = cfg
    it = iter(refs)
    q_ref, k_ref, v_ref, bias_ref = next(it), next(it), next(it), next(it)
    sel_ref = next(it) if has_sel else None
    if mode == "diff":
        lam_ref, subln_ref = next(it), next(it)
    o_ref = next(it)
    qs_ref, m_ref, l_ref, acc_ref = next(it), next(it), next(it), next(it)
    t = FLASH_TILE
    n_heads = 2 * n_pairs
    rows = n_heads * t
    i = pl.program_id(2)
    lane = lax.broadcasted_iota(jnp.int32, (t, LANES), 1)
    low = lane < HEAD_DIM
    row = lax.broadcasted_iota(jnp.int32, (rows, t), 0) & (t - 1)
    col = lax.broadcasted_iota(jnp.int32, (rows, t), 1)
    diag_ok = lax.shift_right_arithmetic(col, 6) <= lax.shift_right_arithmetic(row, 6)

    for g in range(n_groups):
        qs = []
        for p in range(n_pairs):
            qp = q_ref[:, (g * n_pairs + p) * LANES:(g * n_pairs + p + 1) * LANES]
            qs += [jnp.where(low, qp, jnp.zeros_like(qp)), jnp.where(low, jnp.zeros_like(qp), qp)]
        qs_ref[g] = jnp.concatenate(qs, axis=0)
    m_ref[...] = jnp.full(m_ref.shape, NEG_INF, F32)
    l_ref[...] = jnp.zeros(l_ref.shape, F32)
    acc_ref[...] = jnp.zeros(acc_ref.shape, F32)

    def kstep(j, carry):
        r0 = pl.multiple_of(j * t, t)
        tt = jnp.minimum(i - j, 2)
        on_diag = j >= i
        ss = []
        for g in range(n_groups):
            kb = k_ref[pl.ds(r0, t), g * LANES:(g + 1) * LANES]
            s = _dot_nt(qs_ref[g], kb) + bias_ref[tt, g * n_heads:(g + 1) * n_heads].reshape(rows, t)
            s = jnp.where(jnp.logical_or(jnp.logical_not(on_diag), diag_ok), s, NEG_INF)
            if has_sel:
                s = (s.reshape(n_heads, t, t) + sel_ref[j].astype(F32)[None]).reshape(rows, t)
            ss.append(s)
        prs = []
        for g in range(n_groups):
            m_old = m_ref[g]
            m_new = jnp.maximum(m_old, jnp.max(ss[g], axis=-1, keepdims=True))
            alpha = jnp.exp(m_old - m_new)
            pr = jnp.exp(ss[g] - m_new)
            l_ref[g] = alpha * l_ref[g] + jnp.sum(pr, axis=-1, keepdims=True)
            m_ref[g] = m_new
            prs.append((pr.astype(BF16), alpha))
        for g in range(n_groups):
            pr, alpha = prs[g]
            vb = v_ref[pl.ds(r0, t), g * LANES:(g + 1) * LANES]
            acc_ref[g] = alpha * acc_ref[g] + jnp.dot(pr, vb, preferred_element_type=F32)
        return carry

    lax.fori_loop(0, i + 1, kstep, 0)

    if mode == "diff":
        lv = lam_ref[...]
        lam = (jnp.exp(jnp.sum(lv[0:1] * lv[1:2], axis=-1, keepdims=True))
               - jnp.exp(jnp.sum(lv[2:3] * lv[3:4], axis=-1, keepdims=True)) + lambda_init)
    for g in range(n_groups):
        res = acc_ref[g] / l_ref[g]
        for p in range(n_pairs):
            lo = res[2 * p * t:(2 * p + 1) * t]
            hi = res[(2 * p + 1) * t:(2 * p + 2) * t]
            if mode == "diff":
                o = _rms(lo - lam * hi, subln_ref[...]) * (1.0 - lambda_init)
            else:
                o = jnp.where(low, lo, hi)
            o_ref[:, (g * n_pairs + p) * LANES:(g * n_pairs + p + 1) * LANES] = o.astype(BF16)


def _flash_attention_rowmajor(q_arr, q_blk0, k_arr, k_blk0, v_arr, v_blk0, bias, *, batch, seq,
                              n_kv_pairs, n_groups, n_pairs, mode="plain", sel=None,
                              lam_vecs=None, subln=None, lambda_init=0.0):
    t = FLASH_TILE
    nq = seq // t
    qw = n_groups * n_pairs * LANES
    kw = n_groups * LANES
    rows = 2 * n_pairs * t
    n_steps = n_kv_pairs // n_groups
    cfg = (n_groups, n_pairs, mode, lambda_init, sel is not None)
    in_specs = [
        pl.BlockSpec((t, qw), lambda b, j, i: (b * nq + i, q_blk0 + j)),
        pl.BlockSpec((seq, kw), lambda b, j, i: (b, k_blk0 + j)),
        pl.BlockSpec((seq, kw), lambda b, j, i: (b, v_blk0 + j)),
        pl.BlockSpec((3, 2 * n_pairs * n_groups, t, t), lambda b, j, i: (0, j, 0, 0)),
    ]
    args = [q_arr, k_arr, v_arr, bias]
    if sel is not None:
        in_specs.append(pl.BlockSpec((None, nq, t, t), lambda b, j, i: (b * nq + i, 0, 0, 0)))
        args.append(sel)
    if mode == "diff":
        in_specs.append(pl.BlockSpec((4, HEAD_DIM), lambda b, j, i: (0, 0)))
        in_specs.append(pl.BlockSpec((1, LANES), lambda b, j, i: (0, 0)))
        args += [lam_vecs, subln.reshape(1, LANES)]
    return pl.pallas_call(
        functools.partial(_flash_body, cfg),
        grid=(batch, n_steps, nq),
        in_specs=in_specs,
        out_specs=pl.BlockSpec((t, qw), lambda b, j, i: (b * nq + i, j)),
        out_shape=jax.ShapeDtypeStruct((batch * seq, n_steps * qw), BF16),
        scratch_shapes=[pltpu.VMEM((n_groups, rows, LANES), BF16),
                        pltpu.VMEM((n_groups, rows, 1), F32),
                        pltpu.VMEM((n_groups, rows, 1), F32),
                        pltpu.VMEM((n_groups, rows, LANES), F32)],
        compiler_params=_params(3),
        name="flash_attention",
    )(*args)


def _select_body(cfg, qi_ref, wi_ref, ki_ref, o_ref, key_ref, qs_ref, cut_ref):
    n_keys, kpos0, qpos0, topk = cfg
    r = qi_ref.shape[0]
    t = FLASH_TILE
    n_tiles = n_keys // t
    qrow0 = qpos0 + pl.program_id(1) * r
    last_kpos = (lax.shift_right_arithmetic(qrow0 + r - 1, 6) + 1) * CHUNK - 1
    n_adm = jnp.minimum(n_tiles,
                        lax.shift_right_arithmetic(last_kpos - kpos0, t.bit_length() - 1) + 1)
    lane = lax.broadcasted_iota(jnp.int32, (r, LANES), 1)
    low = lane < C_IDX_DIM
    wi = wi_ref[...]
    qchunk = lax.shift_right_arithmetic(qrow0 + lax.broadcasted_iota(jnp.int32, (r, t), 0), 6)
    col_t = lax.broadcasted_iota(jnp.int32, (r, t), 1)

    def admissible(kb):
        kpos = kpos0 + kb * t + col_t
        return jnp.logical_and(kpos >= 0, lax.shift_right_arithmetic(kpos, 6) <= qchunk)

    qs = []
    for h in range(C_IDX_HEADS):
        qp = qi_ref[:, (h // 2) * LANES:(h // 2 + 1) * LANES]
        zero = jnp.zeros_like(qp)
        qs.append(jnp.where(low, qp, zero) if h % 2 == 0 else jnp.where(low, zero, qp))
    qs_ref[...] = jnp.concatenate(qs, axis=0)

    def score_tile(kb, carry):
        r0 = pl.multiple_of(kb * t, t)
        dots = jnp.maximum(_dot_nt(qs_ref[...], ki_ref[pl.ds(r0, t), :]), 0.0)
        score = jnp.zeros((r, t), F32)
        for h in range(C_IDX_HEADS):
            score = score + wi[:, h:h + 1] * dots[h * r:(h + 1) * r]
        score = jnp.where(admissible(kb), score, NEG_INF)
        bits = pltpu.bitcast(score, jnp.int32)
        key_ref[kb] = jnp.where(bits >= 0, bits, bits ^ jnp.int32(0x7FFFFFFF))
        return carry

    lax.fori_loop(0, n_adm, score_tile, 0)

    def count(pred):
        def tile(kb, acc):
            hit = jnp.where(pred(key_ref[kb], kb), 1.0, 0.0)
            for c in range(0, t, LANES):
                acc = acc + hit[:, c:c + LANES]
            return acc
        acc = lax.fori_loop(0, n_adm, tile, jnp.zeros((r, LANES), F32))
        return jnp.sum(acc, axis=-1, keepdims=True)

    thr = jnp.full((r, 1), jnp.int32(-2 ** 31), jnp.int32)
    cand0 = jnp.zeros((r, 1), jnp.int32)
    thr = jnp.where(count(lambda k, kb: k >= cand0) >= topk, cand0, thr)

    def value_bit(it, thr):
        cand = thr | lax.shift_left(jnp.int32(1), 30 - it)
        return jnp.where(count(lambda k, kb: k >= cand) >= topk, cand, thr)

    thr = lax.fori_loop(0, 31, value_bit, thr)

    n_ge = count(lambda k, kb: k >= thr)
    cut_ref[...] = jnp.full((r, 1), n_keys, jnp.int32)
    n_bits = max(1, (n_keys - 1).bit_length())

    @pl.when(jnp.max(n_ge) > topk)
    def _():
        ties_wanted = topk - count(lambda k, kb: k > thr)

        def index_bit(it, cut):
            cand = cut | lax.shift_left(jnp.int32(1), n_bits - 1 - it)
            before = count(lambda k, kb: jnp.logical_and(k == thr, kb * t + col_t < cand))
            return jnp.where(before <= ties_wanted - 1.0, cand, cut)

        cut_ref[...] = lax.fori_loop(0, n_bits, index_bit, jnp.zeros((r, 1), jnp.int32))

    cut = cut_ref[...]
    for kb in range(n_tiles):
        @pl.when(kb < n_adm)
        def _():
            kk = key_ref[kb]
            chosen = jnp.logical_or(kk > thr, jnp.logical_and(kk == thr, kb * t + col_t <= cut))
            valid = jnp.logical_and(chosen, admissible(kb))
            o_ref[kb] = jnp.where(valid, 0.0, NEG_INF).astype(BF16)

        @pl.when(kb >= n_adm)
        def _():
            o_ref[kb] = jnp.full((r, t), NEG_INF, BF16)


def _select_mask_rowmajor(qi_arr, qi_blk, wi_arr, ki_arr, *, batch, q_rows, n_keys, kpos0, qpos0,
                          topk):
    r = min(SELECT_ROWS, q_rows)
    t = FLASH_TILE
    nq = q_rows // r
    cfg = (n_keys, kpos0, qpos0, topk)
    return pl.pallas_call(
        functools.partial(_select_body, cfg),
        grid=(batch, nq),
        in_specs=[
            pl.BlockSpec((r, C_IDX_HEADS * C_IDX_DIM), lambda b, i: (b * nq + i, qi_blk)),
            pl.BlockSpec((r, LANES), lambda b, i: (b * nq + i, 0)),
            pl.BlockSpec((n_keys, LANES), lambda b, i: (b, 0)),
        ],
        out_specs=pl.BlockSpec((None, n_keys // t, r, t), lambda b, i: (b * nq + i, 0, 0, 0)),
        out_shape=jax.ShapeDtypeStruct((batch * nq, n_keys // t, r, t), BF16),
        scratch_shapes=[pltpu.VMEM((n_keys // t, r, t), jnp.int32),
                        pltpu.VMEM((C_IDX_HEADS * r, LANES), BF16),
                        pltpu.VMEM((r, 1), jnp.int32)],
        compiler_params=_params(2),
        name="select_mask",
    )(qi_arr, wi_arr, ki_arr)


FFN_PAD = 16


def _block_ffn_body(cfg, *refs):
    seq_tiles, seq_rows, stream, final_norm = cfg
    it = iter(refs)
    x_ref, att_ref = next(it), next(it)
    if stream:
        xprev_ref, attprev_ref = next(it), next(it)
    else:
        fix1_ref, fix2_ref = next(it), next(it)
    wo_ref, g_ref, win_ref, cw_ref, cb_ref, wout_ref = (next(it), next(it), next(it), next(it),
                                                        next(it), next(it))
    gfin_ref = next(it) if final_norm else None
    o_ref, st_ref = next(it), next(it)
    hext_ref, act_ref = next(it), next(it)
    tm = x_ref.shape[0]
    pad = FFN_PAD
    gain = g_ref[...]
    if stream:
        aext_ref = next(it)
        aext_ref[0:pad, :] = attprev_ref[...]
        aext_ref[pad:, :] = att_ref[...]
        xext = jnp.concatenate([xprev_ref[...], x_ref[...]], axis=0)
        x1 = xext + jnp.dot(aext_ref[...], wo_ref[...], preferred_element_type=F32)
        o_ref[...] = x1[pad:, :]
        not_start = (pl.program_id(0) % seq_tiles) != 0
        rowi = lax.broadcasted_iota(jnp.int32, (tm + pad, 1), 0)
        keep = jnp.logical_or(rowi >= pad, not_start)
        hext_ref[...] = jnp.where(keep, _rms(x1, gain), 0.0).astype(BF16)
    else:
        x1 = x_ref[...] + jnp.dot(att_ref[...], wo_ref[...], preferred_element_type=F32)
        o_ref[...] = x1
        hext_ref[pad:, :] = _rms(x1, gain).astype(BF16)
        hext_ref[0:pad, :] = jnp.zeros((pad, D_MODEL), BF16)
        assert seq_rows & (seq_rows - 1) == 0
        rmod = lax.broadcasted_iota(jnp.int32, (tm, FF_CHUNK), 0) & (seq_rows - 1)
    hext = hext_ref[...]
    for c0 in range(0, D_FF, FF_CHUNK):
        ag = jnp.dot(hext, win_ref[:, 2 * c0:2 * c0 + 2 * FF_CHUNK], preferred_element_type=F32)
        a_ext = ag[:, :FF_CHUNK]
        gate = ag[pad:, FF_CHUNK:]
        a = a_ext[pad:, :]
        a1 = pltpu.roll(a_ext, 1, 0)[pad:, :]
        a2 = pltpu.roll(a_ext, 2, 0)[pad:, :]
        if not stream:
            a1 = jnp.where(rmod == 0, fix1_ref[:, c0:c0 + FF_CHUNK], a1)
            a2 = jnp.where(rmod <= 1, fix2_ref[:, c0:c0 + FF_CHUNK], a2)
        cw = cw_ref[:, c0:c0 + FF_CHUNK]
        u = cb_ref[:, c0:c0 + FF_CHUNK] + (cw[0:1] * a2 + cw[1:2] * a1 + cw[2:3] * a)
        act_ref[:, c0:c0 + FF_CHUNK] = (jax.nn.silu(u) * gate).astype(BF16)
        st_ref[:, c0:c0 + FF_CHUNK] = a[tm - SUBLANES:, :] if stream else a
    y = o_ref[...] + jnp.dot(act_ref[...], wout_ref[...], preferred_element_type=F32)
    if final_norm:
        y = _rms(y, gfin_ref[...])
    o_ref[...] = y


def _block_ffn(x, attn, w_o, gain, w_in, conv_w, conv_b, w_out, *, tm, seq_tiles, seq_rows,
               fixes=None, final_gain=None):
    n = x.shape[0]
    stream = fixes is None
    final_norm = final_gain is not None
    cfg = (seq_tiles, seq_rows, stream, final_norm)
    in_specs = [pl.BlockSpec((tm, D_MODEL), lambda i: (i, 0)),
                pl.BlockSpec((tm, ATTN_WIDTH), lambda i: (i, 0))]
    args = [x, attn]
    if stream:
        per = tm // FFN_PAD

        def prev(i):
            return (jnp.maximum(i * per - 1, 0), 0)
        in_specs += [pl.BlockSpec((FFN_PAD, D_MODEL), prev), pl.BlockSpec((FFN_PAD, ATTN_WIDTH), prev)]
        args += [x, attn]
    else:
        in_specs += [pl.BlockSpec((tm, D_FF), lambda i: (i, 0))] * 2
        args += list(fixes)
    in_specs += [
        _resident((ATTN_WIDTH, D_MODEL), lambda i: (0, 0)),
        _resident((1, D_MODEL), lambda i: (0, 0)),
        _resident((D_MODEL, 2 * D_FF), lambda i: (0, 0)),
        _resident((CONV_W, D_FF), lambda i: (0, 0)),
        _resident((1, D_FF), lambda i: (0, 0)),
        _resident((D_FF, D_MODEL), lambda i: (0, 0)),
    ]
    args += [w_o, gain.reshape(1, D_MODEL), w_in, conv_w, conv_b.reshape(1, D_FF), w_out]
    if final_norm:
        in_specs.append(_resident((1, D_MODEL), lambda i: (0, 0)))
        args.append(final_gain.reshape(1, D_MODEL))
    if stream:
        st_shape = jax.ShapeDtypeStruct((n // (tm * seq_tiles) * SUBLANES, D_FF), F32)
        st_spec = pl.BlockSpec((SUBLANES, D_FF), lambda i: (i // seq_tiles, 0))
    else:
        st_shape = jax.ShapeDtypeStruct((n, D_FF), F32)
        st_spec = pl.BlockSpec((tm, D_FF), lambda i: (i, 0))
    return pl.pallas_call(
        functools.partial(_block_ffn_body, cfg),
        grid=(n // tm,),
        in_specs=in_specs,
        out_specs=[pl.BlockSpec((tm, D_MODEL), lambda i: (i, 0)), st_spec],
        out_shape=[jax.ShapeDtypeStruct((n, D_MODEL), F32), st_shape],
        scratch_shapes=[pltpu.VMEM((tm + FFN_PAD, D_MODEL), BF16), pltpu.VMEM((tm, D_FF), BF16)]
        + ([pltpu.VMEM((tm + FFN_PAD, ATTN_WIDTH), BF16)] if stream else []),
        compiler_params=_params(1),
        name="block_ffn",
    )(*args)


def _oproj_body(x_ref, a_ref, w_ref, o_ref):
    o_ref[...] = x_ref[...] + jnp.dot(a_ref[...], w_ref[...], preferred_element_type=F32)


def _oproj_residual(x, attn, w_o, tm):
    n = x.shape[0]
    return pl.pallas_call(
        _oproj_body,
        grid=(n // tm,),
        in_specs=[
            pl.BlockSpec((tm, D_MODEL), lambda i: (i, 0)),
            pl.BlockSpec((tm, ATTN_WIDTH), lambda i: (i, 0)),
            _resident((ATTN_WIDTH, D_MODEL), lambda i: (0, 0)),
        ],
        out_specs=pl.BlockSpec((tm, D_MODEL), lambda i: (i, 0)),
        out_shape=jax.ShapeDtypeStruct((n, D_MODEL), F32),
        compiler_params=_params(1),
        name="oproj_residual",
    )(x, attn, w_o)


def _ffn_body(cfg, *refs):
    seq_tiles, seq_rows, stream, final_norm = cfg
    it = iter(refs)
    x_ref = next(it)
    if stream:
        xprev_ref = next(it)
    else:
        fix1_ref, fix2_ref = next(it), next(it)
    g_ref, win_ref, cw_ref, cb_ref, wout_ref = next(it), next(it), next(it), next(it), next(it)
    gfin_ref = next(it) if final_norm else None
    o_ref, st_ref = next(it), next(it)
    hext_ref, act_ref = next(it), next(it)
    tm = x_ref.shape[0]
    pad = SUBLANES
    x = x_ref[...]
    gain = g_ref[...]
    hext_ref[pad:, :] = _rms(x, gain).astype(BF16)
    if stream:
        not_start = ((pl.program_id(0) % seq_tiles) != 0).astype(F32)
        hext_ref[0:pad, :] = (_rms(xprev_ref[...], gain) * not_start).astype(BF16)
    else:
        hext_ref[0:pad, :] = jnp.zeros((pad, D_MODEL), BF16)
    hext = hext_ref[...]
    h = hext[pad:, :]
    is_seq_end = (pl.program_id(0) % seq_tiles) == seq_tiles - 1
    if not stream:
        assert seq_rows & (seq_rows - 1) == 0
        rmod = lax.broadcasted_iota(jnp.int32, (tm, FF_CHUNK), 0) & (seq_rows - 1)
    for c0 in range(0, D_FF, FF_CHUNK):
        a_ext = jnp.dot(hext, win_ref[:, c0:c0 + FF_CHUNK], preferred_element_type=F32)
        gate = jnp.dot(h, win_ref[:, D_FF + c0:D_FF + c0 + FF_CHUNK], preferred_element_type=F32)
        a = a_ext[pad:, :]
        a1 = pltpu.roll(a_ext, 1, 0)[pad:, :]
        a2 = pltpu.roll(a_ext, 2, 0)[pad:, :]
        if not stream:
            a1 = jnp.where(rmod == 0, fix1_ref[:, c0:c0 + FF_CHUNK], a1)
            a2 = jnp.where(rmod <= 1, fix2_ref[:, c0:c0 + FF_CHUNK], a2)
        cw = cw_ref[:, c0:c0 + FF_CHUNK]
        u = cb_ref[:, c0:c0 + FF_CHUNK] + (cw[0:1] * a2 + cw[1:2] * a1 + cw[2:3] * a)
        act_ref[:, c0:c0 + FF_CHUNK] = (jax.nn.silu(u) * gate).astype(BF16)
        if stream:
            @pl.when(is_seq_end)
            def _():
                st_ref[:, c0:c0 + FF_CHUNK] = a[tm - pad:, :]
        else:
            st_ref[:, c0:c0 + FF_CHUNK] = a
    y = x + jnp.dot(act_ref[...], wout_ref[...], preferred_element_type=F32)
    if final_norm:
        y = _rms(y, gfin_ref[...])
    o_ref[...] = y


def _conv_ffn(x, gain, w_in, conv_w, conv_b, w_out, *, tm, seq_tiles, seq_rows, fixes=None,
              final_gain=None):
    n = x.shape[0]
    stream = fixes is None
    final_norm = final_gain is not None
    cfg = (seq_tiles, seq_rows, stream, final_norm)
    in_specs = [pl.BlockSpec((tm, D_MODEL), lambda i: (i, 0))]
    args = [x]
    if stream:
        per = tm // SUBLANES
        in_specs.append(pl.BlockSpec((SUBLANES, D_MODEL), lambda i: (jnp.maximum(i * per - 1, 0), 0)))
        args.append(x)
    else:
        in_specs += [pl.BlockSpec((tm, D_FF), lambda i: (i, 0))] * 2
        args += list(fixes)
    in_specs += [
        _resident((1, D_MODEL), lambda i: (0, 0)),
        _resident((D_MODEL, 2 * D_FF), lambda i: (0, 0)),
        _resident((CONV_W, D_FF), lambda i: (0, 0)),
        _resident((1, D_FF), lambda i: (0, 0)),
        _resident((D_FF, D_MODEL), lambda i: (0, 0)),
    ]
    args += [gain.reshape(1, D_MODEL), w_in, conv_w, conv_b.reshape(1, D_FF), w_out]
    if final_norm:
        in_specs.append(_resident((1, D_MODEL), lambda i: (0, 0)))
        args.append(final_gain.reshape(1, D_MODEL))
    if stream:
        st_shape = jax.ShapeDtypeStruct((n // (tm * seq_tiles) * SUBLANES, D_FF), F32)
        st_spec = pl.BlockSpec((SUBLANES, D_FF), lambda i: (i // seq_tiles, 0))
    else:
        st_shape = jax.ShapeDtypeStruct((n, D_FF), F32)
        st_spec = pl.BlockSpec((tm, D_FF), lambda i: (i, 0))
    return pl.pallas_call(
        functools.partial(_ffn_body, cfg),
        grid=(n // tm,),
        in_specs=in_specs,
        out_specs=[pl.BlockSpec((tm, D_MODEL), lambda i: (i, 0)), st_spec],
        out_shape=[jax.ShapeDtypeStruct((n, D_MODEL), F32), st_shape],
        scratch_shapes=[pltpu.VMEM((tm + SUBLANES, D_MODEL), BF16), pltpu.VMEM((tm, D_FF), BF16)],
        compiler_params=_params(1),
        name="conv_ffn",
    )(*args)


def _t5_bucket(rel):
    half = NUM_BUCKETS // 2
    max_exact = half // 2
    base = jnp.where(rel > 0, half, 0)
    n = jnp.abs(rel)
    nf = jnp.maximum(n, 1).astype(F32)
    large = max_exact + (jnp.log(nf / max_exact) / math.log(T5_MAX_DISTANCE / max_exact)
                         * (half - max_exact)).astype(jnp.int32)
    large = jnp.minimum(large, half - 1)
    return base + jnp.where(n < max_exact, n, large)


def _lookup_heads(table, idx):
    onehot = (idx[..., None] == jnp.arange(table.shape[0])).astype(F32)
    return jnp.einsum("...n,nh->h...", onehot, table.astype(F32), precision=lax.Precision.HIGHEST)


def _t5_bias(table, rel):
    return _lookup_heads(table, _t5_bucket(rel))


def _band_rel(win, n_past_eff):
    i = jnp.arange(CHUNK)[:, None]
    j = jnp.arange(win)[None, :]
    return (j - n_past_eff) - i


def _flash_rel():
    t = FLASH_TILE
    r = jnp.arange(t)[:, None]
    c = jnp.arange(t)[None, :]
    return jnp.stack([c - r - d * t for d in range(3)])


def _flash_bias(bias):
    h, kinds, tq, tk = bias.shape
    return bias.transpose(1, 3, 0, 2).reshape(kinds, tk, h * tq)


def _perm_cols(w, perm):
    return w.reshape(w.shape[0], len(perm), HEAD_DIM)[:, perm, :].reshape(w.shape[0], -1)


def _perm_rows(w, perm):
    return w.reshape(len(perm), HEAD_DIM, w.shape[1])[perm, :, :].reshape(-1, w.shape[1])


def _pad_keys(cache, new, pad, width):
    b = cache.shape[0]
    allk = jnp.concatenate([cache.reshape(b, -1, width), new.reshape(b, -1, width)], axis=1)
    padded = jnp.pad(allk, ((0, 0), (pad, 0), (0, 0))).astype(BF16)
    return allk, padded.reshape(-1, width)


def kernel(x_prompt, x_sample, cache_a_k, cache_a_v, cache_b_k, cache_b_v, cache_c_k, cache_c_v,
           cache_c_kidx, cache_d_k, cache_d_v, state_ffn_conv, t5_table, norm_mix, norm_ffn,
           norm_final, a_w_qkv, a_w_o, a_rel_bias, b_w_qkv, b_w_o, b_sinks, c_w_qkv, c_w_o,
           c_w_idx_q, c_w_idx_k, c_idx_k_norm, c_w_idx_w, d_w_qkv, d_w_o, d_lambda_q1,
           d_lambda_k1, d_lambda_q2, d_lambda_k2, d_subln, ffn_w_in, ffn_conv_w, ffn_conv_b,
           ffn_w_out):
    bp, seq, d = x_prompt.shape
    bs, ts, _ = x_sample.shape
    past = cache_c_k.shape[1]
    assert d == D_MODEL and ts == CHUNK and seq % ROW_TILE == 0 and past % CHUNK == 0
    n_p, n_s = bp * seq, bs * ts
    seq_tiles = seq // ROW_TILE
    depth = norm_mix.shape[0]
    scale = HEAD_DIM ** -0.5
    perm = jnp.array(GQA_PERM)
    none_aux = jnp.zeros((1, LANES), F32)
    kvw = KV_HEADS * HEAD_DIM

    xp = x_prompt.reshape(n_p, d)
    xs = x_sample.reshape(n_s, d)

    def cast_dests(col_dests):
        secs = []
        for c0, c1, fn in col_dests:
            for c in range(c0, c1, PROJ_CHUNK):
                secs.append((c, min(PROJ_CHUNK, c1 - c), fn(c)))
        return secs

    layer = 0
    w = a_w_qkv.astype(BF16)
    aw = ATTN_WIDTH
    a_keep = min(A_PAST, seq)
    secs_p = cast_dests([
        (0, aw, lambda c: [("cast", 0, c, scale)]),
        (aw, 2 * aw, lambda c: [("cast", 0, c, None), ("tail", 1, c - aw, a_keep)]),
        (2 * aw, 3 * aw, lambda c: [("cast", 0, c, None), ("tail", 2, c - 2 * aw, a_keep)]),
    ])
    qkv, a_k_p, a_v_p = _norm_proj(
        xp, norm_mix[layer], w, none_aux, secs_p,
        [(n_p, 3 * aw, BF16, ROW_TILE, "all"), (bp * a_keep, aw, F32, a_keep, "tail"),
         (bp * a_keep, aw, F32, a_keep, "tail")], seq_tiles, ROW_TILE)
    secs_s = cast_dests([
        (0, aw, lambda c: [("cast", 0, c, scale)]),
        (aw, 3 * aw, lambda c: [("cast", 1, c - aw, None)]),
    ])
    q_s, kv_s = _norm_proj(xs, norm_mix[layer], w, none_aux, secs_s,
                           [(n_s, aw, BF16, n_s, "all"), (n_s, 2 * aw, F32, n_s, "all")], 1, n_s)
    a_pad = CHUNK
    a_win = A_PAST + CHUNK + a_pad
    rel = _band_rel(a_win, A_PAST + a_pad)
    bias_a = _lookup_heads(a_rel_bias, jnp.clip(rel, -A_CLIP, A_CLIP) + A_CLIP)
    bias_a = jnp.where((jnp.arange(a_win) >= a_pad)[None, None, :], bias_a, NEG_INF)
    att_p = _band_attention(qkv, 0, qkv, aw // LANES, qkv, 2 * aw // LANES, bias_a, batch=bp,
                            q_rows=seq, k_rows=seq, n_kv_pairs=N_HEADS // 2, n_pairs=1,
                            win=a_win, n_past=A_PAST + a_pad)
    ks = kv_s[:, :aw].reshape(bs, ts, aw)
    vs = kv_s[:, aw:].reshape(bs, ts, aw)
    k_all, k_in = _pad_keys(cache_a_k, ks, a_pad, aw)
    v_all, v_in = _pad_keys(cache_a_v, vs, a_pad, aw)
    assert k_all.shape[1] + a_pad == a_win
    att_s = _band_attention(q_s, 0, k_in, 0, v_in, 0, bias_a, batch=bs, q_rows=ts, k_rows=a_win,
                            n_kv_pairs=N_HEADS // 2, n_pairs=1, win=a_win, n_past=0)
    a_keep_s = min(A_PAST, k_all.shape[1])
    a_k_prompt = a_k_p.reshape(bp, a_keep, N_HEADS, HEAD_DIM)
    a_v_prompt = a_v_p.reshape(bp, a_keep, N_HEADS, HEAD_DIM)
    a_k_sample = k_all[:, -a_keep_s:].reshape(bs, a_keep_s, N_HEADS, HEAD_DIM)
    a_v_sample = v_all[:, -a_keep_s:].reshape(bs, a_keep_s, N_HEADS, HEAD_DIM)
    w_o = a_w_o.astype(BF16)
    xp, xs, conv_p0, conv_s0 = _ffn_layer(xp, xs, att_p, att_s, w_o, layer, bp, bs, seq_tiles, ts, norm_ffn,
                                          ffn_w_in, ffn_conv_w, ffn_conv_b, ffn_w_out,
                                          state_ffn_conv, None)

    layer = 1
    w = jnp.concatenate([_perm_cols(b_w_qkv[:, :aw], perm), b_w_qkv[:, aw:]], axis=1).astype(BF16)
    b_keep = min(B_WINDOW, seq)
    secs_p = cast_dests([
        (0, aw, lambda c: [("cast", 0, c, scale)]),
        (aw, aw + kvw, lambda c: [("cast", 0, c, None), ("tail", 1, c - aw, b_keep)]),
        (aw + kvw, aw + 2 * kvw, lambda c: [("cast", 0, c, None), ("tail", 2, c - aw - kvw, b_keep)]),
    ])
    qkv, b_k_p, b_v_p = _norm_proj(
        xp, norm_mix[layer], w, none_aux, secs_p,
        [(n_p, aw + 2 * kvw, BF16, ROW_TILE, "all"), (bp * b_keep, kvw, F32, b_keep, "tail"),
         (bp * b_keep, kvw, F32, b_keep, "tail")], seq_tiles, ROW_TILE)
    secs_s = cast_dests([
        (0, aw, lambda c: [("cast", 0, c, scale)]),
        (aw, aw + 2 * kvw, lambda c: [("cast", 1, c - aw, None)]),
    ])
    q_s, kv_s = _norm_proj(xs, norm_mix[layer], w, none_aux, secs_s,
                           [(n_s, aw, BF16, n_s, "all"), (n_s, 2 * kvw, F32, n_s, "all")], 1, n_s)
    b_pad = CHUNK
    b_win = B_WINDOW + CHUNK + b_pad
    bias_b = _t5_bias(t5_table, _band_rel(b_win, B_WINDOW + b_pad))[perm]
    bias_b = jnp.where((jnp.arange(b_win) >= b_pad)[None, None, :], bias_b, NEG_INF)
    sinks = b_sinks.astype(F32)[perm]
    n_kvp = KV_HEADS // 2
    gq = N_HEADS // KV_HEADS
    att_p = _band_attention(qkv, 0, qkv, aw // LANES, qkv, (aw + kvw) // LANES, bias_b, batch=bp,
                            q_rows=seq, k_rows=seq, n_kv_pairs=n_kvp, n_pairs=gq, win=b_win,
                            n_past=B_WINDOW + b_pad, mode="sink", sinks=sinks)
    ks = kv_s[:, :kvw].reshape(bs, ts, kvw)
    vs = kv_s[:, kvw:].reshape(bs, ts, kvw)
    k_all, k_in = _pad_keys(cache_b_k, ks, b_pad, kvw)
    v_all, v_in = _pad_keys(cache_b_v, vs, b_pad, kvw)
    assert k_all.shape[1] + b_pad == b_win
    att_s = _band_attention(q_s, 0, k_in, 0, v_in, 0, bias_b, batch=bs, q_rows=ts, k_rows=b_win,
                            n_kv_pairs=n_kvp, n_pairs=gq, win=b_win, n_past=0, mode="sink",
                            sinks=sinks)
    b_keep_s = min(B_WINDOW, k_all.shape[1])
    b_k_prompt = b_k_p.reshape(bp, b_keep, KV_HEADS, HEAD_DIM)
    b_v_prompt = b_v_p.reshape(bp, b_keep, KV_HEADS, HEAD_DIM)
    b_k_sample = k_all[:, -b_keep_s:].reshape(bs, b_keep_s, KV_HEADS, HEAD_DIM)
    b_v_sample = v_all[:, -b_keep_s:].reshape(bs, b_keep_s, KV_HEADS, HEAD_DIM)
    w_o = _perm_rows(b_w_o, perm).astype(BF16)
    xp, xs, conv_p1, conv_s1 = _ffn_layer(xp, xs, att_p, att_s, w_o, layer, bp, bs, seq_tiles, ts, norm_ffn,
                                          ffn_w_in, ffn_conv_w, ffn_conv_b, ffn_w_out,
                                          state_ffn_conv, None)

    layer = 2
    iw = C_IDX_HEADS * C_IDX_DIM
    w_idx_w = jnp.pad(c_w_idx_w, ((0, 0), (0, LANES - C_IDX_HEADS)))
    w = jnp.concatenate([_perm_cols(c_w_qkv[:, :aw], perm), c_w_qkv[:, aw:], c_w_idx_q,
                         c_w_idx_k, c_w_idx_k, w_idx_w], axis=1).astype(BF16)
    c_qkv_w = aw + 2 * kvw
    col_ki = c_qkv_w + iw
    col_wi = col_ki + LANES
    knorm = jnp.concatenate([c_idx_k_norm, c_idx_k_norm]).astype(F32).reshape(1, LANES)
    wi_scale = C_IDX_HEADS ** -0.5

    def c_sections(k_out, v_out, qkv_out):
        secs = cast_dests([
            (0, aw, lambda c: [("cast", qkv_out, c, scale)]),
            (aw, aw + kvw, lambda c: [("cast", qkv_out, c, None), ("cast", k_out, c - aw, None)]),
            (aw + kvw, c_qkv_w,
             lambda c: [("cast", qkv_out, c, None), ("cast", v_out, c - aw - kvw, None)]),
            (c_qkv_w, col_ki, lambda c: [("cast", qkv_out, c, C_IDX_DIM ** -0.5)]),
        ])
        secs.append((col_ki, LANES, [("kidx", 3, 4)]))
        secs.append((col_wi, LANES, [("cast", 5, 0, wi_scale)]))
        return secs

    def c_outs(n, tm):
        return [(n, c_qkv_w + iw, BF16, tm, "all"), (n, kvw, F32, tm, "all"),
                (n, kvw, F32, tm, "all"), (n, LANES, BF16, tm, "all"),
                (n, C_IDX_DIM, F32, tm, "all"), (n, LANES, F32, tm, "all")]

    qkv, c_k_p, c_v_p, ki_p, kidx_p, wi_p = _norm_proj(
        xp, norm_mix[layer], w, knorm, c_sections(1, 2, 0), c_outs(n_p, ROW_TILE), seq_tiles,
        ROW_TILE)
    qkv_s, c_k_s, c_v_s, ki_s, kidx_s, wi_s = _norm_proj(
        xs, norm_mix[layer], w, knorm, c_sections(1, 2, 0), c_outs(n_s, n_s), 1, n_s)
    qi_blk = c_qkv_w // iw
    assert qi_blk * iw == c_qkv_w
    t = FLASH_TILE
    assert SELECT_ROWS == t
    sel_p = _select_mask(qkv, qi_blk, wi_p, ki_p, batch=bp, q_rows=seq, n_keys=seq, kpos0=0,
                         qpos0=0, topk=min(C_TOPK, seq // 4))
    bias_c = _flash_bias(_t5_bias(t5_table, _flash_rel())[perm])
    att_p = _flash_attention(qkv, 0, qkv, aw // LANES, qkv, (aw + kvw) // LANES, bias_c, batch=bp,
                             seq=seq, n_kv_pairs=n_kvp, n_groups=1, n_pairs=gq, sel=sel_p)
    n_keys_s = past + ts
    c_pad = (-n_keys_s) % t
    c_win = n_keys_s + c_pad
    k_all, k_in = _pad_keys(cache_c_k, c_k_s.reshape(bs, ts, kvw), c_pad, kvw)
    v_all, v_in = _pad_keys(cache_c_v, c_v_s.reshape(bs, ts, kvw), c_pad, kvw)
    del ki_s
    _, ki_in = _pad_keys(cache_c_kidx, kidx_s.reshape(bs, ts, C_IDX_DIM), c_pad, C_IDX_DIM)
    ki_in = jnp.concatenate([ki_in, ki_in], axis=1)
    qi_s = jnp.pad(qkv_s[:, c_qkv_w:].reshape(bs, ts, iw), ((0, 0), (0, LANES - ts), (0, 0)))
    wi_s = jnp.pad(wi_s.reshape(bs, ts, LANES), ((0, 0), (0, LANES - ts), (0, 0)))
    sel_s = _select_mask(qi_s.reshape(bs * LANES, iw), 0, wi_s.reshape(bs * LANES, LANES), ki_in,
                         batch=bs, q_rows=LANES, n_keys=c_win, kpos0=-c_pad, qpos0=past,
                         topk=min(C_TOPK, n_keys_s // 4))
    sel_s = sel_s[..., :ts].transpose(0, 3, 1, 2).reshape(bs, ts, c_win)
    rel_s = (jnp.arange(c_win)[None, :] - c_pad) - (past + jnp.arange(ts)[:, None])
    pad_ok = (jnp.arange(c_win) >= c_pad)[None, None, :]
    bias_s_t5 = jnp.where(pad_ok, _t5_bias(t5_table, rel_s), NEG_INF)
    att_s = _band_attention(qkv_s, 0, k_in, 0, v_in, 0, bias_s_t5[perm], batch=bs, q_rows=ts,
                            k_rows=c_win, n_kv_pairs=n_kvp, n_pairs=gq, win=c_win, n_past=0,
                            sel=sel_s)
    c_k_prompt = c_k_p.reshape(bp, seq, KV_HEADS, HEAD_DIM)
    c_v_prompt = c_v_p.reshape(bp, seq, KV_HEADS, HEAD_DIM)
    c_kidx_prompt = kidx_p.reshape(bp, seq, C_IDX_DIM)
    c_k_sample = c_k_s.reshape(bs, ts, KV_HEADS, HEAD_DIM)
    c_v_sample = c_v_s.reshape(bs, ts, KV_HEADS, HEAD_DIM)
    c_kidx_sample = kidx_s.reshape(bs, ts, C_IDX_DIM)
    w_o = _perm_rows(c_w_o, perm).astype(BF16)
    xp, xs, conv_p2, conv_s2 = _ffn_layer(xp, xs, att_p, att_s, w_o, layer, bp, bs, seq_tiles, ts, norm_ffn,
                                          ffn_w_in, ffn_conv_w, ffn_conv_b, ffn_w_out,
                                          state_ffn_conv, None)

    layer = 3
    lambda_init = 0.8 - 0.6 * math.exp(-0.3 * layer)
    w = d_w_qkv.astype(BF16)
    lam_vecs = jnp.stack([d_lambda_q1, d_lambda_k1, d_lambda_q2, d_lambda_k2]).astype(F32)
    secs = cast_dests([
        (0, aw, lambda c: [("cast", 0, c, scale)]),
        (aw, 2 * aw, lambda c: [("cast", 0, c, None), ("cast", 1, c - aw, None)]),
        (2 * aw, 3 * aw, lambda c: [("cast", 0, c, None), ("cast", 2, c - 2 * aw, None)]),
    ])

    def d_outs(n, tm):
        return [(n, 3 * aw, BF16, tm, "all"), (n, aw, F32, tm, "all"), (n, aw, F32, tm, "all")]

    qkv, d_k_p, d_v_p = _norm_proj(xp, norm_mix[layer], w, none_aux, secs, d_outs(n_p, ROW_TILE),
                                   seq_tiles, ROW_TILE)
    qkv_s, d_k_s, d_v_s = _norm_proj(xs, norm_mix[layer], w, none_aux, secs, d_outs(n_s, n_s), 1,
                                     n_s)
    bias_d = _flash_bias(_t5_bias(t5_table, _flash_rel()))
    d_grp = 4
    att_p = _flash_attention(qkv, 0, qkv, aw // (d_grp * LANES), qkv, 2 * aw // (d_grp * LANES),
                             bias_d, batch=bp, seq=seq, n_kv_pairs=D_HEADS, n_groups=d_grp,
                             n_pairs=1, mode="diff",
                             lam_vecs=lam_vecs, subln=d_subln.astype(F32), lambda_init=lambda_init)
    k_all, k_in = _pad_keys(cache_d_k, d_k_s.reshape(bs, ts, aw), c_pad, aw)
    v_all, v_in = _pad_keys(cache_d_v, d_v_s.reshape(bs, ts, aw), c_pad, aw)
    att_s = _band_attention(qkv_s, 0, k_in, 0, v_in, 0, bias_s_t5, batch=bs, q_rows=ts,
                            k_rows=c_win, n_kv_pairs=D_HEADS, n_pairs=1, win=c_win, n_past=0,
                            mode="diff", lam_vecs=lam_vecs, subln=d_subln.astype(F32),
                            lambda_init=lambda_init)
    d_k_prompt = d_k_p.reshape(bp, seq, 2 * D_HEADS, HEAD_DIM)
    d_v_prompt = d_v_p.reshape(bp, seq, D_HEADS, 2 * HEAD_DIM)
    d_k_sample = d_k_s.reshape(bs, ts, 2 * D_HEADS, HEAD_DIM)
    d_v_sample = d_v_s.reshape(bs, ts, D_HEADS, 2 * HEAD_DIM)
    w_o = d_w_o.astype(BF16)
    xp, xs, conv_p3, conv_s3 = _ffn_layer(xp, xs, att_p, att_s, w_o, layer, bp, bs, seq_tiles, ts, norm_ffn,
                                          ffn_w_in, ffn_conv_w, ffn_conv_b, ffn_w_out,
                                          state_ffn_conv, norm_final)
    assert depth == 4

    y_prompt = xp.reshape(bp, seq, d)
    y_sample = xs.reshape(bs, ts, d)
    ffn_conv_prompt = jnp.stack([conv_p0, conv_p1, conv_p2, conv_p3])
    ffn_conv_sample = jnp.stack([conv_s0, conv_s1, conv_s2, conv_s3])
    return (y_prompt, y_sample,
            a_k_prompt, a_v_prompt, a_k_sample, a_v_sample,
            b_k_prompt, b_v_prompt, b_k_sample, b_v_sample,
            c_k_prompt, c_v_prompt, c_kidx_prompt, c_k_sample, c_v_sample, c_kidx_sample,
            d_k_prompt, d_v_prompt, d_k_sample, d_v_sample,
            ffn_conv_prompt, ffn_conv_sample)


def _ffn_layer(xp, xs, att_p, att_s, w_o, layer, bp, bs, seq_tiles, ts, norm_ffn, ffn_w_in,
               ffn_conv_w, ffn_conv_b, ffn_w_out, state, final_gain):
    n_ch = D_FF // FF_CHUNK
    w_in = ffn_w_in[layer].astype(BF16)
    w_in = w_in.reshape(D_MODEL, 2, n_ch, FF_CHUNK).transpose(0, 2, 1, 3).reshape(D_MODEL, 2 * D_FF)
    w_out = ffn_w_out[layer].astype(BF16)
    cw, cb, gain = ffn_conv_w[layer], ffn_conv_b[layer], norm_ffn[layer]
    xp, tail = _block_ffn(xp, att_p, w_o, gain, w_in, cw, cb, w_out, tm=ROW_TILE,
                          seq_tiles=seq_tiles, seq_rows=ROW_TILE * seq_tiles,
                          final_gain=final_gain)
    conv_p = tail.reshape(bp, SUBLANES, D_FF)[:, SUBLANES - (CONV_W - 1):, :]
    n_s = bs * ts
    st = state[layer]
    zeros = jnp.zeros((bs, ts - 2, D_FF), F32)
    fix1 = jnp.concatenate([st[:, 1:2], jnp.zeros((bs, 1, D_FF), F32), zeros], axis=1)
    fix2 = jnp.concatenate([st, zeros], axis=1)
    xs, a_s = _block_ffn(xs, att_s, w_o, gain, w_in, cw, cb, w_out, tm=n_s, seq_tiles=1,
                         seq_rows=ts, fixes=(fix1.reshape(n_s, D_FF), fix2.reshape(n_s, D_FF)),
                         final_gain=final_gain)
    conv_s = a_s.reshape(bs, ts, D_FF)[:, ts - (CONV_W - 1):, :]
    return xp, xs, conv_p, conv_s
```

```python
import functools
import math

import jax
import jax.numpy as jnp
from jax import lax
from jax.experimental import pallas as pl
from jax.experimental.pallas import tpu as pltpu

F32 = jnp.float32
BF16 = jnp.bfloat16

D_MODEL = 1024
CHUNK = 64
N_HEADS = 16
HEAD_DIM = 64
ATTN_WIDTH = N_HEADS * HEAD_DIM
NUM_BUCKETS = 32
T5_MAX_DISTANCE = 128
A_PAST = 512
A_CLIP = 64
B_WINDOW = 128
KV_HEADS = 4
C_TOPK = 256
C_IDX_HEADS = 8
C_IDX_DIM = 64
D_HEADS = 8
D_FF = 2816
CONV_W = 3
RMS_EPS = 1e-6
NEG_INF = -1e30
LOG2E = math.log2(math.e)

LANES = 128
SUBLANES = 8
ROW_TILE = 512
FLASH_TILE = 256
BAND_UNROLL = 4
SELECT_ROWS = FLASH_TILE
FF_CHUNK = 256
PROJ_CHUNK = 512
VMEM_LIMIT = 56 * 1024 * 1024

GQA_PERM = tuple(8 * j + 4 * half + t for j in range(2) for t in range(4) for half in range(2))


def _params(n_grid_dims):
    return pltpu.CompilerParams(
        dimension_semantics=("arbitrary",) * n_grid_dims, vmem_limit_bytes=VMEM_LIMIT)


def _resident(shape, index_map):
    return pl.BlockSpec(shape, index_map, pipeline_mode=pl.Buffered(1))


def _rms(x, gain):
    ms = jnp.mean(x * x, axis=-1, keepdims=True)
    return (x * lax.rsqrt(ms + RMS_EPS)) * gain


def _dot_nt(a, b):
    return lax.dot_general(a, b, (((1,), (1,)), ((), ())), preferred_element_type=F32)


def _norm_proj_body(sections, seq_tiles, n_out, x_ref, g_ref, w_ref, aux_ref, *out_refs):
    tm = x_ref.shape[0]
    h = _rms(x_ref[...], g_ref[...]).astype(BF16)
    for col0, width, dests in sections:
        y = jnp.dot(h, w_ref[:, col0:col0 + width], preferred_element_type=F32)
        for dest in dests:
            kind = dest[0]
            if kind == "cast":
                _, o, oc, scale = dest
                val = y if scale is None else y * scale
                out_refs[o][:, oc:oc + width] = val.astype(out_refs[o].dtype)
            elif kind == "tail":
                _, o, oc, nrows = dest
                out_refs[o][:, oc:oc + width] = y[tm - nrows:, :]
            elif kind == "heads":
                _, o, head0, dh, nh, nrows = dest
                src = y if nrows is None else y[tm - nrows:, :]
                for hh in range(width // dh):
                    out_refs[o][pl.ds(head0 + hh, src.shape[0], stride=nh), :] = (
                        src[:, hh * dh:(hh + 1) * dh])
            elif kind == "kidx":
                _, o_dup, o_f32 = dest
                yn = _rms(y, aux_ref[...])
                out_refs[o_dup][...] = yn.astype(BF16)
                out_refs[o_f32][...] = yn[:, :C_IDX_DIM]
            else:
                raise ValueError(kind)


def _norm_proj(x, gain, w, aux, sections, outs, seq_tiles, tm):
    n = x.shape[0]
    grid = (n // tm,)
    out_shape, out_specs = [], []
    for rows, width, dtype, blk_rows, mode in outs:
        out_shape.append(jax.ShapeDtypeStruct((rows, width), dtype))
        if mode == "all":
            out_specs.append(pl.BlockSpec((blk_rows, width), lambda i: (i, 0)))
        else:
            out_specs.append(pl.BlockSpec((blk_rows, width), lambda i: (i // seq_tiles, 0)))
    body = functools.partial(_norm_proj_body, sections, seq_tiles, len(outs))
    return pl.pallas_call(
        body,
        grid=grid,
        in_specs=[
            pl.BlockSpec((tm, D_MODEL), lambda i: (i, 0)),
            _resident((1, D_MODEL), lambda i: (0, 0)),
            _resident(w.shape, lambda i: (0, 0)),
            _resident(aux.shape, lambda i: (0, 0)),
        ],
        out_specs=out_specs,
        out_shape=out_shape,
        compiler_params=_params(1),
        name="norm_proj",
    )(x, gain.reshape(1, D_MODEL), w, aux)


def _split_sections(col0, total, width, dests_fn):
    return [(col0 + c, width, dests_fn(c)) for c in range(0, total, width)]


def _softmax_rows(s, sink):
    m = jnp.max(s, axis=-1, keepdims=True)
    if sink is not None:
        m = jnp.maximum(m, sink)
    p = jnp.exp2(s - m)
    l = jnp.sum(p, axis=-1, keepdims=True)
    if sink is not None:
        l = l + jnp.exp2(sink - m)
    return p, l


def _band_body(cfg, *refs):
    n_pairs, win, n_past, n_chunks, unroll, mode, lambda_init, has_sel = cfg
    it = iter(refs)
    q_ref, k_ref, v_ref, bias_ref = next(it), next(it), next(it), next(it)
    sel_ref = next(it) if has_sel else None
    sink_ref = next(it) if mode == "sink" else None
    if mode == "diff":
        lam_ref, subln_ref = next(it), next(it)
    o_ref = next(it)
    if n_past:
        kp_ref, vp_ref = next(it), next(it)
        kp_ref[0:n_past, :] = jnp.zeros((n_past, LANES), BF16)
        vp_ref[0:n_past, :] = jnp.zeros((n_past, LANES), BF16)
        kp_ref[n_past:, :] = k_ref[...]
        vp_ref[n_past:, :] = v_ref[...]
    else:
        kp_ref, vp_ref = k_ref, v_ref

    n_heads = 2 * n_pairs
    rows = n_heads * CHUNK
    lane = lax.broadcasted_iota(jnp.int32, (CHUNK, LANES), 1)
    col = lax.broadcasted_iota(jnp.int32, (rows, win), 1)
    low = lane < HEAD_DIM
    if mode == "diff":
        lv = lam_ref[...]
        lam = (jnp.exp(jnp.sum(lv[0:1] * lv[1:2], axis=-1, keepdims=True))
               - jnp.exp(jnp.sum(lv[2:3] * lv[3:4], axis=-1, keepdims=True)) + lambda_init)

    def logits(c):
        r0 = pl.multiple_of(c * CHUNK, CHUNK)
        kwin = kp_ref[pl.ds(r0, win), :]
        qs = []
        for p in range(n_pairs):
            qp = q_ref[pl.ds(r0, CHUNK), p * LANES:(p + 1) * LANES]
            qs += [jnp.where(low, qp, jnp.zeros_like(qp)), jnp.where(low, jnp.zeros_like(qp), qp)]
        s = _dot_nt(jnp.concatenate(qs, axis=0), kwin) + bias_ref[...].reshape(rows, win)
        if n_past:
            s = jnp.where(col >= n_past - c * CHUNK, s, NEG_INF)
        if has_sel:
            s = (s.reshape(n_heads, CHUNK, win) + sel_ref[...].astype(F32)[None]).reshape(rows, win)
        return s

    def attend(c, pr, l):
        r0 = pl.multiple_of(c * CHUNK, CHUNK)
        vwin = vp_ref[pl.ds(r0, win), :]
        if mode == "diff":
            pn = pr / l
            a = (pn[0:CHUNK] - lam * pn[CHUNK:2 * CHUNK]).astype(BF16)
            o = jnp.dot(a, vwin, preferred_element_type=F32)
            o_ref[pl.ds(r0, CHUNK), :] = (_rms(o, subln_ref[...]) * (1.0 - lambda_init)).astype(BF16)
            return
        o = jnp.dot(pr.astype(BF16), vwin, preferred_element_type=F32) / l
        for p in range(n_pairs):
            lo = o[2 * p * CHUNK:(2 * p + 1) * CHUNK]
            hi = o[(2 * p + 1) * CHUNK:(2 * p + 2) * CHUNK]
            o_ref[pl.ds(r0, CHUNK), p * LANES:(p + 1) * LANES] = jnp.where(low, lo, hi).astype(BF16)

    sink = sink_ref[...] if mode == "sink" else None

    def chunks(cc, carry):
        ids = [cc * unroll + u for u in range(unroll)]
        ss = [logits(c) for c in ids]
        pls = [_softmax_rows(s, sink) for s in ss]
        for c, (pr, l) in zip(ids, pls):
            attend(c, pr, l)
        return carry

    if n_chunks == unroll:
        chunks(0, 0)
    else:
        lax.fori_loop(0, n_chunks // unroll, chunks, 0)


def _band_attention(q_arr, q_blk0, k_arr, k_blk0, v_arr, v_blk0, bias, *, batch, q_rows, k_rows,
                    n_kv_pairs, n_pairs, win, n_past, mode="plain", sel=None, sinks=None,
                    lam_vecs=None, subln=None, lambda_init=0.0, chunks_in_flight=BAND_UNROLL):
    n_chunks = q_rows // CHUNK
    qw = n_pairs * LANES
    rows = 2 * n_pairs * CHUNK
    unroll = math.gcd(n_chunks, chunks_in_flight)
    cfg = (n_pairs, win, n_past, n_chunks, unroll, mode, lambda_init, sel is not None)
    in_specs = [
        pl.BlockSpec((q_rows, qw), lambda b, j: (b, q_blk0 + j)),
        pl.BlockSpec((k_rows, LANES), lambda b, j: (b, k_blk0 + j)),
        pl.BlockSpec((k_rows, LANES), lambda b, j: (b, v_blk0 + j)),
        pl.BlockSpec((2 * n_pairs, CHUNK, win), lambda b, j: (j, 0, 0)),
    ]
    args = [q_arr, k_arr, v_arr, bias]
    if sel is not None:
        in_specs.append(pl.BlockSpec((None, CHUNK, win), lambda b, j: (b, 0, 0)))
        args.append(sel)
    if mode == "sink":
        in_specs.append(pl.BlockSpec((rows, 1), lambda b, j: (j, 0)))
        args.append(jnp.repeat(sinks, CHUNK).reshape(n_kv_pairs * rows, 1))
    if mode == "diff":
        in_specs.append(pl.BlockSpec((4, HEAD_DIM), lambda b, j: (0, 0)))
        in_specs.append(pl.BlockSpec((1, LANES), lambda b, j: (0, 0)))
        args += [lam_vecs, subln.reshape(1, LANES)]
    scratch = []
    if n_past:
        scratch = [pltpu.VMEM((n_past + k_rows, LANES), BF16)] * 2
    return pl.pallas_call(
        functools.partial(_band_body, cfg),
        grid=(batch, n_kv_pairs),
        in_specs=in_specs,
        out_specs=pl.BlockSpec((q_rows, qw), lambda b, j: (b, j)),
        out_shape=jax.ShapeDtypeStruct((batch * q_rows, n_kv_pairs * qw), BF16),
        scratch_shapes=scratch,
        compiler_params=_params(2),
        name="band_attention",
    )(*args)


def _flash_t_body(cfg, *refs):
    n_groups, n_pairs, mode, lambda_init, has_sel = cfg
    it = iter(refs)
    q_ref, k_ref, v_ref, bias_ref = next(it), next(it), next(it), next(it)
    sel_ref = next(it) if has_sel else None
    if mode == "diff":
        lam_ref, subln_ref = next(it), next(it)
    o_ref = next(it)
    qs_ref, vt_ref, m_ref, l_ref, acc_ref = next(it), next(it), next(it), next(it), next(it)
    sa_ref, sb_ref, pa_ref, pb_ref, ala_ref, alb_ref = (next(it), next(it), next(it), next(it),
                                                        next(it), next(it))
    t = FLASH_TILE
    n_heads = 2 * n_pairs
    rows = n_heads * t
    seq = k_ref.shape[0]
    i = pl.program_id(2)
    lane = lax.broadcasted_iota(jnp.int32, (t, LANES), 1)
    low = lane < HEAD_DIM

    @pl.when(i == 0)
    def _():
        for g in range(n_groups):
            for r0 in range(0, seq, t):
                vb = v_ref[r0:r0 + t, g * LANES:(g + 1) * LANES].astype(F32)
                vt_ref[g, :, r0:r0 + t] = vb.T.astype(BF16)

    for g in range(n_groups):
        qs = []
        for p in range(n_pairs):
            qp = q_ref[:, (g * n_pairs + p) * LANES:(g * n_pairs + p + 1) * LANES]
            qs += [jnp.where(low, qp, jnp.zeros_like(qp)), jnp.where(low, jnp.zeros_like(qp), qp)]
        qs_ref[g] = jnp.concatenate(qs, axis=0)
    m_ref[...] = jnp.full(m_ref.shape, NEG_INF, F32)
    l_ref[...] = jnp.zeros(l_ref.shape, F32)
    acc_ref[...] = jnp.zeros(acc_ref.shape, F32)

    def logits_into(s_ref, jt):
        jc = jnp.minimum(jt, i)
        kind = jnp.where(jt > i, 3, jnp.minimum(i - jt, 2))
        r0 = pl.multiple_of(jc * t, t)
        for g in range(n_groups):
            kb = k_ref[pl.ds(r0, t), g * LANES:(g + 1) * LANES]
            s = _dot_nt(kb, qs_ref[g]) + bias_ref[kind, :, g * rows:(g + 1) * rows]
            if has_sel:
                s = s + jnp.concatenate([sel_ref[jc].astype(F32)] * n_heads, axis=1)
            s_ref[g] = s

    def attend(p_ref, al_ref, jt):
        r0 = pl.multiple_of(jnp.clip(jt, 0, i) * t, t)
        for g in range(n_groups):
            pv = jnp.dot(vt_ref[g, :, pl.ds(r0, t)], p_ref[g], preferred_element_type=F32)
            acc_ref[g] = al_ref[g] * acc_ref[g] + pv

    def softmax_from(s_ref, p_ref, al_ref):
        for g in range(n_groups):
            s = s_ref[g]
            m_old = m_ref[g]
            m_new = jnp.maximum(m_old, jnp.max(s, axis=0, keepdims=True))
            alpha = jnp.exp2(m_old - m_new)
            pr = jnp.exp2(s - m_new)
            l_ref[g] = alpha * l_ref[g] + jnp.sum(pr, axis=0, keepdims=True)
            m_ref[g] = m_new
            p_ref[g] = pr.astype(BF16)
            al_ref[g] = alpha

    logits_into(sa_ref, 0)
    pb_ref[...] = jnp.zeros(pb_ref.shape, BF16)
    alb_ref[...] = jnp.ones(alb_ref.shape, F32)

    def two_tiles(u, carry):
        k0 = 2 * u
        logits_into(sb_ref, k0 + 1)
        attend(pb_ref, alb_ref, k0 - 1)
        softmax_from(sa_ref, pa_ref, ala_ref)
        logits_into(sa_ref, k0 + 2)
        attend(pa_ref, ala_ref, k0)
        softmax_from(sb_ref, pb_ref, alb_ref)
        return carry

    n_trips = lax.shift_right_logical(i + 2, 1)
    lax.fori_loop(0, n_trips, two_tiles, 0)
    attend(pb_ref, alb_ref, 2 * n_trips - 1)

    if mode == "diff":
        lv = lam_ref[...]
        lam = (jnp.exp(jnp.sum(lv[0:1] * lv[1:2], axis=-1, keepdims=True))
               - jnp.exp(jnp.sum(lv[2:3] * lv[3:4], axis=-1, keepdims=True)) + lambda_init)
    drow_low = lax.broadcasted_iota(jnp.int32, (LANES, t), 0) < HEAD_DIM
    for g in range(n_groups):
        res = acc_ref[g] / l_ref[g]
        for p in range(n_pairs):
            lo = res[:, 2 * p * t:(2 * p + 1) * t]
            hi = res[:, (2 * p + 1) * t:(2 * p + 2) * t]
            if mode == "diff":
                o = _rms((lo - lam * hi).T, subln_ref[...]) * (1.0 - lambda_init)
            else:
                o = jnp.where(drow_low, lo, hi).T
            o_ref[:, (g * n_pairs + p) * LANES:(g * n_pairs + p + 1) * LANES] = o.astype(BF16)


def _flash_attention(q_arr, q_blk0, k_arr, k_blk0, v_arr, v_blk0, bias, *, batch, seq, n_kv_pairs,
                     n_groups, n_pairs, mode="plain", sel=None, lam_vecs=None, subln=None,
                     lambda_init=0.0):
    t = FLASH_TILE
    nq = seq // t
    qw = n_groups * n_pairs * LANES
    kw = n_groups * LANES
    rows = 2 * n_pairs * t
    n_steps = n_kv_pairs // n_groups
    cfg = (n_groups, n_pairs, mode, lambda_init, sel is not None)
    in_specs = [
        pl.BlockSpec((t, qw), lambda b, j, i: (b * nq + i, q_blk0 + j)),
        pl.BlockSpec((seq, kw), lambda b, j, i: (b, k_blk0 + j)),
        pl.BlockSpec((seq, kw), lambda b, j, i: (b, v_blk0 + j)),
        pl.BlockSpec((4, t, n_groups * rows), lambda b, j, i: (0, 0, j)),
    ]
    args = [q_arr, k_arr, v_arr, bias]
    if sel is not None:
        in_specs.append(pl.BlockSpec((None, nq, t, t), lambda b, j, i: (b * nq + i, 0, 0, 0)))
        args.append(sel)
    if mode == "diff":
        in_specs.append(pl.BlockSpec((4, HEAD_DIM), lambda b, j, i: (0, 0)))
        in_specs.append(pl.BlockSpec((1, LANES), lambda b, j, i: (0, 0)))
        args += [lam_vecs, subln.reshape(1, LANES)]
    return pl.pallas_call(
        functools.partial(_flash_t_body, cfg),
        grid=(batch, n_steps, nq),
        in_specs=in_specs,
        out_specs=pl.BlockSpec((t, qw), lambda b, j, i: (b * nq + i, j)),
        out_shape=jax.ShapeDtypeStruct((batch * seq, n_steps * qw), BF16),
        scratch_shapes=[pltpu.VMEM((n_groups, rows, LANES), BF16),
                        pltpu.VMEM((n_groups, LANES, seq), BF16),
                        pltpu.VMEM((n_groups, 1, rows), F32),
                        pltpu.VMEM((n_groups, 1, rows), F32),
                        pltpu.VMEM((n_groups, LANES, rows), F32),
                        pltpu.VMEM((n_groups, t, rows), F32),
                        pltpu.VMEM((n_groups, t, rows), F32),
                        pltpu.VMEM((n_groups, t, rows), BF16),
                        pltpu.VMEM((n_groups, t, rows), BF16),
                        pltpu.VMEM((n_groups, 1, rows), F32),
                        pltpu.VMEM((n_groups, 1, rows), F32)],
        compiler_params=_params(3),
        name="flash_attention",
    )(*args)


def _select_t_body(cfg, qi_ref, wi_ref, ki_ref, o_ref, key_ref, qs_ref, cut_ref):
    n_keys, kpos0, qpos0, topk = cfg
    r = qi_ref.shape[0]
    t = FLASH_TILE
    n_tiles = n_keys // t
    qrow0 = qpos0 + pl.program_id(1) * r
    last_kpos = (lax.shift_right_arithmetic(qrow0 + r - 1, 6) + 1) * CHUNK - 1
    n_adm = jnp.minimum(n_tiles,
                        lax.shift_right_arithmetic(last_kpos - kpos0, t.bit_length() - 1) + 1)
    lane = lax.broadcasted_iota(jnp.int32, (r, LANES), 1)
    low = lane < C_IDX_DIM
    wit = wi_ref[...].T
    qchunk = lax.shift_right_arithmetic(qrow0 + lax.broadcasted_iota(jnp.int32, (t, r), 1), 6)
    krow = lax.broadcasted_iota(jnp.int32, (t, r), 0)

    def admissible(kb):
        kpos = kpos0 + kb * t + krow
        return jnp.logical_and(kpos >= 0, lax.shift_right_arithmetic(kpos, 6) <= qchunk)

    qs = []
    for h in range(C_IDX_HEADS):
        qp = qi_ref[:, (h // 2) * LANES:(h // 2 + 1) * LANES]
        zero = jnp.zeros_like(qp)
        qs.append(jnp.where(low, qp, zero) if h % 2 == 0 else jnp.where(low, zero, qp))
    qs_ref[...] = jnp.concatenate(qs, axis=0)

    def score_tile(kb, carry):
        r0 = pl.multiple_of(kb * t, t)
        dots = jnp.maximum(_dot_nt(ki_ref[pl.ds(r0, t), :], qs_ref[...]), 0.0)
        score = jnp.zeros((t, r), F32)
        for h in range(C_IDX_HEADS):
            score = score + wit[h:h + 1, :] * dots[:, h * r:(h + 1) * r]
        score = jnp.where(admissible(kb), score, NEG_INF)
        bits = pltpu.bitcast(score, jnp.int32)
        key_ref[kb] = jnp.where(bits >= 0, bits, bits ^ jnp.int32(0x7FFFFFFF))
        return carry

    lax.fori_loop(0, n_adm, score_tile, 0)

    def count(pred):
        def tile(kb, acc):
            hit = jnp.where(pred(key_ref[kb], kb), 1.0, 0.0)
            return acc + jnp.sum(hit.reshape(t // SUBLANES, SUBLANES, r), axis=0)
        acc = lax.fori_loop(0, n_adm, tile, jnp.zeros((SUBLANES, r), F32))
        return jnp.sum(acc, axis=0, keepdims=True)

    thr = jnp.full((1, r), jnp.int32(-2 ** 31), jnp.int32)
    cand0 = jnp.zeros((1, r), jnp.int32)
    thr = jnp.where(count(lambda k, kb: k >= cand0) >= topk, cand0, thr)

    def value_bit(it, thr):
        cand = thr | lax.shift_left(jnp.int32(1), 30 - it)
        return jnp.where(count(lambda k, kb: k >= cand) >= topk, cand, thr)

    thr = lax.fori_loop(0, 31, value_bit, thr)

    n_ge = count(lambda k, kb: k >= thr)
    cut_ref[...] = jnp.full((1, r), n_keys, jnp.int32)
    n_bits = max(1, (n_keys - 1).bit_length())

    @pl.when(jnp.max(n_ge) > topk)
    def _():
        ties_wanted = topk - count(lambda k, kb: k > thr)

        def index_bit(it, cut):
            cand = cut | lax.shift_left(jnp.int32(1), n_bits - 1 - it)
            before = count(lambda k, kb: jnp.logical_and(k == thr, kb * t + krow < cand))
            return jnp.where(before <= ties_wanted - 1.0, cand, cut)

        cut_ref[...] = lax.fori_loop(0, n_bits, index_bit, jnp.zeros((1, r), jnp.int32))

    cut = cut_ref[...]
    for kb in range(n_tiles):
        @pl.when(kb < n_adm)
        def _():
            kk = key_ref[kb]
            chosen = jnp.logical_or(kk > thr, jnp.logical_and(kk == thr, kb * t + krow <= cut))
            valid = jnp.logical_and(chosen, admissible(kb))
            o_ref[kb] = jnp.where(valid, 0.0, NEG_INF).astype(BF16)

        @pl.when(kb >= n_adm)
        def _():
            o_ref[kb] = jnp.full((t, r), NEG_INF, BF16)


def _select_mask(qi_arr, qi_blk, wi_arr, ki_arr, *, batch, q_rows, n_keys, kpos0, qpos0, topk):
    r = min(SELECT_ROWS, q_rows)
    t = FLASH_TILE
    nq = q_rows // r
    cfg = (n_keys, kpos0, qpos0, topk)
    return pl.pallas_call(
        functools.partial(_select_t_body, cfg),
        grid=(batch, nq),
        in_specs=[
            pl.BlockSpec((r, C_IDX_HEADS * C_IDX_DIM), lambda b, i: (b * nq + i, qi_blk)),
            pl.BlockSpec((r, LANES), lambda b, i: (b * nq + i, 0)),
            pl.BlockSpec((n_keys, LANES), lambda b, i: (b, 0)),
        ],
        out_specs=pl.BlockSpec((None, n_keys // t, t, r), lambda b, i: (b * nq + i, 0, 0, 0)),
        out_shape=jax.ShapeDtypeStruct((batch * nq, n_keys // t, t, r), BF16),
        scratch_shapes=[pltpu.VMEM((n_keys // t, t, r), jnp.int32),
                        pltpu.VMEM((C_IDX_HEADS * r, LANES), BF16),
                        pltpu.VMEM((1, r), jnp.int32)],
        compiler_params=_params(2),
        name="select_mask",
    )(qi_arr, wi_arr, ki_arr)


def _flash_body(cfg, *refs):
    n_groups, n_pairs, mode, lambda_init, has_sel = cfg
    it = iter(refs)
    q_ref, k_ref, v_ref, bias_ref = next(it), next(it), next(it), next(it)
    sel_ref = next(it) if has_sel else None
    if mode == "diff":
        lam_ref, subln_ref = next(it), next(it)
    o_ref = next(it)
    qs_ref, m_ref, l_ref, acc_ref = next(it), next(it), next(it), next(it)
    t = FLASH_TILE
    n_heads = 2 * n_pairs
    rows = n_heads * t
    i = pl.program_id(2)
    lane = lax.broadcasted_iota(jnp.int32, (t, LANES), 1)
    low = lane < HEAD_DIM
    row = lax.broadcasted_iota(jnp.int32, (rows, t), 0) & (t - 1)
    col = lax.broadcasted_iota(jnp.int32, (rows, t), 1)
    diag_ok = lax.shift_right_arithmetic(col, 6) <= lax.shift_right_arithmetic(row, 6)

    for g in range(n_groups):
        qs = []
        for p in range(n_pairs):
            qp = q_ref[:, (g * n_pairs + p) * LANES:(g * n_pairs + p + 1) * LANES]
            qs += [jnp.where(low, qp, jnp.zeros_like(qp)), jnp.where(low, jnp.zeros_like(qp), qp)]
        qs_ref[g] = jnp.concatenate(qs, axis=0)
    m_ref[...] = jnp.full(m_ref.shape, NEG_INF, F32)
    l_ref[...] = jnp.zeros(l_ref.shape, F32)
    acc_ref[...] = jnp.zeros(acc_ref.shape, F32)

    def kstep(j, carry):
        r0 = pl.multiple_of(j * t, t)
        tt = jnp.minimum(i - j, 2)
        on_diag = j >= i
        ss = []
        for g in range(n_groups):
            kb = k_ref[pl.ds(r0, t), g * LANES:(g + 1) * LANES]
            s = _dot_nt(qs_ref[g], kb) + bias_ref[tt, g * n_heads:(g + 1) * n_heads].reshape(rows, t)
            s = jnp.where(jnp.logical_or(jnp.logical_not(on_diag), diag_ok), s, NEG_INF)
            if has_sel:
                s = (s.reshape(n_heads, t, t) + sel_ref[j].astype(F32)[None]).reshape(rows, t)
            ss.append(s)
        prs = []
        for g in range(n_groups):
            m_old = m_ref[g]
            m_new = jnp.maximum(m_old, jnp.max(ss[g], axis=-1, keepdims=True))
            alpha = jnp.exp(m_old - m_new)
            pr = jnp.exp(ss[g] - m_new)
            l_ref[g] = alpha * l_ref[g] + jnp.sum(pr, axis=-1, keepdims=True)
            m_ref[g] = m_new
            prs.append((pr.astype(BF16), alpha))
        for g in range(n_groups):
            pr, alpha = prs[g]
            vb = v_ref[pl.ds(r0, t), g * LANES:(g + 1) * LANES]
            acc_ref[g] = alpha * acc_ref[g] + jnp.dot(pr, vb, preferred_element_type=F32)
        return carry

    lax.fori_loop(0, i + 1, kstep, 0)

    if mode == "diff":
        lv = lam_ref[...]
        lam = (jnp.exp(jnp.sum(lv[0:1] * lv[1:2], axis=-1, keepdims=True))
               - jnp.exp(jnp.sum(lv[2:3] * lv[3:4], axis=-1, keepdims=True)) + lambda_init)
    for g in range(n_groups):
        res = acc_ref[g] / l_ref[g]
        for p in range(n_pairs):
            lo = res[2 * p * t:(2 * p + 1) * t]
            hi = res[(2 * p + 1) * t:(2 * p + 2) * t]
            if mode == "diff":
                o = _rms(lo - lam * hi, subln_ref[...]) * (1.0 - lambda_init)
            else:
                o = jnp.where(low, lo, hi)
            o_ref[:, (g * n_pairs + p) * LANES:(g * n_pairs + p + 1) * LANES] = o.astype(BF16)


def _flash_attention_rowmajor(q_arr, q_blk0, k_arr, k_blk0, v_arr, v_blk0, bias, *, batch, seq,
                              n_kv_pairs, n_groups, n_pairs, mode="plain", sel=None,
                              lam_vecs=None, subln=None, lambda_init=0.0):
    t = FLASH_TILE
    nq = seq // t
    qw = n_groups * n_pairs * LANES
    kw = n_groups * LANES
    rows = 2 * n_pairs * t
    n_steps = n_kv_pairs // n_groups
    cfg = (n_groups, n_pairs, mode, lambda_init, sel is not None)
    in_specs = [
        pl.BlockSpec((t, qw), lambda b, j, i: (b * nq + i, q_blk0 + j)),
        pl.BlockSpec((seq, kw), lambda b, j, i: (b, k_blk0 + j)),
        pl.BlockSpec((seq, kw), lambda b, j, i: (b, v_blk0 + j)),
        pl.BlockSpec((3, 2 * n_pairs * n_groups, t, t), lambda b, j, i: (0, j, 0, 0)),
    ]
    args = [q_arr, k_arr, v_arr, bias]
    if sel is not None:
        in_specs.append(pl.BlockSpec((None, nq, t, t), lambda b, j, i: (b * nq + i, 0, 0, 0)))
        args.append(sel)
    if mode == "diff":
        in_specs.append(pl.BlockSpec((4, HEAD_DIM), lambda b, j, i: (0, 0)))
        in_specs.append(pl.BlockSpec((1, LANES), lambda b, j, i: (0, 0)))
        args += [lam_vecs, subln.reshape(1, LANES)]
    return pl.pallas_call(
        functools.partial(_flash_body, cfg),
        grid=(batch, n_steps, nq),
        in_specs=in_specs,
        out_specs=pl.BlockSpec((t, qw), lambda b, j, i: (b * nq + i, j)),
        out_shape=jax.ShapeDtypeStruct((batch * seq, n_steps * qw), BF16),
        scratch_shapes=[pltpu.VMEM((n_groups, rows, LANES), BF16),
                        pltpu.VMEM((n_groups, rows, 1), F32),
                        pltpu.VMEM((n_groups, rows, 1), F32),
                        pltpu.VMEM((n_groups, rows, LANES), F32)],
        compiler_params=_params(3),
        name="flash_attention",
    )(*args)


def _select_body(cfg, qi_ref, wi_ref, ki_ref, o_ref, key_ref, qs_ref, cut_ref):
    n_keys, kpos0, qpos0, topk = cfg
    r = qi_ref.shape[0]
    t = FLASH_TILE
    n_tiles = n_keys // t
    qrow0 = qpos0 + pl.program_id(1) * r
    last_kpos = (lax.shift_right_arithmetic(qrow0 + r - 1, 6) + 1) * CHUNK - 1
    n_adm = jnp.minimum(n_tiles,
                        lax.shift_right_arithmetic(last_kpos - kpos0, t.bit_length() - 1) + 1)
    lane = lax.broadcasted_iota(jnp.int32, (r, LANES), 1)
    low = lane < C_IDX_DIM
    wi = wi_ref[...]
    qchunk = lax.shift_right_arithmetic(qrow0 + lax.broadcasted_iota(jnp.int32, (r, t), 0), 6)
    col_t = lax.broadcasted_iota(jnp.int32, (r, t), 1)

    def admissible(kb):
        kpos = kpos0 + kb * t + col_t
        return jnp.logical_and(kpos >= 0, lax.shift_right_arithmetic(kpos, 6) <= qchunk)

    qs = []
    for h in range(C_IDX_HEADS):
        qp = qi_ref[:, (h // 2) * LANES:(h // 2 + 1) * LANES]
        zero = jnp.zeros_like(qp)
        qs.append(jnp.where(low, qp, zero) if h % 2 == 0 else jnp.where(low, zero, qp))
    qs_ref[...] = jnp.concatenate(qs, axis=0)

    def score_tile(kb, carry):
        r0 = pl.multiple_of(kb * t, t)
        dots = jnp.maximum(_dot_nt(qs_ref[...], ki_ref[pl.ds(r0, t), :]), 0.0)
        score = jnp.zeros((r, t), F32)
        for h in range(C_IDX_HEADS):
            score = score + wi[:, h:h + 1] * dots[h * r:(h + 1) * r]
        score = jnp.where(admissible(kb), score, NEG_INF)
        bits = pltpu.bitcast(score, jnp.int32)
        key_ref[kb] = jnp.where(bits >= 0, bits, bits ^ jnp.int32(0x7FFFFFFF))
        return carry

    lax.fori_loop(0, n_adm, score_tile, 0)

    def count(pred):
        def tile(kb, acc):
            hit = jnp.where(pred(key_ref[kb], kb), 1.0, 0.0)
            for c in range(0, t, LANES):
                acc = acc + hit[:, c:c + LANES]
            return acc
        acc = lax.fori_loop(0, n_adm, tile, jnp.zeros((r, LANES), F32))
        return jnp.sum(acc, axis=-1, keepdims=True)

    thr = jnp.full((r, 1), jnp.int32(-2 ** 31), jnp.int32)
    cand0 = jnp.zeros((r, 1), jnp.int32)
    thr = jnp.where(count(lambda k, kb: k >= cand0) >= topk, cand0, thr)

    def value_bit(it, thr):
        cand = thr | lax.shift_left(jnp.int32(1), 30 - it)
        return jnp.where(count(lambda k, kb: k >= cand) >= topk, cand, thr)

    thr = lax.fori_loop(0, 31, value_bit, thr)

    n_ge = count(lambda k, kb: k >= thr)
    cut_ref[...] = jnp.full((r, 1), n_keys, jnp.int32)
    n_bits = max(1, (n_keys - 1).bit_length())

    @pl.when(jnp.max(n_ge) > topk)
    def _():
        ties_wanted = topk - count(lambda k, kb: k > thr)

        def index_bit(it, cut):
            cand = cut | lax.shift_left(jnp.int32(1), n_bits - 1 - it)
            before = count(lambda k, kb: jnp.logical_and(k == thr, kb * t + col_t < cand))
            return jnp.where(before <= ties_wanted - 1.0, cand, cut)

        cut_ref[...] = lax.fori_loop(0, n_bits, index_bit, jnp.zeros((r, 1), jnp.int32))

    cut = cut_ref[...]
    for kb in range(n_tiles):
        @pl.when(kb < n_adm)
        def _():
            kk = key_ref[kb]
            chosen = jnp.logical_or(kk > thr, jnp.logical_and(kk == thr, kb * t + col_t <= cut))
            valid = jnp.logical_and(chosen, admissible(kb))
            o_ref[kb] = jnp.where(valid, 0.0, NEG_INF).astype(BF16)

        @pl.when(kb >= n_adm)
        def _():
            o_ref[kb] = jnp.full((r, t), NEG_INF, BF16)


def _select_mask_rowmajor(qi_arr, qi_blk, wi_arr, ki_arr, *, batch, q_rows, n_keys, kpos0, qpos0,
                          topk):
    r = min(SELECT_ROWS, q_rows)
    t = FLASH_TILE
    nq = q_rows // r
    cfg = (n_keys, kpos0, qpos0, topk)
    return pl.pallas_call(
        functools.partial(_select_body, cfg),
        grid=(batch, nq),
        in_specs=[
            pl.BlockSpec((r, C_IDX_HEADS * C_IDX_DIM), lambda b, i: (b * nq + i, qi_blk)),
            pl.BlockSpec((r, LANES), lambda b, i: (b * nq + i, 0)),
            pl.BlockSpec((n_keys, LANES), lambda b, i: (b, 0)),
        ],
        out_specs=pl.BlockSpec((None, n_keys // t, r, t), lambda b, i: (b * nq + i, 0, 0, 0)),
        out_shape=jax.ShapeDtypeStruct((batch * nq, n_keys // t, r, t), BF16),
        scratch_shapes=[pltpu.VMEM((n_keys // t, r, t), jnp.int32),
                        pltpu.VMEM((C_IDX_HEADS * r, LANES), BF16),
                        pltpu.VMEM((r, 1), jnp.int32)],
        compiler_params=_params(2),
        name="select_mask",
    )(qi_arr, wi_arr, ki_arr)


FFN_PAD = 16


def _block_ffn_body(cfg, *refs):
    seq_tiles, seq_rows, stream, final_norm = cfg
    it = iter(refs)
    x_ref, att_ref = next(it), next(it)
    if stream:
        xprev_ref, attprev_ref = next(it), next(it)
    else:
        fix1_ref, fix2_ref = next(it), next(it)
    wo_ref, g_ref, win_ref, cw_ref, cb_ref, wout_ref = (next(it), next(it), next(it), next(it),
                                                        next(it), next(it))
    gfin_ref = next(it) if final_norm else None
    o_ref, st_ref = next(it), next(it)
    hext_ref, act_ref = next(it), next(it)
    tm = x_ref.shape[0]
    pad = FFN_PAD
    gain = g_ref[...]
    if stream:
        aext_ref = next(it)
        aext_ref[0:pad, :] = attprev_ref[...]
        aext_ref[pad:, :] = att_ref[...]
        xext = jnp.concatenate([xprev_ref[...], x_ref[...]], axis=0)
        x1 = xext + jnp.dot(aext_ref[...], wo_ref[...], preferred_element_type=F32)
        o_ref[...] = x1[pad:, :]
        not_start = (pl.program_id(0) % seq_tiles) != 0
        rowi = lax.broadcasted_iota(jnp.int32, (tm + pad, 1), 0)
        keep = jnp.logical_or(rowi >= pad, not_start)
        hext_ref[...] = jnp.where(keep, _rms(x1, gain), 0.0).astype(BF16)
    else:
        x1 = x_ref[...] + jnp.dot(att_ref[...], wo_ref[...], preferred_element_type=F32)
        o_ref[...] = x1
        hext_ref[pad:, :] = _rms(x1, gain).astype(BF16)
        hext_ref[0:pad, :] = jnp.zeros((pad, D_MODEL), BF16)
        assert seq_rows & (seq_rows - 1) == 0
        rmod = lax.broadcasted_iota(jnp.int32, (tm, FF_CHUNK), 0) & (seq_rows - 1)
    hext = hext_ref[...]
    for c0 in range(0, D_FF, FF_CHUNK):
        ag = jnp.dot(hext, win_ref[:, 2 * c0:2 * c0 + 2 * FF_CHUNK], preferred_element_type=F32)
        a_ext = ag[:, :FF_CHUNK]
        gate = ag[pad:, FF_CHUNK:]
        a = a_ext[pad:, :]
        a1 = pltpu.roll(a_ext, 1, 0)[pad:, :]
        a2 = pltpu.roll(a_ext, 2, 0)[pad:, :]
        if not stream:
            a1 = jnp.where(rmod == 0, fix1_ref[:, c0:c0 + FF_CHUNK], a1)
            a2 = jnp.where(rmod <= 1, fix2_ref[:, c0:c0 + FF_CHUNK], a2)
        cw = cw_ref[:, c0:c0 + FF_CHUNK]
        u = cb_ref[:, c0:c0 + FF_CHUNK] + (cw[0:1] * a2 + cw[1:2] * a1 + cw[2:3] * a)
        act_ref[:, c0:c0 + FF_CHUNK] = (jax.nn.silu(u) * gate).astype(BF16)
        st_ref[:, c0:c0 + FF_CHUNK] = a[tm - SUBLANES:, :] if stream else a
    y = o_ref[...] + jnp.dot(act_ref[...], wout_ref[...], preferred_element_type=F32)
    if final_norm:
        y = _rms(y, gfin_ref[...])
    o_ref[...] = y


def _block_ffn(x, attn, w_o, gain, w_in, conv_w, conv_b, w_out, *, tm, seq_tiles, seq_rows,
               fixes=None, final_gain=None):
    n = x.shape[0]
    stream = fixes is None
    final_norm = final_gain is not None
    cfg = (seq_tiles, seq_rows, stream, final_norm)
    in_specs = [pl.BlockSpec((tm, D_MODEL), lambda i: (i, 0)),
                pl.BlockSpec((tm, ATTN_WIDTH), lambda i: (i, 0))]
    args = [x, attn]
    if stream:
        per = tm // FFN_PAD

        def prev(i):
            return (jnp.maximum(i * per - 1, 0), 0)
        in_specs += [pl.BlockSpec((FFN_PAD, D_MODEL), prev), pl.BlockSpec((FFN_PAD, ATTN_WIDTH), prev)]
        args += [x, attn]
    else:
        in_specs += [pl.BlockSpec((tm, D_FF), lambda i: (i, 0))] * 2
        args += list(fixes)
    in_specs += [
        _resident((ATTN_WIDTH, D_MODEL), lambda i: (0, 0)),
        _resident((1, D_MODEL), lambda i: (0, 0)),
        _resident((D_MODEL, 2 * D_FF), lambda i: (0, 0)),
        _resident((CONV_W, D_FF), lambda i: (0, 0)),
        _resident((1, D_FF), lambda i: (0, 0)),
        _resident((D_FF, D_MODEL), lambda i: (0, 0)),
    ]
    args += [w_o, gain.reshape(1, D_MODEL), w_in, conv_w, conv_b.reshape(1, D_FF), w_out]
    if final_norm:
        in_specs.append(_resident((1, D_MODEL), lambda i: (0, 0)))
        args.append(final_gain.reshape(1, D_MODEL))
    if stream:
        st_shape = jax.ShapeDtypeStruct((n // (tm * seq_tiles) * SUBLANES, D_FF), F32)
        st_spec = pl.BlockSpec((SUBLANES, D_FF), lambda i: (i // seq_tiles, 0))
    else:
        st_shape = jax.ShapeDtypeStruct((n, D_FF), F32)
        st_spec = pl.BlockSpec((tm, D_FF), lambda i: (i, 0))
    return pl.pallas_call(
        functools.partial(_block_ffn_body, cfg),
        grid=(n // tm,),
        in_specs=in_specs,
        out_specs=[pl.BlockSpec((tm, D_MODEL), lambda i: (i, 0)), st_spec],
        out_shape=[jax.ShapeDtypeStruct((n, D_MODEL), F32), st_shape],
        scratch_shapes=[pltpu.VMEM((tm + FFN_PAD, D_MODEL), BF16), pltpu.VMEM((tm, D_FF), BF16)]
        + ([pltpu.VMEM((tm + FFN_PAD, ATTN_WIDTH), BF16)] if stream else []),
        compiler_params=_params(1),
        name="block_ffn",
    )(*args)


def _oproj_body(x_ref, a_ref, w_ref, o_ref):
    o_ref[...] = x_ref[...] + jnp.dot(a_ref[...], w_ref[...], preferred_element_type=F32)


def _oproj_residual(x, attn, w_o, tm):
    n = x.shape[0]
    return pl.pallas_call(
        _oproj_body,
        grid=(n // tm,),
        in_specs=[
            pl.BlockSpec((tm, D_MODEL), lambda i: (i, 0)),
            pl.BlockSpec((tm, ATTN_WIDTH), lambda i: (i, 0)),
            _resident((ATTN_WIDTH, D_MODEL), lambda i: (0, 0)),
        ],
        out_specs=pl.BlockSpec((tm, D_MODEL), lambda i: (i, 0)),
        out_shape=jax.ShapeDtypeStruct((n, D_MODEL), F32),
        compiler_params=_params(1),
        name="oproj_residual",
    )(x, attn, w_o)


def _ffn_body(cfg, *refs):
    seq_tiles, seq_rows, stream, final_norm = cfg
    it = iter(refs)
    x_ref = next(it)
    if stream:
        xprev_ref = next(it)
    else:
        fix1_ref, fix2_ref = next(it), next(it)
    g_ref, win_ref, cw_ref, cb_ref, wout_ref = next(it), next(it), next(it), next(it), next(it)
    gfin_ref = next(it) if final_norm else None
    o_ref, st_ref = next(it), next(it)
    hext_ref, act_ref = next(it), next(it)
    tm = x_ref.shape[0]
    pad = SUBLANES
    x = x_ref[...]
    gain = g_ref[...]
    hext_ref[pad:, :] = _rms(x, gain).astype(BF16)
    if stream:
        not_start = ((pl.program_id(0) % seq_tiles) != 0).astype(F32)
        hext_ref[0:pad, :] = (_rms(xprev_ref[...], gain) * not_start).astype(BF16)
    else:
        hext_ref[0:pad, :] = jnp.zeros((pad, D_MODEL), BF16)
    hext = hext_ref[...]
    h = hext[pad:, :]
    is_seq_end = (pl.program_id(0) % seq_tiles) == seq_tiles - 1
    if not stream:
        assert seq_rows & (seq_rows - 1) == 0
        rmod = lax.broadcasted_iota(jnp.int32, (tm, FF_CHUNK), 0) & (seq_rows - 1)
    for c0 in range(0, D_FF, FF_CHUNK):
        a_ext = jnp.dot(hext, win_ref[:, c0:c0 + FF_CHUNK], preferred_element_type=F32)
        gate = jnp.dot(h, win_ref[:, D_FF + c0:D_FF + c0 + FF_CHUNK], preferred_element_type=F32)
        a = a_ext[pad:, :]
        a1 = pltpu.roll(a_ext, 1, 0)[pad:, :]
        a2 = pltpu.roll(a_ext, 2, 0)[pad:, :]
        if not stream:
            a1 = jnp.where(rmod == 0, fix1_ref[:, c0:c0 + FF_CHUNK], a1)
            a2 = jnp.where(rmod <= 1, fix2_ref[:, c0:c0 + FF_CHUNK], a2)
        cw = cw_ref[:, c0:c0 + FF_CHUNK]
        u = cb_ref[:, c0:c0 + FF_CHUNK] + (cw[0:1] * a2 + cw[1:2] * a1 + cw[2:3] * a)
        act_ref[:, c0:c0 + FF_CHUNK] = (jax.nn.silu(u) * gate).astype(BF16)
        if stream:
            @pl.when(is_seq_end)
            def _():
                st_ref[:, c0:c0 + FF_CHUNK] = a[tm - pad:, :]
        else:
            st_ref[:, c0:c0 + FF_CHUNK] = a
    y = x + jnp.dot(act_ref[...], wout_ref[...], preferred_element_type=F32)
    if final_norm:
        y = _rms(y, gfin_ref[...])
    o_ref[...] = y


def _conv_ffn(x, gain, w_in, conv_w, conv_b, w_out, *, tm, seq_tiles, seq_rows, fixes=None,
              final_gain=None):
    n = x.shape[0]
    stream = fixes is None
    final_norm = final_gain is not None
    cfg = (seq_tiles, seq_rows, stream, final_norm)
    in_specs = [pl.BlockSpec((tm, D_MODEL), lambda i: (i, 0))]
    args = [x]
    if stream:
        per = tm // SUBLANES
        in_specs.append(pl.BlockSpec((SUBLANES, D_MODEL), lambda i: (jnp.maximum(i * per - 1, 0), 0)))
        args.append(x)
    else:
        in_specs += [pl.BlockSpec((tm, D_FF), lambda i: (i, 0))] * 2
        args += list(fixes)
    in_specs += [
        _resident((1, D_MODEL), lambda i: (0, 0)),
        _resident((D_MODEL, 2 * D_FF), lambda i: (0, 0)),
        _resident((CONV_W, D_FF), lambda i: (0, 0)),
        _resident((1, D_FF), lambda i: (0, 0)),
        _resident((D_FF, D_MODEL), lambda i: (0, 0)),
    ]
    args += [gain.reshape(1, D_MODEL), w_in, conv_w, conv_b.reshape(1, D_FF), w_out]
    if final_norm:
        in_specs.append(_resident((1, D_MODEL), lambda i: (0, 0)))
        args.append(final_gain.reshape(1, D_MODEL))
    if stream:
        st_shape = jax.ShapeDtypeStruct((n // (tm * seq_tiles) * SUBLANES, D_FF), F32)
        st_spec = pl.BlockSpec((SUBLANES, D_FF), lambda i: (i // seq_tiles, 0))
    else:
        st_shape = jax.ShapeDtypeStruct((n, D_FF), F32)
        st_spec = pl.BlockSpec((tm, D_FF), lambda i: (i, 0))
    return pl.pallas_call(
        functools.partial(_ffn_body, cfg),
        grid=(n // tm,),
        in_specs=in_specs,
        out_specs=[pl.BlockSpec((tm, D_MODEL), lambda i: (i, 0)), st_spec],
        out_shape=[jax.ShapeDtypeStruct((n, D_MODEL), F32), st_shape],
        scratch_shapes=[pltpu.VMEM((tm + SUBLANES, D_MODEL), BF16), pltpu.VMEM((tm, D_FF), BF16)],
        compiler_params=_params(1),
        name="conv_ffn",
    )(*args)


def _t5_bucket(rel):
    half = NUM_BUCKETS // 2
    max_exact = half // 2
    base = jnp.where(rel > 0, half, 0)
    n = jnp.abs(rel)
    nf = jnp.maximum(n, 1).astype(F32)
    large = max_exact + (jnp.log(nf / max_exact) / math.log(T5_MAX_DISTANCE / max_exact)
                         * (half - max_exact)).astype(jnp.int32)
    large = jnp.minimum(large, half - 1)
    return base + jnp.where(n < max_exact, n, large)


def _lookup_heads(table, idx):
    onehot = (idx[..., None] == jnp.arange(table.shape[0])).astype(F32)
    return jnp.einsum("...n,nh->h...", onehot, table.astype(F32), precision=lax.Precision.HIGHEST)


def _t5_bias(table, rel):
    return _lookup_heads(table, _t5_bucket(rel))


def _band_rel(win, n_past_eff):
    i = jnp.arange(CHUNK)[:, None]
    j = jnp.arange(win)[None, :]
    return (j - n_past_eff) - i


def _flash_rel():
    t = FLASH_TILE
    r = jnp.arange(t)[:, None]
    c = jnp.arange(t)[None, :]
    return jnp.stack([c - r - d * t for d in range(3)])


def _flash_bias(bias):
    h, kinds, tq, tk = bias.shape
    ok = (jnp.arange(tk)[None, :] // CHUNK) <= (jnp.arange(tq)[:, None] // CHUNK)
    bias = bias.at[:, 0].set(jnp.where(ok[None], bias[:, 0], NEG_INF))
    bias = jnp.concatenate([bias, jnp.full((h, 1, tq, tk), NEG_INF, F32)], axis=1)
    return bias.transpose(1, 3, 0, 2).reshape(kinds + 1, tk, h * tq)


def _perm_cols(w, perm):
    return w.reshape(w.shape[0], len(perm), HEAD_DIM)[:, perm, :].reshape(w.shape[0], -1)


def _perm_rows(w, perm):
    return w.reshape(len(perm), HEAD_DIM, w.shape[1])[perm, :, :].reshape(-1, w.shape[1])


def _pad_keys(cache, new, pad, width):
    b = cache.shape[0]
    allk = jnp.concatenate([cache.reshape(b, -1, width), new.reshape(b, -1, width)], axis=1)
    padded = jnp.pad(allk, ((0, 0), (pad, 0), (0, 0))).astype(BF16)
    return allk, padded.reshape(-1, width)


def kernel(x_prompt, x_sample, cache_a_k, cache_a_v, cache_b_k, cache_b_v, cache_c_k, cache_c_v,
           cache_c_kidx, cache_d_k, cache_d_v, state_ffn_conv, t5_table, norm_mix, norm_ffn,
           norm_final, a_w_qkv, a_w_o, a_rel_bias, b_w_qkv, b_w_o, b_sinks, c_w_qkv, c_w_o,
           c_w_idx_q, c_w_idx_k, c_idx_k_norm, c_w_idx_w, d_w_qkv, d_w_o, d_lambda_q1,
           d_lambda_k1, d_lambda_q2, d_lambda_k2, d_subln, ffn_w_in, ffn_conv_w, ffn_conv_b,
           ffn_w_out):
    bp, seq, d = x_prompt.shape
    bs, ts, _ = x_sample.shape
    past = cache_c_k.shape[1]
    assert d == D_MODEL and ts == CHUNK and seq % ROW_TILE == 0 and past % CHUNK == 0
    n_p, n_s = bp * seq, bs * ts
    seq_tiles = seq // ROW_TILE
    depth = norm_mix.shape[0]
    scale = HEAD_DIM ** -0.5 * LOG2E
    t5_table = t5_table.astype(F32) * LOG2E
    perm = jnp.array(GQA_PERM)
    none_aux = jnp.zeros((1, LANES), F32)
    kvw = KV_HEADS * HEAD_DIM

    xp = x_prompt.reshape(n_p, d)
    xs = x_sample.reshape(n_s, d)

    def cast_dests(col_dests):
        secs = []
        for c0, c1, fn in col_dests:
            for c in range(c0, c1, PROJ_CHUNK):
                secs.append((c, min(PROJ_CHUNK, c1 - c), fn(c)))
        return secs

    layer = 0
    w = a_w_qkv.astype(BF16)
    aw = ATTN_WIDTH
    a_keep = min(A_PAST, seq)
    secs_p = cast_dests([
        (0, aw, lambda c: [("cast", 0, c, scale)]),
        (aw, 2 * aw, lambda c: [("cast", 0, c, None), ("tail", 1, c - aw, a_keep)]),
        (2 * aw, 3 * aw, lambda c: [("cast", 0, c, None), ("tail", 2, c - 2 * aw, a_keep)]),
    ])
    a_cache_out = (bp * a_keep, aw, F32, a_keep, "tail")
    qkv, a_k_p, a_v_p = _norm_proj(
        xp, norm_mix[layer], w, none_aux, secs_p,
        [(n_p, 3 * aw, BF16, ROW_TILE, "all"), a_cache_out, a_cache_out], seq_tiles, ROW_TILE)
    secs_s = cast_dests([
        (0, aw, lambda c: [("cast", 0, c, scale)]),
        (aw, 3 * aw, lambda c: [("cast", 1, c - aw, None)]),
    ])
    q_s, kv_s = _norm_proj(xs, norm_mix[layer], w, none_aux, secs_s,
                           [(n_s, aw, BF16, n_s, "all"), (n_s, 2 * aw, F32, n_s, "all")], 1, n_s)
    a_pad = CHUNK
    a_win = A_PAST + CHUNK + a_pad
    rel = _band_rel(a_win, A_PAST + a_pad)
    bias_a = _lookup_heads(a_rel_bias.astype(F32) * LOG2E,jnp.clip(rel, -A_CLIP, A_CLIP) + A_CLIP)
    bias_a = jnp.where((jnp.arange(a_win) >= a_pad)[None, None, :], bias_a, NEG_INF)
    att_p = _band_attention(qkv, 0, qkv, aw // LANES, qkv, 2 * aw // LANES, bias_a, batch=bp,
                            q_rows=seq, k_rows=seq, n_kv_pairs=N_HEADS // 2, n_pairs=1,
                            win=a_win, n_past=A_PAST + a_pad, chunks_in_flight=8)
    ks = kv_s[:, :aw].reshape(bs, ts, aw)
    vs = kv_s[:, aw:].reshape(bs, ts, aw)
    k_all, k_in = _pad_keys(cache_a_k, ks, a_pad, aw)
    v_all, v_in = _pad_keys(cache_a_v, vs, a_pad, aw)
    assert k_all.shape[1] + a_pad == a_win
    att_s = _band_attention(q_s, 0, k_in, 0, v_in, 0, bias_a, batch=bs, q_rows=ts, k_rows=a_win,
                            n_kv_pairs=N_HEADS // 2, n_pairs=1, win=a_win, n_past=0)
    a_keep_s = min(A_PAST, k_all.shape[1])
    a_k_prompt = a_k_p.reshape(bp, a_keep, N_HEADS, HEAD_DIM)
    a_v_prompt = a_v_p.reshape(bp, a_keep, N_HEADS, HEAD_DIM)
    a_k_sample = k_all[:, -a_keep_s:].reshape(bs, a_keep_s, N_HEADS, HEAD_DIM)
    a_v_sample = v_all[:, -a_keep_s:].reshape(bs, a_keep_s, N_HEADS, HEAD_DIM)
    w_o = a_w_o.astype(BF16)
    xp, xs, conv_p0, conv_s0 = _ffn_layer(xp, xs, att_p, att_s, w_o, layer, bp, bs, seq_tiles, ts, norm_ffn,
                                          ffn_w_in, ffn_conv_w, ffn_conv_b, ffn_w_out,
                                          state_ffn_conv, None)

    layer = 1
    w = jnp.concatenate([_perm_cols(b_w_qkv[:, :aw], perm), b_w_qkv[:, aw:]], axis=1).astype(BF16)
    b_keep = min(B_WINDOW, seq)
    secs_p = cast_dests([
        (0, aw, lambda c: [("cast", 0, c, scale)]),
        (aw, aw + kvw, lambda c: [("cast", 0, c, None), ("tail", 1, c - aw, b_keep)]),
        (aw + kvw, aw + 2 * kvw, lambda c: [("cast", 0, c, None), ("tail", 2, c - aw - kvw, b_keep)]),
    ])
    qkv, b_k_p, b_v_p = _norm_proj(
        xp, norm_mix[layer], w, none_aux, secs_p,
        [(n_p, aw + 2 * kvw, BF16, ROW_TILE, "all"), (bp * b_keep, kvw, F32, b_keep, "tail"),
         (bp * b_keep, kvw, F32, b_keep, "tail")], seq_tiles, ROW_TILE)
    secs_s = cast_dests([
        (0, aw, lambda c: [("cast", 0, c, scale)]),
        (aw, aw + 2 * kvw, lambda c: [("cast", 1, c - aw, None)]),
    ])
    q_s, kv_s = _norm_proj(xs, norm_mix[layer], w, none_aux, secs_s,
                           [(n_s, aw, BF16, n_s, "all"), (n_s, 2 * kvw, F32, n_s, "all")], 1, n_s)
    b_pad = CHUNK
    b_win = B_WINDOW + CHUNK + b_pad
    bias_b = _t5_bias(t5_table, _band_rel(b_win, B_WINDOW + b_pad))[perm]
    bias_b = jnp.where((jnp.arange(b_win) >= b_pad)[None, None, :], bias_b, NEG_INF)
    sinks = b_sinks.astype(F32)[perm] * LOG2E
    n_kvp = KV_HEADS // 2
    gq = N_HEADS // KV_HEADS
    att_p = _band_attention(qkv, 0, qkv, aw // LANES, qkv, (aw + kvw) // LANES, bias_b, batch=bp,
                            q_rows=seq, k_rows=seq, n_kv_pairs=n_kvp, n_pairs=gq, win=b_win,
                            n_past=B_WINDOW + b_pad, mode="sink", sinks=sinks)
    ks = kv_s[:, :kvw].reshape(bs, ts, kvw)
    vs = kv_s[:, kvw:].reshape(bs, ts, kvw)
    k_all, k_in = _pad_keys(cache_b_k, ks, b_pad, kvw)
    v_all, v_in = _pad_keys(cache_b_v, vs, b_pad, kvw)
    assert k_all.shape[1] + b_pad == b_win
    att_s = _band_attention(q_s, 0, k_in, 0, v_in, 0, bias_b, batch=bs, q_rows=ts, k_rows=b_win,
                            n_kv_pairs=n_kvp, n_pairs=gq, win=b_win, n_past=0, mode="sink",
                            sinks=sinks)
    b_keep_s = min(B_WINDOW, k_all.shape[1])
    b_k_prompt = b_k_p.reshape(bp, b_keep, KV_HEADS, HEAD_DIM)
    b_v_prompt = b_v_p.reshape(bp, b_keep, KV_HEADS, HEAD_DIM)
    b_k_sample = k_all[:, -b_keep_s:].reshape(bs, b_keep_s, KV_HEADS, HEAD_DIM)
    b_v_sample = v_all[:, -b_keep_s:].reshape(bs, b_keep_s, KV_HEADS, HEAD_DIM)
    w_o = _perm_rows(b_w_o, perm).astype(BF16)
    xp, xs, conv_p1, conv_s1 = _ffn_layer(xp, xs, att_p, att_s, w_o, layer, bp, bs, seq_tiles, ts, norm_ffn,
                                          ffn_w_in, ffn_conv_w, ffn_conv_b, ffn_w_out,
                                          state_ffn_conv, None)

    layer = 2
    iw = C_IDX_HEADS * C_IDX_DIM
    w_idx_w = jnp.pad(c_w_idx_w, ((0, 0), (0, LANES - C_IDX_HEADS)))
    w = jnp.concatenate([_perm_cols(c_w_qkv[:, :aw], perm), c_w_qkv[:, aw:], c_w_idx_q,
                         c_w_idx_k, c_w_idx_k, w_idx_w], axis=1).astype(BF16)
    c_qkv_w = aw + 2 * kvw
    col_ki = c_qkv_w + iw
    col_wi = col_ki + LANES
    knorm = jnp.concatenate([c_idx_k_norm, c_idx_k_norm]).astype(F32).reshape(1, LANES)
    wi_scale = C_IDX_HEADS ** -0.5

    def c_sections(k_out, v_out, qkv_out, by_head):
        def cache(o, c0):
            if by_head:
                return lambda c: ("heads", o, (c - c0) // HEAD_DIM, HEAD_DIM, KV_HEADS, None)
            return lambda c: ("cast", o, c - c0, None)
        k_dest, v_dest = cache(k_out, aw), cache(v_out, aw + kvw)
        secs = cast_dests([
            (0, aw, lambda c: [("cast", qkv_out, c, scale)]),
            (aw, aw + kvw, lambda c: [("cast", qkv_out, c, None), k_dest(c)]),
            (aw + kvw, c_qkv_w, lambda c: [("cast", qkv_out, c, None), v_dest(c)]),
            (c_qkv_w, col_ki, lambda c: [("cast", qkv_out, c, C_IDX_DIM ** -0.5)]),
        ])
        secs.append((col_ki, LANES, [("kidx", 3, 4)]))
        secs.append((col_wi, LANES, [("cast", 5, 0, wi_scale)]))
        return secs

    def c_outs(n, tm, by_head):
        cache = ((n * KV_HEADS, HEAD_DIM, F32, tm * KV_HEADS, "all") if by_head
                 else (n, kvw, F32, tm, "all"))
        return [(n, c_qkv_w + iw, BF16, tm, "all"), cache, cache, (n, LANES, BF16, tm, "all"),
                (n, C_IDX_DIM, F32, tm, "all"), (n, LANES, F32, tm, "all")]

    qkv, c_k_p, c_v_p, ki_p, kidx_p, wi_p = _norm_proj(
        xp, norm_mix[layer], w, knorm, c_sections(1, 2, 0, True), c_outs(n_p, ROW_TILE, True),
        seq_tiles, ROW_TILE)
    qkv_s, c_k_s, c_v_s, ki_s, kidx_s, wi_s = _norm_proj(
        xs, norm_mix[layer], w, knorm, c_sections(1, 2, 0, False), c_outs(n_s, n_s, False), 1, n_s)
    qi_blk = c_qkv_w // iw
    assert qi_blk * iw == c_qkv_w
    t = FLASH_TILE
    assert SELECT_ROWS == t
    sel_p = _select_mask(qkv, qi_blk, wi_p, ki_p, batch=bp, q_rows=seq, n_keys=seq, kpos0=0,
                         qpos0=0, topk=min(C_TOPK, seq // 4))
    bias_c = _flash_bias(_t5_bias(t5_table, _flash_rel())[perm])
    att_p = _flash_attention(qkv, 0, qkv, aw // LANES, qkv, (aw + kvw) // LANES, bias_c, batch=bp,
                             seq=seq, n_kv_pairs=n_kvp, n_groups=1, n_pairs=gq, sel=sel_p)
    n_keys_s = past + ts
    c_pad = (-n_keys_s) % t
    c_win = n_keys_s + c_pad
    k_all, k_in = _pad_keys(cache_c_k, c_k_s.reshape(bs, ts, kvw), c_pad, kvw)
    v_all, v_in = _pad_keys(cache_c_v, c_v_s.reshape(bs, ts, kvw), c_pad, kvw)
    del ki_s
    _, ki_in = _pad_keys(cache_c_kidx, kidx_s.reshape(bs, ts, C_IDX_DIM), c_pad, C_IDX_DIM)
    ki_in = jnp.concatenate([ki_in, ki_in], axis=1)
    qi_s = jnp.pad(qkv_s[:, c_qkv_w:].reshape(bs, ts, iw), ((0, 0), (0, LANES - ts), (0, 0)))
    wi_s = jnp.pad(wi_s.reshape(bs, ts, LANES), ((0, 0), (0, LANES - ts), (0, 0)))
    sel_s = _select_mask(qi_s.reshape(bs * LANES, iw), 0, wi_s.reshape(bs * LANES, LANES), ki_in,
                         batch=bs, q_rows=LANES, n_keys=c_win, kpos0=-c_pad, qpos0=past,
                         topk=min(C_TOPK, n_keys_s // 4))
    sel_s = sel_s[..., :ts].transpose(0, 3, 1, 2).reshape(bs, ts, c_win)
    rel_s = (jnp.arange(c_win)[None, :] - c_pad) - (past + jnp.arange(ts)[:, None])
    pad_ok = (jnp.arange(c_win) >= c_pad)[None, None, :]
    bias_s_t5 = jnp.where(pad_ok, _t5_bias(t5_table, rel_s), NEG_INF)
    att_s = _band_attention(qkv_s, 0, k_in, 0, v_in, 0, bias_s_t5[perm], batch=bs, q_rows=ts,
                            k_rows=c_win, n_kv_pairs=n_kvp, n_pairs=gq, win=c_win, n_past=0,
                            sel=sel_s)
    c_k_prompt = c_k_p.reshape(bp, seq, KV_HEADS, HEAD_DIM)
    c_v_prompt = c_v_p.reshape(bp, seq, KV_HEADS, HEAD_DIM)
    c_kidx_prompt = kidx_p.reshape(bp, seq, C_IDX_DIM)
    c_k_sample = c_k_s.reshape(bs, ts, KV_HEADS, HEAD_DIM)
    c_v_sample = c_v_s.reshape(bs, ts, KV_HEADS, HEAD_DIM)
    c_kidx_sample = kidx_s.reshape(bs, ts, C_IDX_DIM)
    w_o = _perm_rows(c_w_o, perm).astype(BF16)
    xp, xs, conv_p2, conv_s2 = _ffn_layer(xp, xs, att_p, att_s, w_o, layer, bp, bs, seq_tiles, ts, norm_ffn,
                                          ffn_w_in, ffn_conv_w, ffn_conv_b, ffn_w_out,
                                          state_ffn_conv, None)

    layer = 3
    lambda_init = 0.8 - 0.6 * math.exp(-0.3 * layer)
    w = d_w_qkv.astype(BF16)
    lam_vecs = jnp.stack([d_lambda_q1, d_lambda_k1, d_lambda_q2, d_lambda_k2]).astype(F32)
    secs = cast_dests([
        (0, aw, lambda c: [("cast", 0, c, scale)]),
        (aw, 2 * aw, lambda c: [("cast", 0, c, None), ("cast", 1, c - aw, None)]),
        (2 * aw, 3 * aw, lambda c: [("cast", 0, c, None), ("cast", 2, c - 2 * aw, None)]),
    ])

    def d_outs(n, tm):
        return [(n, 3 * aw, BF16, tm, "all"), (n, aw, F32, tm, "all"), (n, aw, F32, tm, "all")]

    secs_p = cast_dests([
        (0, aw, lambda c: [("cast", 0, c, scale)]),
        (aw, 2 * aw, lambda c: [("cast", 0, c, None),
                                ("heads", 1, (c - aw) // HEAD_DIM, HEAD_DIM, 2 * D_HEADS, None)]),
        (2 * aw, 3 * aw, lambda c: [("cast", 0, c, None),
                                    ("heads", 2, (c - 2 * aw) // LANES, LANES, D_HEADS, None)]),
    ])
    qkv, d_k_p, d_v_p = _norm_proj(
        xp, norm_mix[layer], w, none_aux, secs_p,
        [(n_p, 3 * aw, BF16, ROW_TILE, "all"),
         (n_p * 2 * D_HEADS, HEAD_DIM, F32, ROW_TILE * 2 * D_HEADS, "all"),
         (n_p * D_HEADS, LANES, F32, ROW_TILE * D_HEADS, "all")], seq_tiles, ROW_TILE)
    qkv_s, d_k_s, d_v_s = _norm_proj(xs, norm_mix[layer], w, none_aux, secs, d_outs(n_s, n_s), 1,
                                     n_s)
    bias_d = _flash_bias(_t5_bias(t5_table, _flash_rel()))
    d_grp = 4
    att_p = _flash_attention(qkv, 0, qkv, aw // (d_grp * LANES), qkv, 2 * aw // (d_grp * LANES),
                             bias_d, batch=bp, seq=seq, n_kv_pairs=D_HEADS, n_groups=d_grp,
                             n_pairs=1, mode="diff",
                             lam_vecs=lam_vecs, subln=d_subln.astype(F32), lambda_init=lambda_init)
    k_all, k_in = _pad_keys(cache_d_k, d_k_s.reshape(bs, ts, aw), c_pad, aw)
    v_all, v_in = _pad_keys(cache_d_v, d_v_s.reshape(bs, ts, aw), c_pad, aw)
    att_s = _band_attention(qkv_s, 0, k_in, 0, v_in, 0, bias_s_t5, batch=bs, q_rows=ts,
                            k_rows=c_win, n_kv_pairs=D_HEADS, n_pairs=1, win=c_win, n_past=0,
                            mode="diff", lam_vecs=lam_vecs, subln=d_subln.astype(F32),
                            lambda_init=lambda_init)
    d_k_prompt = d_k_p.reshape(bp, seq, 2 * D_HEADS, HEAD_DIM)
    d_v_prompt = d_v_p.reshape(bp, seq, D_HEADS, 2 * HEAD_DIM)
    d_k_sample = d_k_s.reshape(bs, ts, 2 * D_HEADS, HEAD_DIM)
    d_v_sample = d_v_s.reshape(bs, ts, D_HEADS, 2 * HEAD_DIM)
    w_o = d_w_o.astype(BF16)
    xp, xs, conv_p3, conv_s3 = _ffn_layer(xp, xs, att_p, att_s, w_o, layer, bp, bs, seq_tiles, ts, norm_ffn,
                                          ffn_w_in, ffn_conv_w, ffn_conv_b, ffn_w_out,
                                          state_ffn_conv, norm_final)
    assert depth == 4

    y_prompt = xp.reshape(bp, seq, d)
    y_sample = xs.reshape(bs, ts, d)
    ffn_conv_prompt = jnp.stack([conv_p0, conv_p1, conv_p2, conv_p3])
    ffn_conv_sample = jnp.stack([conv_s0, conv_s1, conv_s2, conv_s3])
    return (y_prompt, y_sample,
            a_k_prompt, a_v_prompt, a_k_sample, a_v_sample,
            b_k_prompt, b_v_prompt, b_k_sample, b_v_sample,
            c_k_prompt, c_v_prompt, c_kidx_prompt, c_k_sample, c_v_sample, c_kidx_sample,
            d_k_prompt, d_v_prompt, d_k_sample, d_v_sample,
            ffn_conv_prompt, ffn_conv_sample)


def _ffn_layer(xp, xs, att_p, att_s, w_o, layer, bp, bs, seq_tiles, ts, norm_ffn, ffn_w_in,
               ffn_conv_w, ffn_conv_b, ffn_w_out, state, final_gain):
    n_ch = D_FF // FF_CHUNK
    w_in = ffn_w_in[layer].astype(BF16)
    w_in = w_in.reshape(D_MODEL, 2, n_ch, FF_CHUNK).transpose(0, 2, 1, 3).reshape(D_MODEL, 2 * D_FF)
    w_out = ffn_w_out[layer].astype(BF16)
    cw, cb, gain = ffn_conv_w[layer], ffn_conv_b[layer], norm_ffn[layer]
    xp, tail = _block_ffn(xp, att_p, w_o, gain, w_in, cw, cb, w_out, tm=ROW_TILE,
                          seq_tiles=seq_tiles, seq_rows=ROW_TILE * seq_tiles,
                          final_gain=final_gain)
    conv_p = tail.reshape(bp, SUBLANES, D_FF)[:, SUBLANES - (CONV_W - 1):, :]
    n_s = bs * ts
    st = state[layer]
    zeros = jnp.zeros((bs, ts - 2, D_FF), F32)
    fix1 = jnp.concatenate([st[:, 1:2], jnp.zeros((bs, 1, D_FF), F32), zeros], axis=1)
    fix2 = jnp.concatenate([st, zeros], axis=1)
    xs, a_s = _block_ffn(xs, att_s, w_o, gain, w_in, cw, cb, w_out, tm=n_s, seq_tiles=1,
                         seq_rows=ts, fixes=(fix1.reshape(n_s, D_FF), fix2.reshape(n_s, D_FF)),
                         final_gain=final_gain)
    conv_s = a_s.reshape(bs, ts, D_FF)[:, ts - (CONV_W - 1):, :]
    return xp, xs, conv_p, conv_s
```

```python
import functools
import math

import jax
import jax.numpy as jnp
from jax import lax
from jax.experimental import pallas as pl
from jax.experimental.pallas import tpu as pltpu

F32 = jnp.float32
BF16 = jnp.bfloat16

D_MODEL = 1024
CHUNK = 64
N_HEADS = 16
HEAD_DIM = 64
ATTN_WIDTH = N_HEADS * HEAD_DIM
NUM_BUCKETS = 32
T5_MAX_DISTANCE = 128
A_PAST = 512
A_CLIP = 64
B_WINDOW = 128
KV_HEADS = 4
C_TOPK = 256
C_IDX_HEADS = 8
C_IDX_DIM = 64
D_HEADS = 8
D_FF = 2816
CONV_W = 3
RMS_EPS = 1e-6
NEG_INF = -1e30
LOG2E = math.log2(math.e)

LANES = 128
SUBLANES = 8
ROW_TILE = 512
FLASH_TILE = 256
BAND_UNROLL = 4
SELECT_ROWS = FLASH_TILE
FF_CHUNK = 256
PROJ_CHUNK = 512
VMEM_LIMIT = 56 * 1024 * 1024

GQA_PERM = tuple(8 * j + 4 * half + t for j in range(2) for t in range(4) for half in range(2))


def _params(n_grid_dims):
    return pltpu.CompilerParams(
        dimension_semantics=("arbitrary",) * n_grid_dims, vmem_limit_bytes=VMEM_LIMIT)


def _resident(shape, index_map):
    return pl.BlockSpec(shape, index_map, pipeline_mode=pl.Buffered(1))


def _rms(x, gain):
    ms = jnp.mean(x * x, axis=-1, keepdims=True)
    return (x * lax.rsqrt(ms + RMS_EPS)) * gain


def _dot_nt(a, b):
    return lax.dot_general(a, b, (((1,), (1,)), ((), ())), preferred_element_type=F32)


def _norm_proj_body(sections, seq_tiles, n_out, x_ref, g_ref, w_ref, aux_ref, *out_refs):
    tm = x_ref.shape[0]
    h = _rms(x_ref[...], g_ref[...]).astype(BF16)
    for col0, width, dests in sections:
        y = jnp.dot(h, w_ref[:, col0:col0 + width], preferred_element_type=F32)
        for dest in dests:
            kind = dest[0]
            if kind == "cast":
                _, o, oc, scale = dest
                val = y if scale is None else y * scale
                out_refs[o][:, oc:oc + width] = val.astype(out_refs[o].dtype)
            elif kind == "tail":
                _, o, oc, nrows = dest
                out_refs[o][:, oc:oc + width] = y[tm - nrows:, :]
            elif kind == "xpose":
                _, o, orow = dest
                out_refs[o][orow:orow + width, :] = y.T.astype(BF16)
            elif kind == "heads":
                _, o, head0, dh, nh, nrows = dest
                src = y if nrows is None else y[tm - nrows:, :]
                for hh in range(width // dh):
                    out_refs[o][pl.ds(head0 + hh, src.shape[0], stride=nh), :] = (
                        src[:, hh * dh:(hh + 1) * dh])
            elif kind == "kidx":
                _, o_dup, o_f32 = dest
                yn = _rms(y, aux_ref[...])
                out_refs[o_dup][...] = yn.astype(BF16)
                out_refs[o_f32][...] = yn[:, :C_IDX_DIM]
            else:
                raise ValueError(kind)


def _norm_proj(x, gain, w, aux, sections, outs, seq_tiles, tm):
    n = x.shape[0]
    grid = (n // tm,)
    out_shape, out_specs = [], []
    for rows, width, dtype, blk_rows, mode in outs:
        out_shape.append(jax.ShapeDtypeStruct((rows, width), dtype))
        if mode == "all":
            out_specs.append(pl.BlockSpec((blk_rows, width), lambda i: (i, 0)))
        elif mode == "cols":
            out_specs.append(pl.BlockSpec((rows, tm), lambda i: (0, i)))
        else:
            out_specs.append(pl.BlockSpec((blk_rows, width), lambda i: (i // seq_tiles, 0)))
    body = functools.partial(_norm_proj_body, sections, seq_tiles, len(outs))
    return pl.pallas_call(
        body,
        grid=grid,
        in_specs=[
            pl.BlockSpec((tm, D_MODEL), lambda i: (i, 0)),
            _resident((1, D_MODEL), lambda i: (0, 0)),
            _resident(w.shape, lambda i: (0, 0)),
            _resident(aux.shape, lambda i: (0, 0)),
        ],
        out_specs=out_specs,
        out_shape=out_shape,
        compiler_params=_params(1),
        name="norm_proj",
    )(x, gain.reshape(1, D_MODEL), w, aux)


def _split_sections(col0, total, width, dests_fn):
    return [(col0 + c, width, dests_fn(c)) for c in range(0, total, width)]


def _softmax_rows(s, sink, want_sum):
    m = jnp.max(s, axis=-1, keepdims=True)
    if sink is not None:
        m = jnp.maximum(m, sink)
    p = jnp.exp2(s - m)
    l = jnp.exp2(sink - m) if sink is not None else None
    if want_sum:
        total = jnp.sum(p, axis=-1, keepdims=True)
        l = total if l is None else total + l
    return p, l


def _band_body(cfg, *refs):
    n_pairs, win, n_past, n_chunks, unroll, mode, lambda_init, has_sel = cfg
    it = iter(refs)
    q_ref, k_ref, v_ref, bias_ref = next(it), next(it), next(it), next(it)
    sel_ref = next(it) if has_sel else None
    sink_ref = next(it) if mode == "sink" else None
    if mode == "diff":
        lam_ref, subln_ref = next(it), next(it)
    o_ref = next(it)
    if n_past:
        kp_ref, vp_ref = next(it), next(it)
        kp_ref[0:n_past, :] = jnp.zeros((n_past, LANES), BF16)
        vp_ref[0:n_past, :] = jnp.zeros((n_past, LANES), BF16)
        kp_ref[n_past:, :] = k_ref[...]
        vp_ref[n_past:, :] = v_ref[...]
    else:
        kp_ref, vp_ref = k_ref, v_ref

    n_heads = 2 * n_pairs
    rows = n_heads * CHUNK
    lane = lax.broadcasted_iota(jnp.int32, (CHUNK, LANES), 1)
    col = lax.broadcasted_iota(jnp.int32, (rows, win), 1)
    low = lane < HEAD_DIM
    if mode == "diff":
        lv = lam_ref[...]
        lam = (jnp.exp(jnp.sum(lv[0:1] * lv[1:2], axis=-1, keepdims=True))
               - jnp.exp(jnp.sum(lv[2:3] * lv[3:4], axis=-1, keepdims=True)) + lambda_init)

    def logits(c):
        r0 = pl.multiple_of(c * CHUNK, CHUNK)
        kwin = kp_ref[pl.ds(r0, win), :]
        qs = []
        for p in range(n_pairs):
            qp = q_ref[pl.ds(r0, CHUNK), p * LANES:(p + 1) * LANES]
            qs += [jnp.where(low, qp, jnp.zeros_like(qp)), jnp.where(low, jnp.zeros_like(qp), qp)]
        s = _dot_nt(jnp.concatenate(qs, axis=0), kwin) + bias_ref[...].reshape(rows, win)
        if n_past:
            s = jnp.where(col >= n_past - c * CHUNK, s, NEG_INF)
        if has_sel:
            s = (s.reshape(n_heads, CHUNK, win) + sel_ref[...].astype(F32)[None]).reshape(rows, win)
        return s

    def attend(c, pr, l):
        r0 = pl.multiple_of(c * CHUNK, CHUNK)
        vwin = vp_ref[pl.ds(r0, win), :]
        if mode == "diff":
            pn = pr / l
            a = (pn[0:CHUNK] - lam * pn[CHUNK:2 * CHUNK]).astype(BF16)
            o = jnp.dot(a, vwin, preferred_element_type=F32)
            o_ref[pl.ds(r0, CHUNK), :] = (_rms(o, subln_ref[...]) * (1.0 - lambda_init)).astype(BF16)
            return
        v_ones = jnp.concatenate([vwin, jnp.ones((win, LANES), BF16)], axis=1)
        ol = jnp.dot(pr.astype(BF16), v_ones, preferred_element_type=F32)
        denom = ol[:, LANES:] if l is None else ol[:, LANES:] + l
        o = ol[:, :LANES] / denom
        for p in range(n_pairs):
            lo = o[2 * p * CHUNK:(2 * p + 1) * CHUNK]
            hi = o[(2 * p + 1) * CHUNK:(2 * p + 2) * CHUNK]
            o_ref[pl.ds(r0, CHUNK), p * LANES:(p + 1) * LANES] = jnp.where(low, lo, hi).astype(BF16)

    sink = sink_ref[...] if mode == "sink" else None

    def chunks(cc, carry):
        ids = [cc * unroll + u for u in range(unroll)]
        ss = [logits(c) for c in ids]
        pls = [_softmax_rows(s, sink, mode == "diff") for s in ss]
        for c, (pr, l) in zip(ids, pls):
            attend(c, pr, l)
        return carry

    if n_chunks == unroll:
        chunks(0, 0)
    else:
        lax.fori_loop(0, n_chunks // unroll, chunks, 0)


def _band_attention(q_arr, q_blk0, k_arr, k_blk0, v_arr, v_blk0, bias, *, batch, q_rows, k_rows,
                    n_kv_pairs, n_pairs, win, n_past, mode="plain", sel=None, sinks=None,
                    lam_vecs=None, subln=None, lambda_init=0.0, chunks_in_flight=BAND_UNROLL):
    n_chunks = q_rows // CHUNK
    qw = n_pairs * LANES
    rows = 2 * n_pairs * CHUNK
    unroll = math.gcd(n_chunks, chunks_in_flight)
    cfg = (n_pairs, win, n_past, n_chunks, unroll, mode, lambda_init, sel is not None)
    in_specs = [
        pl.BlockSpec((q_rows, qw), lambda b, j: (b, q_blk0 + j)),
        pl.BlockSpec((k_rows, LANES), lambda b, j: (b, k_blk0 + j)),
        pl.BlockSpec((k_rows, LANES), lambda b, j: (b, v_blk0 + j)),
        pl.BlockSpec((2 * n_pairs, CHUNK, win), lambda b, j: (j, 0, 0)),
    ]
    args = [q_arr, k_arr, v_arr, bias]
    if sel is not None:
        in_specs.append(pl.BlockSpec((None, CHUNK, win), lambda b, j: (b, 0, 0)))
        args.append(sel)
    if mode == "sink":
        in_specs.append(pl.BlockSpec((rows, 1), lambda b, j: (j, 0)))
        args.append(jnp.repeat(sinks, CHUNK).reshape(n_kv_pairs * rows, 1))
    if mode == "diff":
        in_specs.append(pl.BlockSpec((4, HEAD_DIM), lambda b, j: (0, 0)))
        in_specs.append(pl.BlockSpec((1, LANES), lambda b, j: (0, 0)))
        args += [lam_vecs, subln.reshape(1, LANES)]
    scratch = []
    if n_past:
        scratch = [pltpu.VMEM((n_past + k_rows, LANES), BF16)] * 2
    return pl.pallas_call(
        functools.partial(_band_body, cfg),
        grid=(batch, n_kv_pairs),
        in_specs=in_specs,
        out_specs=pl.BlockSpec((q_rows, qw), lambda b, j: (b, j)),
        out_shape=jax.ShapeDtypeStruct((batch * q_rows, n_kv_pairs * qw), BF16),
        scratch_shapes=scratch,
        compiler_params=_params(2),
        name="band_attention",
    )(*args)


def _flash_t_body(cfg, *refs):
    n_groups, n_pairs, mode, lambda_init, has_sel = cfg
    it = iter(refs)
    q_ref, k_ref, vt_in_ref, bias_ref = next(it), next(it), next(it), next(it)
    sel_ref = next(it) if has_sel else None
    if mode == "diff":
        lam_ref, subln_ref = next(it), next(it)
    o_ref = next(it)
    qs_ref, m_ref, acc_ref = next(it), next(it), next(it)
    sa_ref, sb_ref, pa_ref, pb_ref, ala_ref, alb_ref = (next(it), next(it), next(it), next(it),
                                                        next(it), next(it))
    t = FLASH_TILE
    n_heads = 2 * n_pairs
    rows = n_heads * t
    i = pl.program_id(2)
    lane = lax.broadcasted_iota(jnp.int32, (t, LANES), 1)
    low = lane < HEAD_DIM

    for g in range(n_groups):
        qs = []
        for p in range(n_pairs):
            qp = q_ref[:, (g * n_pairs + p) * LANES:(g * n_pairs + p + 1) * LANES]
            qs += [jnp.where(low, qp, jnp.zeros_like(qp)), jnp.where(low, jnp.zeros_like(qp), qp)]
        qs_ref[g] = jnp.concatenate(qs, axis=0)
    m_ref[...] = jnp.full(m_ref.shape, NEG_INF, F32)
    acc_ref[...] = jnp.zeros(acc_ref.shape, F32)

    def logits_into(s_ref, jt):
        jc = jnp.minimum(jt, i)
        kind = jnp.where(jt > i, 3, jnp.minimum(i - jt, 2))
        r0 = pl.multiple_of(jc * t, t)
        for g in range(n_groups):
            kb = k_ref[pl.ds(r0, t), g * LANES:(g + 1) * LANES]
            s = _dot_nt(kb, qs_ref[g]) + bias_ref[kind, :, g * rows:(g + 1) * rows]
            if has_sel:
                s = s + jnp.concatenate([sel_ref[jc].astype(F32)] * n_heads, axis=1)
            s_ref[g] = s

    ones_rows = jnp.ones((FLASH_SUM_ROWS, t), BF16)

    def attend(p_ref, al_ref, jt):
        r0 = pl.multiple_of(jnp.clip(jt, 0, i) * t, t)
        for g in range(n_groups):
            vt = jnp.concatenate([vt_in_ref[g * LANES:(g + 1) * LANES, pl.ds(r0, t)], ones_rows],
                                 axis=0)
            pv = jnp.dot(vt, p_ref[g], preferred_element_type=F32)
            acc_ref[g] = al_ref[g] * acc_ref[g] + pv

    def softmax_from(s_ref, p_ref, al_ref):
        for g in range(n_groups):
            s = s_ref[g]
            m_old = m_ref[g]
            m_new = jnp.maximum(m_old, jnp.max(s, axis=0, keepdims=True))
            m_ref[g] = m_new
            p_ref[g] = jnp.exp2(s - m_new).astype(BF16)
            al_ref[g] = jnp.exp2(m_old - m_new)

    logits_into(sa_ref, 0)
    pb_ref[...] = jnp.zeros(pb_ref.shape, BF16)
    alb_ref[...] = jnp.ones(alb_ref.shape, F32)

    def two_tiles(u, carry):
        k0 = 2 * u
        logits_into(sb_ref, k0 + 1)
        attend(pb_ref, alb_ref, k0 - 1)
        softmax_from(sa_ref, pa_ref, ala_ref)
        logits_into(sa_ref, k0 + 2)
        attend(pa_ref, ala_ref, k0)
        softmax_from(sb_ref, pb_ref, alb_ref)
        return carry

    n_trips = lax.shift_right_logical(i + 2, 1)
    lax.fori_loop(0, n_trips, two_tiles, 0)
    attend(pb_ref, alb_ref, 2 * n_trips - 1)

    if mode == "diff":
        lv = lam_ref[...]
        lam = (jnp.exp(jnp.sum(lv[0:1] * lv[1:2], axis=-1, keepdims=True))
               - jnp.exp(jnp.sum(lv[2:3] * lv[3:4], axis=-1, keepdims=True)) + lambda_init)
    drow_low = lax.broadcasted_iota(jnp.int32, (LANES, t), 0) < HEAD_DIM
    for g in range(n_groups):
        acc = acc_ref[g]
        res = acc[:LANES] / acc[LANES:LANES + 1]
        for p in range(n_pairs):
            lo = res[:, 2 * p * t:(2 * p + 1) * t]
            hi = res[:, (2 * p + 1) * t:(2 * p + 2) * t]
            if mode == "diff":
                d = lo - lam * hi
                ms = jnp.mean(d * d, axis=0, keepdims=True)
                o = ((d * lax.rsqrt(ms + RMS_EPS)) * subln_ref[...] * (1.0 - lambda_init)).T
            else:
                o = jnp.where(drow_low, lo, hi).T
            o_ref[:, (g * n_pairs + p) * LANES:(g * n_pairs + p + 1) * LANES] = o.astype(BF16)


def _flash_attention(q_arr, q_blk0, k_arr, k_blk0, vt_arr, v_blk0, bias, *, batch, seq, n_kv_pairs,
                     n_groups, n_pairs, mode="plain", sel=None, lam_vecs=None, subln=None,
                     lambda_init=0.0):
    t = FLASH_TILE
    nq = seq // t
    qw = n_groups * n_pairs * LANES
    kw = n_groups * LANES
    rows = 2 * n_pairs * t
    n_steps = n_kv_pairs // n_groups
    cfg = (n_groups, n_pairs, mode, lambda_init, sel is not None)
    in_specs = [
        pl.BlockSpec((t, qw), lambda b, j, i: (b * nq + i, q_blk0 + j)),
        pl.BlockSpec((seq, kw), lambda b, j, i: (b, k_blk0 + j)),
        pl.BlockSpec((kw, seq), lambda b, j, i: (v_blk0 + j, b)),
        pl.BlockSpec((4, t, n_groups * rows), lambda b, j, i: (0, 0, j)),
    ]
    args = [q_arr, k_arr, vt_arr, bias]
    if sel is not None:
        in_specs.append(pl.BlockSpec((None, nq, t, t), lambda b, j, i: (b * nq + i, 0, 0, 0)))
        args.append(sel)
    if mode == "diff":
        in_specs.append(pl.BlockSpec((4, HEAD_DIM), lambda b, j, i: (0, 0)))
        in_specs.append(pl.BlockSpec((LANES, t), lambda b, j, i: (0, 0)))
        args += [lam_vecs, jnp.broadcast_to(subln.reshape(LANES, 1), (LANES, t))]
    return pl.pallas_call(
        functools.partial(_flash_t_body, cfg),
        grid=(batch, n_steps, nq),
        in_specs=in_specs,
        out_specs=pl.BlockSpec((t, qw), lambda b, j, i: (b * nq + i, j)),
        out_shape=jax.ShapeDtypeStruct((batch * seq, n_steps * qw), BF16),
        scratch_shapes=[pltpu.VMEM((n_groups, rows, LANES), BF16),
                        pltpu.VMEM((n_groups, 1, rows), F32),
                        pltpu.VMEM((n_groups, LANES + FLASH_SUM_ROWS, rows), F32),
                        pltpu.VMEM((n_groups, t, rows), F32),
                        pltpu.VMEM((n_groups, t, rows), F32),
                        pltpu.VMEM((n_groups, t, rows), BF16),
                        pltpu.VMEM((n_groups, t, rows), BF16),
                        pltpu.VMEM((n_groups, 1, rows), F32),
                        pltpu.VMEM((n_groups, 1, rows), F32)],
        compiler_params=_params(3),
        name="flash_attention",
    )(*args)


def _select_t_body(cfg, qi_ref, wi_ref, ki_ref, o_ref, key_ref, qs_ref, cut_ref):
    n_keys, kpos0, qpos0, topk = cfg
    r = qi_ref.shape[0]
    t = FLASH_TILE
    n_tiles = n_keys // t
    qrow0 = qpos0 + pl.program_id(1) * r
    last_kpos = (lax.shift_right_arithmetic(qrow0 + r - 1, 6) + 1) * CHUNK - 1
    n_adm = jnp.minimum(n_tiles,
                        lax.shift_right_arithmetic(last_kpos - kpos0, t.bit_length() - 1) + 1)
    lane = lax.broadcasted_iota(jnp.int32, (r, LANES), 1)
    low = lane < C_IDX_DIM
    wit = wi_ref[...].T
    qchunk = lax.shift_right_arithmetic(qrow0 + lax.broadcasted_iota(jnp.int32, (t, r), 1), 6)
    krow = lax.broadcasted_iota(jnp.int32, (t, r), 0)

    def admissible(kb):
        kpos = kpos0 + kb * t + krow
        return jnp.logical_and(kpos >= 0, lax.shift_right_arithmetic(kpos, 6) <= qchunk)

    qs = []
    for h in range(C_IDX_HEADS):
        qp = qi_ref[:, (h // 2) * LANES:(h // 2 + 1) * LANES]
        zero = jnp.zeros_like(qp)
        qs.append(jnp.where(low, qp, zero) if h % 2 == 0 else jnp.where(low, zero, qp))
    qs_ref[...] = jnp.concatenate(qs, axis=0)

    def score_tile(kb, carry):
        r0 = pl.multiple_of(kb * t, t)
        dots = jnp.maximum(_dot_nt(ki_ref[pl.ds(r0, t), :], qs_ref[...]), 0.0)
        score = jnp.zeros((t, r), F32)
        for h in range(C_IDX_HEADS):
            score = score + wit[h:h + 1, :] * dots[:, h * r:(h + 1) * r]
        score = jnp.where(admissible(kb), score, NEG_INF)
        bits = pltpu.bitcast(score, jnp.int32)
        key_ref[kb] = jnp.where(bits >= 0, bits, bits ^ jnp.int32(0x7FFFFFFF))
        return carry

    lax.fori_loop(0, n_adm, score_tile, 0)

    def count(pred):
        def tile(kb, acc):
            hit = jnp.where(pred(key_ref[kb], kb), 1.0, 0.0)
            return acc + jnp.sum(hit.reshape(t // SUBLANES, SUBLANES, r), axis=0)
        acc = lax.fori_loop(0, n_adm, tile, jnp.zeros((SUBLANES, r), F32))
        return jnp.sum(acc, axis=0, keepdims=True)

    thr = jnp.full((1, r), jnp.int32(-2 ** 31), jnp.int32)
    cand0 = jnp.zeros((1, r), jnp.int32)
    thr = jnp.where(count(lambda k, kb: k >= cand0) >= topk, cand0, thr)

    def value_bit(it, thr):
        cand = thr | lax.shift_left(jnp.int32(1), 30 - it)
        return jnp.where(count(lambda k, kb: k >= cand) >= topk, cand, thr)

    thr = lax.fori_loop(0, 31, value_bit, thr)

    n_ge = count(lambda k, kb: k >= thr)
    cut_ref[...] = jnp.full((1, r), n_keys, jnp.int32)
    n_bits = max(1, (n_keys - 1).bit_length())

    @pl.when(jnp.max(n_ge) > topk)
    def _():
        ties_wanted = topk - count(lambda k, kb: k > thr)

        def index_bit(it, cut):
            cand = cut | lax.shift_left(jnp.int32(1), n_bits - 1 - it)
            before = count(lambda k, kb: jnp.logical_and(k == thr, kb * t + krow < cand))
            return jnp.where(before <= ties_wanted - 1.0, cand, cut)

        cut_ref[...] = lax.fori_loop(0, n_bits, index_bit, jnp.zeros((1, r), jnp.int32))

    cut = cut_ref[...]
    for kb in range(n_tiles):
        @pl.when(kb < n_adm)
        def _():
            kk = key_ref[kb]
            chosen = jnp.logical_or(kk > thr, jnp.logical_and(kk == thr, kb * t + krow <= cut))
            valid = jnp.logical_and(chosen, admissible(kb))
            o_ref[kb] = jnp.where(valid, 0.0, NEG_INF).astype(BF16)

        @pl.when(kb >= n_adm)
        def _():
            o_ref[kb] = jnp.full((t, r), NEG_INF, BF16)


def _select_mask(qi_arr, qi_blk, wi_arr, ki_arr, *, batch, q_rows, n_keys, kpos0, qpos0, topk):
    r = min(SELECT_ROWS, q_rows)
    t = FLASH_TILE
    nq = q_rows // r
    cfg = (n_keys, kpos0, qpos0, topk)
    return pl.pallas_call(
        functools.partial(_select_t_body, cfg),
        grid=(batch, nq),
        in_specs=[
            pl.BlockSpec((r, C_IDX_HEADS * C_IDX_DIM), lambda b, i: (b * nq + i, qi_blk)),
            pl.BlockSpec((r, LANES), lambda b, i: (b * nq + i, 0)),
            pl.BlockSpec((n_keys, LANES), lambda b, i: (b, 0)),
        ],
        out_specs=pl.BlockSpec((None, n_keys // t, t, r), lambda b, i: (b * nq + i, 0, 0, 0)),
        out_shape=jax.ShapeDtypeStruct((batch * nq, n_keys // t, t, r), BF16),
        scratch_shapes=[pltpu.VMEM((n_keys // t, t, r), jnp.int32),
                        pltpu.VMEM((C_IDX_HEADS * r, LANES), BF16),
                        pltpu.VMEM((1, r), jnp.int32)],
        compiler_params=_params(2),
        name="select_mask",
    )(qi_arr, wi_arr, ki_arr)


def _flash_body(cfg, *refs):
    n_groups, n_pairs, mode, lambda_init, has_sel = cfg
    it = iter(refs)
    q_ref, k_ref, v_ref, bias_ref = next(it), next(it), next(it), next(it)
    sel_ref = next(it) if has_sel else None
    if mode == "diff":
        lam_ref, subln_ref = next(it), next(it)
    o_ref = next(it)
    qs_ref, m_ref, l_ref, acc_ref = next(it), next(it), next(it), next(it)
    t = FLASH_TILE
    n_heads = 2 * n_pairs
    rows = n_heads * t
    i = pl.program_id(2)
    lane = lax.broadcasted_iota(jnp.int32, (t, LANES), 1)
    low = lane < HEAD_DIM
    row = lax.broadcasted_iota(jnp.int32, (rows, t), 0) & (t - 1)
    col = lax.broadcasted_iota(jnp.int32, (rows, t), 1)
    diag_ok = lax.shift_right_arithmetic(col, 6) <= lax.shift_right_arithmetic(row, 6)

    for g in range(n_groups):
        qs = []
        for p in range(n_pairs):
            qp = q_ref[:, (g * n_pairs + p) * LANES:(g * n_pairs + p + 1) * LANES]
            qs += [jnp.where(low, qp, jnp.zeros_like(qp)), jnp.where(low, jnp.zeros_like(qp), qp)]
        qs_ref[g] = jnp.concatenate(qs, axis=0)
    m_ref[...] = jnp.full(m_ref.shape, NEG_INF, F32)
    l_ref[...] = jnp.zeros(l_ref.shape, F32)
    acc_ref[...] = jnp.zeros(acc_ref.shape, F32)

    def kstep(j, carry):
        r0 = pl.multiple_of(j * t, t)
        tt = jnp.minimum(i - j, 2)
        on_diag = j >= i
        ss = []
        for g in range(n_groups):
            kb = k_ref[pl.ds(r0, t), g * LANES:(g + 1) * LANES]
            s = _dot_nt(qs_ref[g], kb) + bias_ref[tt, g * n_heads:(g + 1) * n_heads].reshape(rows, t)
            s = jnp.where(jnp.logical_or(jnp.logical_not(on_diag), diag_ok), s, NEG_INF)
            if has_sel:
                s = (s.reshape(n_heads, t, t) + sel_ref[j].astype(F32)[None]).reshape(rows, t)
            ss.append(s)
        prs = []
        for g in range(n_groups):
            m_old = m_ref[g]
            m_new = jnp.maximum(m_old, jnp.max(ss[g], axis=-1, keepdims=True))
            alpha = jnp.exp(m_old - m_new)
            pr = jnp.exp(ss[g] - m_new)
            l_ref[g] = alpha * l_ref[g] + jnp.sum(pr, axis=-1, keepdims=True)
            m_ref[g] = m_new
            prs.append((pr.astype(BF16), alpha))
        for g in range(n_groups):
            pr, alpha = prs[g]
            vb = v_ref[pl.ds(r0, t), g * LANES:(g + 1) * LANES]
            acc_ref[g] = alpha * acc_ref[g] + jnp.dot(pr, vb, preferred_element_type=F32)
        return carry

    lax.fori_loop(0, i + 1, kstep, 0)

    if mode == "diff":
        lv = lam_ref[...]
        lam = (jnp.exp(jnp.sum(lv[0:1] * lv[1:2], axis=-1, keepdims=True))
               - jnp.exp(jnp.sum(lv[2:3] * lv[3:4], axis=-1, keepdims=True)) + lambda_init)
    for g in range(n_groups):
        res = acc_ref[g] / l_ref[g]
        for p in range(n_pairs):
            lo = res[2 * p * t:(2 * p + 1) * t]
            hi = res[(2 * p + 1) * t:(2 * p + 2) * t]
            if mode == "diff":
                o = _rms(lo - lam * hi, subln_ref[...]) * (1.0 - lambda_init)
            else:
                o = jnp.where(low, lo, hi)
            o_ref[:, (g * n_pairs + p) * LANES:(g * n_pairs + p + 1) * LANES] = o.astype(BF16)


def _flash_attention_rowmajor(q_arr, q_blk0, k_arr, k_blk0, v_arr, v_blk0, bias, *, batch, seq,
                              n_kv_pairs, n_groups, n_pairs, mode="plain", sel=None,
                              lam_vecs=None, subln=None, lambda_init=0.0):
    t = FLASH_TILE
    nq = seq // t
    qw = n_groups * n_pairs * LANES
    kw = n_groups * LANES
    rows = 2 * n_pairs * t
    n_steps = n_kv_pairs // n_groups
    cfg = (n_groups, n_pairs, mode, lambda_init, sel is not None)
    in_specs = [
        pl.BlockSpec((t, qw), lambda b, j, i: (b * nq + i, q_blk0 + j)),
        pl.BlockSpec((seq, kw), lambda b, j, i: (b, k_blk0 + j)),
        pl.BlockSpec((seq, kw), lambda b, j, i: (b, v_blk0 + j)),
        pl.BlockSpec((3, 2 * n_pairs * n_groups, t, t), lambda b, j, i: (0, j, 0, 0)),
    ]
    args = [q_arr, k_arr, v_arr, bias]
    if sel is not None:
        in_specs.append(pl.BlockSpec((None, nq, t, t), lambda b, j, i: (b * nq + i, 0, 0, 0)))
        args.append(sel)
    if mode == "diff":
        in_specs.append(pl.BlockSpec((4, HEAD_DIM), lambda b, j, i: (0, 0)))
        in_specs.append(pl.BlockSpec((1, LANES), lambda b, j, i: (0, 0)))
        args += [lam_vecs, subln.reshape(1, LANES)]
    return pl.pallas_call(
        functools.partial(_flash_body, cfg),
        grid=(batch, n_steps, nq),
        in_specs=in_specs,
        out_specs=pl.BlockSpec((t, qw), lambda b, j, i: (b * nq + i, j)),
        out_shape=jax.ShapeDtypeStruct((batch * seq, n_steps * qw), BF16),
        scratch_shapes=[pltpu.VMEM((n_groups, rows, LANES), BF16),
                        pltpu.VMEM((n_groups, rows, 1), F32),
                        pltpu.VMEM((n_groups, rows, 1), F32),
                        pltpu.VMEM((n_groups, rows, LANES), F32)],
        compiler_params=_params(3),
        name="flash_attention",
    )(*args)


def _select_body(cfg, qi_ref, wi_ref, ki_ref, o_ref, key_ref, qs_ref, cut_ref):
    n_keys, kpos0, qpos0, topk = cfg
    r = qi_ref.shape[0]
    t = FLASH_TILE
    n_tiles = n_keys // t
    qrow0 = qpos0 + pl.program_id(1) * r
    last_kpos = (lax.shift_right_arithmetic(qrow0 + r - 1, 6) + 1) * CHUNK - 1
    n_adm = jnp.minimum(n_tiles,
                        lax.shift_right_arithmetic(last_kpos - kpos0, t.bit_length() - 1) + 1)
    lane = lax.broadcasted_iota(jnp.int32, (r, LANES), 1)
    low = lane < C_IDX_DIM
    wi = wi_ref[...]
    qchunk = lax.shift_right_arithmetic(qrow0 + lax.broadcasted_iota(jnp.int32, (r, t), 0), 6)
    col_t = lax.broadcasted_iota(jnp.int32, (r, t), 1)

    def admissible(kb):
        kpos = kpos0 + kb * t + col_t
        return jnp.logical_and(kpos >= 0, lax.shift_right_arithmetic(kpos, 6) <= qchunk)

    qs = []
    for h in range(C_IDX_HEADS):
        qp = qi_ref[:, (h // 2) * LANES:(h // 2 + 1) * LANES]
        zero = jnp.zeros_like(qp)
        qs.append(jnp.where(low, qp, zero) if h % 2 == 0 else jnp.where(low, zero, qp))
    qs_ref[...] = jnp.concatenate(qs, axis=0)

    def score_tile(kb, carry):
        r0 = pl.multiple_of(kb * t, t)
        dots = jnp.maximum(_dot_nt(qs_ref[...], ki_ref[pl.ds(r0, t), :]), 0.0)
        score = jnp.zeros((r, t), F32)
        for h in range(C_IDX_HEADS):
            score = score + wi[:, h:h + 1] * dots[h * r:(h + 1) * r]
        score = jnp.where(admissible(kb), score, NEG_INF)
        bits = pltpu.bitcast(score, jnp.int32)
        key_ref[kb] = jnp.where(bits >= 0, bits, bits ^ jnp.int32(0x7FFFFFFF))
        return carry

    lax.fori_loop(0, n_adm, score_tile, 0)

    def count(pred):
        def tile(kb, acc):
            hit = jnp.where(pred(key_ref[kb], kb), 1.0, 0.0)
            for c in range(0, t, LANES):
                acc = acc + hit[:, c:c + LANES]
            return acc
        acc = lax.fori_loop(0, n_adm, tile, jnp.zeros((r, LANES), F32))
        return jnp.sum(acc, axis=-1, keepdims=True)

    thr = jnp.full((r, 1), jnp.int32(-2 ** 31), jnp.int32)
    cand0 = jnp.zeros((r, 1), jnp.int32)
    thr = jnp.where(count(lambda k, kb: k >= cand0) >= topk, cand0, thr)

    def value_bit(it, thr):
        cand = thr | lax.shift_left(jnp.int32(1), 30 - it)
        return jnp.where(count(lambda k, kb: k >= cand) >= topk, cand, thr)

    thr = lax.fori_loop(0, 31, value_bit, thr)

    n_ge = count(lambda k, kb: k >= thr)
    cut_ref[...] = jnp.full((r, 1), n_keys, jnp.int32)
    n_bits = max(1, (n_keys - 1).bit_length())

    @pl.when(jnp.max(n_ge) > topk)
    def _():
        ties_wanted = topk - count(lambda k, kb: k > thr)

        def index_bit(it, cut):
            cand = cut | lax.shift_left(jnp.int32(1), n_bits - 1 - it)
            before = count(lambda k, kb: jnp.logical_and(k == thr, kb * t + col_t < cand))
            return jnp.where(before <= ties_wanted - 1.0, cand, cut)

        cut_ref[...] = lax.fori_loop(0, n_bits, index_bit, jnp.zeros((r, 1), jnp.int32))

    cut = cut_ref[...]
    for kb in range(n_tiles):
        @pl.when(kb < n_adm)
        def _():
            kk = key_ref[kb]
            chosen = jnp.logical_or(kk > thr, jnp.logical_and(kk == thr, kb * t + col_t <= cut))
            valid = jnp.logical_and(chosen, admissible(kb))
            o_ref[kb] = jnp.where(valid, 0.0, NEG_INF).astype(BF16)

        @pl.when(kb >= n_adm)
        def _():
            o_ref[kb] = jnp.full((r, t), NEG_INF, BF16)


def _select_mask_rowmajor(qi_arr, qi_blk, wi_arr, ki_arr, *, batch, q_rows, n_keys, kpos0, qpos0,
                          topk):
    r = min(SELECT_ROWS, q_rows)
    t = FLASH_TILE
    nq = q_rows // r
    cfg = (n_keys, kpos0, qpos0, topk)
    return pl.pallas_call(
        functools.partial(_select_body, cfg),
        grid=(batch, nq),
        in_specs=[
            pl.BlockSpec((r, C_IDX_HEADS * C_IDX_DIM), lambda b, i: (b * nq + i, qi_blk)),
            pl.BlockSpec((r, LANES), lambda b, i: (b * nq + i, 0)),
            pl.BlockSpec((n_keys, LANES), lambda b, i: (b, 0)),
        ],
        out_specs=pl.BlockSpec((None, n_keys // t, r, t), lambda b, i: (b * nq + i, 0, 0, 0)),
        out_shape=jax.ShapeDtypeStruct((batch * nq, n_keys // t, r, t), BF16),
        scratch_shapes=[pltpu.VMEM((n_keys // t, r, t), jnp.int32),
                        pltpu.VMEM((C_IDX_HEADS * r, LANES), BF16),
                        pltpu.VMEM((r, 1), jnp.int32)],
        compiler_params=_params(2),
        name="select_mask",
    )(qi_arr, wi_arr, ki_arr)


FLASH_SUM_ROWS = 16
FFN_PAD = 16


def _block_ffn_body(cfg, *refs):
    seq_tiles, seq_rows, stream, final_norm = cfg
    it = iter(refs)
    x_ref, att_ref = next(it), next(it)
    if stream:
        xprev_ref, attprev_ref = next(it), next(it)
    else:
        fix1_ref, fix2_ref = next(it), next(it)
    wo_ref, g_ref, win_ref, cw_ref, cb_ref, wout_ref = (next(it), next(it), next(it), next(it),
                                                        next(it), next(it))
    gfin_ref = next(it) if final_norm else None
    o_ref, st_ref = next(it), next(it)
    hext_ref, act_ref = next(it), next(it)
    tm = x_ref.shape[0]
    pad = FFN_PAD
    gain = g_ref[...]
    if stream:
        aext_ref = next(it)
        aext_ref[0:pad, :] = attprev_ref[...]
        aext_ref[pad:, :] = att_ref[...]
        xext = jnp.concatenate([xprev_ref[...], x_ref[...]], axis=0)
        x1 = xext + jnp.dot(aext_ref[...], wo_ref[...], preferred_element_type=F32)
        o_ref[...] = x1[pad:, :]
        not_start = (pl.program_id(0) % seq_tiles) != 0
        rowi = lax.broadcasted_iota(jnp.int32, (tm + pad, 1), 0)
        keep = jnp.logical_or(rowi >= pad, not_start)
        hext_ref[...] = jnp.where(keep, _rms(x1, gain), 0.0).astype(BF16)
    else:
        x1 = x_ref[...] + jnp.dot(att_ref[...], wo_ref[...], preferred_element_type=F32)
        o_ref[...] = x1
        hext_ref[pad:, :] = _rms(x1, gain).astype(BF16)
        hext_ref[0:pad, :] = jnp.zeros((pad, D_MODEL), BF16)
        assert seq_rows & (seq_rows - 1) == 0
        rmod = lax.broadcasted_iota(jnp.int32, (tm, FF_CHUNK), 0) & (seq_rows - 1)
    hext = hext_ref[...]
    for c0 in range(0, D_FF, FF_CHUNK):
        ag = jnp.dot(hext, win_ref[:, 2 * c0:2 * c0 + 2 * FF_CHUNK], preferred_element_type=F32)
        a_ext = ag[:, :FF_CHUNK]
        gate = ag[pad:, FF_CHUNK:]
        a = a_ext[pad:, :]
        a1 = pltpu.roll(a_ext, 1, 0)[pad:, :]
        a2 = pltpu.roll(a_ext, 2, 0)[pad:, :]
        if not stream:
            a1 = jnp.where(rmod == 0, fix1_ref[:, c0:c0 + FF_CHUNK], a1)
            a2 = jnp.where(rmod <= 1, fix2_ref[:, c0:c0 + FF_CHUNK], a2)
        cw = cw_ref[:, c0:c0 + FF_CHUNK]
        u = cb_ref[:, c0:c0 + FF_CHUNK] + (cw[0:1] * a2 + cw[1:2] * a1 + cw[2:3] * a)
        act_ref[:, c0:c0 + FF_CHUNK] = (jax.nn.silu(u) * gate).astype(BF16)
        st_ref[:, c0:c0 + FF_CHUNK] = a[tm - SUBLANES:, :] if stream else a
    y = o_ref[...] + jnp.dot(act_ref[...], wout_ref[...], preferred_element_type=F32)
    if final_norm:
        y = _rms(y, gfin_ref[...])
    o_ref[...] = y


def _block_ffn(x, attn, w_o, gain, w_in, conv_w, conv_b, w_out, *, tm, seq_tiles, seq_rows,
               fixes=None, final_gain=None):
    n = x.shape[0]
    stream = fixes is None
    final_norm = final_gain is not None
    cfg = (seq_tiles, seq_rows, stream, final_norm)
    in_specs = [pl.BlockSpec((tm, D_MODEL), lambda i: (i, 0)),
                pl.BlockSpec((tm, ATTN_WIDTH), lambda i: (i, 0))]
    args = [x, attn]
    if stream:
        per = tm // FFN_PAD

        def prev(i):
            return (jnp.maximum(i * per - 1, 0), 0)
        in_specs += [pl.BlockSpec((FFN_PAD, D_MODEL), prev), pl.BlockSpec((FFN_PAD, ATTN_WIDTH), prev)]
        args += [x, attn]
    else:
        in_specs += [pl.BlockSpec((tm, D_FF), lambda i: (i, 0))] * 2
        args += list(fixes)
    in_specs += [
        _resident((ATTN_WIDTH, D_MODEL), lambda i: (0, 0)),
        _resident((1, D_MODEL), lambda i: (0, 0)),
        _resident((D_MODEL, 2 * D_FF), lambda i: (0, 0)),
        _resident((CONV_W, D_FF), lambda i: (0, 0)),
        _resident((1, D_FF), lambda i: (0, 0)),
        _resident((D_FF, D_MODEL), lambda i: (0, 0)),
    ]
    args += [w_o, gain.reshape(1, D_MODEL), w_in, conv_w, conv_b.reshape(1, D_FF), w_out]
    if final_norm:
        in_specs.append(_resident((1, D_MODEL), lambda i: (0, 0)))
        args.append(final_gain.reshape(1, D_MODEL))
    if stream:
        st_shape = jax.ShapeDtypeStruct((n // (tm * seq_tiles) * SUBLANES, D_FF), F32)
        st_spec = pl.BlockSpec((SUBLANES, D_FF), lambda i: (i // seq_tiles, 0))
    else:
        st_shape = jax.ShapeDtypeStruct((n, D_FF), F32)
        st_spec = pl.BlockSpec((tm, D_FF), lambda i: (i, 0))
    return pl.pallas_call(
        functools.partial(_block_ffn_body, cfg),
        grid=(n // tm,),
        in_specs=in_specs,
        out_specs=[pl.BlockSpec((tm, D_MODEL), lambda i: (i, 0)), st_spec],
        out_shape=[jax.ShapeDtypeStruct((n, D_MODEL), F32), st_shape],
        scratch_shapes=[pltpu.VMEM((tm + FFN_PAD, D_MODEL), BF16), pltpu.VMEM((tm, D_FF), BF16)]
        + ([pltpu.VMEM((tm + FFN_PAD, ATTN_WIDTH), BF16)] if stream else []),
        compiler_params=_params(1),
        name="block_ffn",
    )(*args)


def _oproj_body(x_ref, a_ref, w_ref, o_ref):
    o_ref[...] = x_ref[...] + jnp.dot(a_ref[...], w_ref[...], preferred_element_type=F32)


def _oproj_residual(x, attn, w_o, tm):
    n = x.shape[0]
    return pl.pallas_call(
        _oproj_body,
        grid=(n // tm,),
        in_specs=[
            pl.BlockSpec((tm, D_MODEL), lambda i: (i, 0)),
            pl.BlockSpec((tm, ATTN_WIDTH), lambda i: (i, 0)),
            _resident((ATTN_WIDTH, D_MODEL), lambda i: (0, 0)),
        ],
        out_specs=pl.BlockSpec((tm, D_MODEL), lambda i: (i, 0)),
        out_shape=jax.ShapeDtypeStruct((n, D_MODEL), F32),
        compiler_params=_params(1),
        name="oproj_residual",
    )(x, attn, w_o)


def _ffn_body(cfg, *refs):
    seq_tiles, seq_rows, stream, final_norm = cfg
    it = iter(refs)
    x_ref = next(it)
    if stream:
        xprev_ref = next(it)
    else:
        fix1_ref, fix2_ref = next(it), next(it)
    g_ref, win_ref, cw_ref, cb_ref, wout_ref = next(it), next(it), next(it), next(it), next(it)
    gfin_ref = next(it) if final_norm else None
    o_ref, st_ref = next(it), next(it)
    hext_ref, act_ref = next(it), next(it)
    tm = x_ref.shape[0]
    pad = SUBLANES
    x = x_ref[...]
    gain = g_ref[...]
    hext_ref[pad:, :] = _rms(x, gain).astype(BF16)
    if stream:
        not_start = ((pl.program_id(0) % seq_tiles) != 0).astype(F32)
        hext_ref[0:pad, :] = (_rms(xprev_ref[...], gain) * not_start).astype(BF16)
    else:
        hext_ref[0:pad, :] = jnp.zeros((pad, D_MODEL), BF16)
    hext = hext_ref[...]
    h = hext[pad:, :]
    is_seq_end = (pl.program_id(0) % seq_tiles) == seq_tiles - 1
    if not stream:
        assert seq_rows & (seq_rows - 1) == 0
        rmod = lax.broadcasted_iota(jnp.int32, (tm, FF_CHUNK), 0) & (seq_rows - 1)
    for c0 in range(0, D_FF, FF_CHUNK):
        a_ext = jnp.dot(hext, win_ref[:, c0:c0 + FF_CHUNK], preferred_element_type=F32)
        gate = jnp.dot(h, win_ref[:, D_FF + c0:D_FF + c0 + FF_CHUNK], preferred_element_type=F32)
        a = a_ext[pad:, :]
        a1 = pltpu.roll(a_ext, 1, 0)[pad:, :]
        a2 = pltpu.roll(a_ext, 2, 0)[pad:, :]
        if not stream:
            a1 = jnp.where(rmod == 0, fix1_ref[:, c0:c0 + FF_CHUNK], a1)
            a2 = jnp.where(rmod <= 1, fix2_ref[:, c0:c0 + FF_CHUNK], a2)
        cw = cw_ref[:, c0:c0 + FF_CHUNK]
        u = cb_ref[:, c0:c0 + FF_CHUNK] + (cw[0:1] * a2 + cw[1:2] * a1 + cw[2:3] * a)
        act_ref[:, c0:c0 + FF_CHUNK] = (jax.nn.silu(u) * gate).astype(BF16)
        if stream:
            @pl.when(is_seq_end)
            def _():
                st_ref[:, c0:c0 + FF_CHUNK] = a[tm - pad:, :]
        else:
            st_ref[:, c0:c0 + FF_CHUNK] = a
    y = x + jnp.dot(act_ref[...], wout_ref[...], preferred_element_type=F32)
    if final_norm:
        y = _rms(y, gfin_ref[...])
    o_ref[...] = y


def _conv_ffn(x, gain, w_in, conv_w, conv_b, w_out, *, tm, seq_tiles, seq_rows, fixes=None,
              final_gain=None):
    n = x.shape[0]
    stream = fixes is None
    final_norm = final_gain is not None
    cfg = (seq_tiles, seq_rows, stream, final_norm)
    in_specs = [pl.BlockSpec((tm, D_MODEL), lambda i: (i, 0))]
    args = [x]
    if stream:
        per = tm // SUBLANES
        in_specs.append(pl.BlockSpec((SUBLANES, D_MODEL), lambda i: (jnp.maximum(i * per - 1, 0), 0)))
        args.append(x)
    else:
        in_specs += [pl.BlockSpec((tm, D_FF), lambda i: (i, 0))] * 2
        args += list(fixes)
    in_specs += [
        _resident((1, D_MODEL), lambda i: (0, 0)),
        _resident((D_MODEL, 2 * D_FF), lambda i: (0, 0)),
        _resident((CONV_W, D_FF), lambda i: (0, 0)),
        _resident((1, D_FF), lambda i: (0, 0)),
        _resident((D_FF, D_MODEL), lambda i: (0, 0)),
    ]
    args += [gain.reshape(1, D_MODEL), w_in, conv_w, conv_b.reshape(1, D_FF), w_out]
    if final_norm:
        in_specs.append(_resident((1, D_MODEL), lambda i: (0, 0)))
        args.append(final_gain.reshape(1, D_MODEL))
    if stream:
        st_shape = jax.ShapeDtypeStruct((n // (tm * seq_tiles) * SUBLANES, D_FF), F32)
        st_spec = pl.BlockSpec((SUBLANES, D_FF), lambda i: (i // seq_tiles, 0))
    else:
        st_shape = jax.ShapeDtypeStruct((n, D_FF), F32)
        st_spec = pl.BlockSpec((tm, D_FF), lambda i: (i, 0))
    return pl.pallas_call(
        functools.partial(_ffn_body, cfg),
        grid=(n // tm,),
        in_specs=in_specs,
        out_specs=[pl.BlockSpec((tm, D_MODEL), lambda i: (i, 0)), st_spec],
        out_shape=[jax.ShapeDtypeStruct((n, D_MODEL), F32), st_shape],
        scratch_shapes=[pltpu.VMEM((tm + SUBLANES, D_MODEL), BF16), pltpu.VMEM((tm, D_FF), BF16)],
        compiler_params=_params(1),
        name="conv_ffn",
    )(*args)


def _t5_bucket(rel):
    half = NUM_BUCKETS // 2
    max_exact = half // 2
    base = jnp.where(rel > 0, half, 0)
    n = jnp.abs(rel)
    nf = jnp.maximum(n, 1).astype(F32)
    large = max_exact + (jnp.log(nf / max_exact) / math.log(T5_MAX_DISTANCE / max_exact)
                         * (half - max_exact)).astype(jnp.int32)
    large = jnp.minimum(large, half - 1)
    return base + jnp.where(n < max_exact, n, large)


def _lookup_heads(table, idx):
    onehot = (idx[..., None] == jnp.arange(table.shape[0])).astype(F32)
    return jnp.einsum("...n,nh->h...", onehot, table.astype(F32), precision=lax.Precision.HIGHEST)


def _t5_bias(table, rel):
    return _lookup_heads(table, _t5_bucket(rel))


def _band_rel(win, n_past_eff):
    i = jnp.arange(CHUNK)[:, None]
    j = jnp.arange(win)[None, :]
    return (j - n_past_eff) - i


def _flash_rel():
    t = FLASH_TILE
    r = jnp.arange(t)[:, None]
    c = jnp.arange(t)[None, :]
    return jnp.stack([c - r - d * t for d in range(3)])


def _flash_bias(bias):
    h, kinds, tq, tk = bias.shape
    ok = (jnp.arange(tk)[None, :] // CHUNK) <= (jnp.arange(tq)[:, None] // CHUNK)
    bias = bias.at[:, 0].set(jnp.where(ok[None], bias[:, 0], NEG_INF))
    bias = jnp.concatenate([bias, jnp.full((h, 1, tq, tk), NEG_INF, F32)], axis=1)
    return bias.transpose(1, 3, 0, 2).reshape(kinds + 1, tk, h * tq)


def _perm_cols(w, perm):
    return w.reshape(w.shape[0], len(perm), HEAD_DIM)[:, perm, :].reshape(w.shape[0], -1)


def _perm_rows(w, perm):
    return w.reshape(len(perm), HEAD_DIM, w.shape[1])[perm, :, :].reshape(-1, w.shape[1])


def _pad_keys(cache, new, pad, width):
    b = cache.shape[0]
    allk = jnp.concatenate([cache.reshape(b, -1, width), new.reshape(b, -1, width)], axis=1)
    padded = jnp.pad(allk, ((0, 0), (pad, 0), (0, 0))).astype(BF16)
    return allk, padded.reshape(-1, width)


def kernel(x_prompt, x_sample, cache_a_k, cache_a_v, cache_b_k, cache_b_v, cache_c_k, cache_c_v,
           cache_c_kidx, cache_d_k, cache_d_v, state_ffn_conv, t5_table, norm_mix, norm_ffn,
           norm_final, a_w_qkv, a_w_o, a_rel_bias, b_w_qkv, b_w_o, b_sinks, c_w_qkv, c_w_o,
           c_w_idx_q, c_w_idx_k, c_idx_k_norm, c_w_idx_w, d_w_qkv, d_w_o, d_lambda_q1,
           d_lambda_k1, d_lambda_q2, d_lambda_k2, d_subln, ffn_w_in, ffn_conv_w, ffn_conv_b,
           ffn_w_out):
    bp, seq, d = x_prompt.shape
    bs, ts, _ = x_sample.shape
    past = cache_c_k.shape[1]
    assert d == D_MODEL and ts == CHUNK and seq % ROW_TILE == 0 and past % CHUNK == 0
    n_p, n_s = bp * seq, bs * ts
    seq_tiles = seq // ROW_TILE
    depth = norm_mix.shape[0]
    scale = HEAD_DIM ** -0.5 * LOG2E
    t5_table = t5_table.astype(F32) * LOG2E
    perm = jnp.array(GQA_PERM)
    none_aux = jnp.zeros((1, LANES), F32)
    kvw = KV_HEADS * HEAD_DIM

    xp = x_prompt.reshape(n_p, d)
    xs = x_sample.reshape(n_s, d)

    def cast_dests(col_dests):
        secs = []
        for c0, c1, fn in col_dests:
            for c in range(c0, c1, PROJ_CHUNK):
                secs.append((c, min(PROJ_CHUNK, c1 - c), fn(c)))
        return secs

    layer = 0
    w = a_w_qkv.astype(BF16)
    aw = ATTN_WIDTH
    a_keep = min(A_PAST, seq)
    secs_p = cast_dests([
        (0, aw, lambda c: [("cast", 0, c, scale)]),
        (aw, 2 * aw, lambda c: [("cast", 0, c, None), ("tail", 1, c - aw, a_keep)]),
        (2 * aw, 3 * aw, lambda c: [("cast", 0, c, None), ("tail", 2, c - 2 * aw, a_keep)]),
    ])
    a_cache_out = (bp * a_keep, aw, F32, a_keep, "tail")
    qkv, a_k_p, a_v_p = _norm_proj(
        xp, norm_mix[layer], w, none_aux, secs_p,
        [(n_p, 3 * aw, BF16, ROW_TILE, "all"), a_cache_out, a_cache_out], seq_tiles, ROW_TILE)
    secs_s = cast_dests([
        (0, aw, lambda c: [("cast", 0, c, scale)]),
        (aw, 3 * aw, lambda c: [("cast", 1, c - aw, None)]),
    ])
    q_s, kv_s = _norm_proj(xs, norm_mix[layer], w, none_aux, secs_s,
                           [(n_s, aw, BF16, n_s, "all"), (n_s, 2 * aw, F32, n_s, "all")], 1, n_s)
    a_pad = CHUNK
    a_win = A_PAST + CHUNK + a_pad
    rel = _band_rel(a_win, A_PAST + a_pad)
    bias_a = _lookup_heads(a_rel_bias.astype(F32) * LOG2E,jnp.clip(rel, -A_CLIP, A_CLIP) + A_CLIP)
    bias_a = jnp.where((jnp.arange(a_win) >= a_pad)[None, None, :], bias_a, NEG_INF)
    att_p = _band_attention(qkv, 0, qkv, aw // LANES, qkv, 2 * aw // LANES, bias_a, batch=bp,
                            q_rows=seq, k_rows=seq, n_kv_pairs=N_HEADS // 2, n_pairs=1,
                            win=a_win, n_past=A_PAST + a_pad, chunks_in_flight=8)
    ks = kv_s[:, :aw].reshape(bs, ts, aw)
    vs = kv_s[:, aw:].reshape(bs, ts, aw)
    k_all, k_in = _pad_keys(cache_a_k, ks, a_pad, aw)
    v_all, v_in = _pad_keys(cache_a_v, vs, a_pad, aw)
    assert k_all.shape[1] + a_pad == a_win
    att_s = _band_attention(q_s, 0, k_in, 0, v_in, 0, bias_a, batch=bs, q_rows=ts, k_rows=a_win,
                            n_kv_pairs=N_HEADS // 2, n_pairs=1, win=a_win, n_past=0)
    a_keep_s = min(A_PAST, k_all.shape[1])
    a_k_prompt = a_k_p.reshape(bp, a_keep, N_HEADS, HEAD_DIM)
    a_v_prompt = a_v_p.reshape(bp, a_keep, N_HEADS, HEAD_DIM)
    a_k_sample = k_all[:, -a_keep_s:].reshape(bs, a_keep_s, N_HEADS, HEAD_DIM)
    a_v_sample = v_all[:, -a_keep_s:].reshape(bs, a_keep_s, N_HEADS, HEAD_DIM)
    w_o = a_w_o.astype(BF16)
    xp, xs, conv_p0, conv_s0 = _ffn_layer(xp, xs, att_p, att_s, w_o, layer, bp, bs, seq_tiles, ts, norm_ffn,
                                          ffn_w_in, ffn_conv_w, ffn_conv_b, ffn_w_out,
                                          state_ffn_conv, None)

    layer = 1
    w = jnp.concatenate([_perm_cols(b_w_qkv[:, :aw], perm), b_w_qkv[:, aw:]], axis=1).astype(BF16)
    b_keep = min(B_WINDOW, seq)
    secs_p = cast_dests([
        (0, aw, lambda c: [("cast", 0, c, scale)]),
        (aw, aw + kvw, lambda c: [("cast", 0, c, None), ("tail", 1, c - aw, b_keep)]),
        (aw + kvw, aw + 2 * kvw, lambda c: [("cast", 0, c, None), ("tail", 2, c - aw - kvw, b_keep)]),
    ])
    qkv, b_k_p, b_v_p = _norm_proj(
        xp, norm_mix[layer], w, none_aux, secs_p,
        [(n_p, aw + 2 * kvw, BF16, ROW_TILE, "all"), (bp * b_keep, kvw, F32, b_keep, "tail"),
         (bp * b_keep, kvw, F32, b_keep, "tail")], seq_tiles, ROW_TILE)
    secs_s = cast_dests([
        (0, aw, lambda c: [("cast", 0, c, scale)]),
        (aw, aw + 2 * kvw, lambda c: [("cast", 1, c - aw, None)]),
    ])
    q_s, kv_s = _norm_proj(xs, norm_mix[layer], w, none_aux, secs_s,
                           [(n_s, aw, BF16, n_s, "all"), (n_s, 2 * kvw, F32, n_s, "all")], 1, n_s)
    b_pad = CHUNK
    b_win = B_WINDOW + CHUNK + b_pad
    bias_b = _t5_bias(t5_table, _band_rel(b_win, B_WINDOW + b_pad))[perm]
    bias_b = jnp.where((jnp.arange(b_win) >= b_pad)[None, None, :], bias_b, NEG_INF)
    sinks = b_sinks.astype(F32)[perm] * LOG2E
    n_kvp = KV_HEADS // 2
    gq = N_HEADS // KV_HEADS
    att_p = _band_attention(qkv, 0, qkv, aw // LANES, qkv, (aw + kvw) // LANES, bias_b, batch=bp,
                            q_rows=seq, k_rows=seq, n_kv_pairs=n_kvp, n_pairs=gq, win=b_win,
                            n_past=B_WINDOW + b_pad, mode="sink", sinks=sinks)
    ks = kv_s[:, :kvw].reshape(bs, ts, kvw)
    vs = kv_s[:, kvw:].reshape(bs, ts, kvw)
    k_all, k_in = _pad_keys(cache_b_k, ks, b_pad, kvw)
    v_all, v_in = _pad_keys(cache_b_v, vs, b_pad, kvw)
    assert k_all.shape[1] + b_pad == b_win
    att_s = _band_attention(q_s, 0, k_in, 0, v_in, 0, bias_b, batch=bs, q_rows=ts, k_rows=b_win,
                            n_kv_pairs=n_kvp, n_pairs=gq, win=b_win, n_past=0, mode="sink",
                            sinks=sinks)
    b_keep_s = min(B_WINDOW, k_all.shape[1])
    b_k_prompt = b_k_p.reshape(bp, b_keep, KV_HEADS, HEAD_DIM)
    b_v_prompt = b_v_p.reshape(bp, b_keep, KV_HEADS, HEAD_DIM)
    b_k_sample = k_all[:, -b_keep_s:].reshape(bs, b_keep_s, KV_HEADS, HEAD_DIM)
    b_v_sample = v_all[:, -b_keep_s:].reshape(bs, b_keep_s, KV_HEADS, HEAD_DIM)
    w_o = _perm_rows(b_w_o, perm).astype(BF16)
    xp, xs, conv_p1, conv_s1 = _ffn_layer(xp, xs, att_p, att_s, w_o, layer, bp, bs, seq_tiles, ts, norm_ffn,
                                          ffn_w_in, ffn_conv_w, ffn_conv_b, ffn_w_out,
                                          state_ffn_conv, None)

    layer = 2
    iw = C_IDX_HEADS * C_IDX_DIM
    w_idx_w = jnp.pad(c_w_idx_w, ((0, 0), (0, LANES - C_IDX_HEADS)))
    w = jnp.concatenate([_perm_cols(c_w_qkv[:, :aw], perm), c_w_qkv[:, aw:], c_w_idx_q,
                         c_w_idx_k, c_w_idx_k, w_idx_w], axis=1).astype(BF16)
    c_qkv_w = aw + 2 * kvw
    col_ki = c_qkv_w + iw
    col_wi = col_ki + LANES
    knorm = jnp.concatenate([c_idx_k_norm, c_idx_k_norm]).astype(F32).reshape(1, LANES)
    wi_scale = C_IDX_HEADS ** -0.5

    def c_sections(k_out, v_out, qkv_out, by_head):
        def cache(o, c0):
            if by_head:
                return lambda c: ("heads", o, (c - c0) // HEAD_DIM, HEAD_DIM, KV_HEADS, None)
            return lambda c: ("cast", o, c - c0, None)
        k_dest, v_dest = cache(k_out, aw), cache(v_out, aw + kvw)

        def v_dests(c):
            dests = [("cast", qkv_out, c, None), v_dest(c)]
            return dests + [("xpose", 6, c - aw - kvw)] if by_head else dests
        secs = cast_dests([
            (0, aw, lambda c: [("cast", qkv_out, c, scale)]),
            (aw, aw + kvw, lambda c: [("cast", qkv_out, c, None), k_dest(c)]),
            (aw + kvw, c_qkv_w, v_dests),
            (c_qkv_w, col_ki, lambda c: [("cast", qkv_out, c, C_IDX_DIM ** -0.5)]),
        ])
        secs.append((col_ki, LANES, [("kidx", 3, 4)]))
        secs.append((col_wi, LANES, [("cast", 5, 0, wi_scale)]))
        return secs

    def c_outs(n, tm, by_head):
        cache = ((n * KV_HEADS, HEAD_DIM, F32, tm * KV_HEADS, "all") if by_head
                 else (n, kvw, F32, tm, "all"))
        outs = [(n, c_qkv_w + iw, BF16, tm, "all"), cache, cache, (n, LANES, BF16, tm, "all"),
                (n, C_IDX_DIM, F32, tm, "all"), (n, LANES, F32, tm, "all")]
        return outs + [(kvw, n, BF16, kvw, "cols")] if by_head else outs

    qkv, c_k_p, c_v_p, ki_p, kidx_p, wi_p, vt_p = _norm_proj(
        xp, norm_mix[layer], w, knorm, c_sections(1, 2, 0, True), c_outs(n_p, ROW_TILE, True),
        seq_tiles, ROW_TILE)
    qkv_s, c_k_s, c_v_s, ki_s, kidx_s, wi_s = _norm_proj(
        xs, norm_mix[layer], w, knorm, c_sections(1, 2, 0, False), c_outs(n_s, n_s, False), 1, n_s)
    qi_blk = c_qkv_w // iw
    assert qi_blk * iw == c_qkv_w
    t = FLASH_TILE
    assert SELECT_ROWS == t
    sel_p = _select_mask(qkv, qi_blk, wi_p, ki_p, batch=bp, q_rows=seq, n_keys=seq, kpos0=0,
                         qpos0=0, topk=min(C_TOPK, seq // 4))
    bias_c = _flash_bias(_t5_bias(t5_table, _flash_rel())[perm])
    att_p = _flash_attention(qkv, 0, qkv, aw // LANES, vt_p, 0, bias_c, batch=bp,
                             seq=seq, n_kv_pairs=n_kvp, n_groups=1, n_pairs=gq, sel=sel_p)
    n_keys_s = past + ts
    c_pad = (-n_keys_s) % t
    c_win = n_keys_s + c_pad
    k_all, k_in = _pad_keys(cache_c_k, c_k_s.reshape(bs, ts, kvw), c_pad, kvw)
    v_all, v_in = _pad_keys(cache_c_v, c_v_s.reshape(bs, ts, kvw), c_pad, kvw)
    del ki_s
    _, ki_in = _pad_keys(cache_c_kidx, kidx_s.reshape(bs, ts, C_IDX_DIM), c_pad, C_IDX_DIM)
    ki_in = jnp.concatenate([ki_in, ki_in], axis=1)
    qi_s = jnp.pad(qkv_s[:, c_qkv_w:].reshape(bs, ts, iw), ((0, 0), (0, LANES - ts), (0, 0)))
    wi_s = jnp.pad(wi_s.reshape(bs, ts, LANES), ((0, 0), (0, LANES - ts), (0, 0)))
    sel_s = _select_mask(qi_s.reshape(bs * LANES, iw), 0, wi_s.reshape(bs * LANES, LANES), ki_in,
                         batch=bs, q_rows=LANES, n_keys=c_win, kpos0=-c_pad, qpos0=past,
                         topk=min(C_TOPK, n_keys_s // 4))
    sel_s = sel_s[..., :ts].transpose(0, 3, 1, 2).reshape(bs, ts, c_win)
    rel_s = (jnp.arange(c_win)[None, :] - c_pad) - (past + jnp.arange(ts)[:, None])
    pad_ok = (jnp.arange(c_win) >= c_pad)[None, None, :]
    bias_s_t5 = jnp.where(pad_ok, _t5_bias(t5_table, rel_s), NEG_INF)
    att_s = _band_attention(qkv_s, 0, k_in, 0, v_in, 0, bias_s_t5[perm], batch=bs, q_rows=ts,
                            k_rows=c_win, n_kv_pairs=n_kvp, n_pairs=gq, win=c_win, n_past=0,
                            sel=sel_s)
    c_k_prompt = c_k_p.reshape(bp, seq, KV_HEADS, HEAD_DIM)
    c_v_prompt = c_v_p.reshape(bp, seq, KV_HEADS, HEAD_DIM)
    c_kidx_prompt = kidx_p.reshape(bp, seq, C_IDX_DIM)
    c_k_sample = c_k_s.reshape(bs, ts, KV_HEADS, HEAD_DIM)
    c_v_sample = c_v_s.reshape(bs, ts, KV_HEADS, HEAD_DIM)
    c_kidx_sample = kidx_s.reshape(bs, ts, C_IDX_DIM)
    w_o = _perm_rows(c_w_o, perm).astype(BF16)
    xp, xs, conv_p2, conv_s2 = _ffn_layer(xp, xs, att_p, att_s, w_o, layer, bp, bs, seq_tiles, ts, norm_ffn,
                                          ffn_w_in, ffn_conv_w, ffn_conv_b, ffn_w_out,
                                          state_ffn_conv, None)

    layer = 3
    lambda_init = 0.8 - 0.6 * math.exp(-0.3 * layer)
    w = d_w_qkv.astype(BF16)
    lam_vecs = jnp.stack([d_lambda_q1, d_lambda_k1, d_lambda_q2, d_lambda_k2]).astype(F32)
    secs = cast_dests([
        (0, aw, lambda c: [("cast", 0, c, scale)]),
        (aw, 2 * aw, lambda c: [("cast", 0, c, None), ("cast", 1, c - aw, None)]),
        (2 * aw, 3 * aw, lambda c: [("cast", 0, c, None), ("cast", 2, c - 2 * aw, None)]),
    ])

    def d_outs(n, tm):
        return [(n, 3 * aw, BF16, tm, "all"), (n, aw, F32, tm, "all"), (n, aw, F32, tm, "all")]

    secs_p = cast_dests([
        (0, aw, lambda c: [("cast", 0, c, scale)]),
        (aw, 2 * aw, lambda c: [("cast", 0, c, None),
                                ("heads", 1, (c - aw) // HEAD_DIM, HEAD_DIM, 2 * D_HEADS, None)]),
        (2 * aw, 3 * aw, lambda c: [("xpose", 3, c - 2 * aw),
                                    ("heads", 2, (c - 2 * aw) // LANES, LANES, D_HEADS, None)]),
    ])
    qkv, d_k_p, d_v_p, vt_p = _norm_proj(
        xp, norm_mix[layer], w, none_aux, secs_p,
        [(n_p, 2 * aw, BF16, ROW_TILE, "all"),
         (n_p * 2 * D_HEADS, HEAD_DIM, F32, ROW_TILE * 2 * D_HEADS, "all"),
         (n_p * D_HEADS, LANES, F32, ROW_TILE * D_HEADS, "all"),
         (aw, n_p, BF16, aw, "cols")], seq_tiles, ROW_TILE)
    qkv_s, d_k_s, d_v_s = _norm_proj(xs, norm_mix[layer], w, none_aux, secs, d_outs(n_s, n_s), 1,
                                     n_s)
    bias_d = _flash_bias(_t5_bias(t5_table, _flash_rel()))
    d_grp = 4
    att_p = _flash_attention(qkv, 0, qkv, aw // (d_grp * LANES), vt_p, 0,
                             bias_d, batch=bp, seq=seq, n_kv_pairs=D_HEADS, n_groups=d_grp,
                             n_pairs=1, mode="diff",
                             lam_vecs=lam_vecs, subln=d_subln.astype(F32), lambda_init=lambda_init)
    k_all, k_in = _pad_keys(cache_d_k, d_k_s.reshape(bs, ts, aw), c_pad, aw)
    v_all, v_in = _pad_keys(cache_d_v, d_v_s.reshape(bs, ts, aw), c_pad, aw)
    att_s = _band_attention(qkv_s, 0, k_in, 0, v_in, 0, bias_s_t5, batch=bs, q_rows=ts,
                            k_rows=c_win, n_kv_pairs=D_HEADS, n_pairs=1, win=c_win, n_past=0,
                            mode="diff", lam_vecs=lam_vecs, subln=d_subln.astype(F32),
                            lambda_init=lambda_init)
    d_k_prompt = d_k_p.reshape(bp, seq, 2 * D_HEADS, HEAD_DIM)
    d_v_prompt = d_v_p.reshape(bp, seq, D_HEADS, 2 * HEAD_DIM)
    d_k_sample = d_k_s.reshape(bs, ts, 2 * D_HEADS, HEAD_DIM)
    d_v_sample = d_v_s.reshape(bs, ts, D_HEADS, 2 * HEAD_DIM)
    w_o = d_w_o.astype(BF16)
    xp, xs, conv_p3, conv_s3 = _ffn_layer(xp, xs, att_p, att_s, w_o, layer, bp, bs, seq_tiles, ts, norm_ffn,
                                          ffn_w_in, ffn_conv_w, ffn_conv_b, ffn_w_out,
                                          state_ffn_conv, norm_final)
    assert depth == 4

    y_prompt = xp.reshape(bp, seq, d)
    y_sample = xs.reshape(bs, ts, d)
    ffn_conv_prompt = jnp.stack([conv_p0, conv_p1, conv_p2, conv_p3])
    ffn_conv_sample = jnp.stack([conv_s0, conv_s1, conv_s2, conv_s3])
    return (y_prompt, y_sample,
            a_k_prompt, a_v_prompt, a_k_sample, a_v_sample,
            b_k_prompt, b_v_prompt, b_k_sample, b_v_sample,
            c_k_prompt, c_v_prompt, c_kidx_prompt, c_k_sample, c_v_sample, c_kidx_sample,
            d_k_prompt, d_v_prompt, d_k_sample, d_v_sample,
            ffn_conv_prompt, ffn_conv_sample)


def _ffn_layer(xp, xs, att_p, att_s, w_o, layer, bp, bs, seq_tiles, ts, norm_ffn, ffn_w_in,
               ffn_conv_w, ffn_conv_b, ffn_w_out, state, final_gain):
    n_ch = D_FF // FF_CHUNK
    w_in = ffn_w_in[layer].astype(BF16)
    w_in = w_in.reshape(D_MODEL, 2, n_ch, FF_CHUNK).transpose(0, 2, 1, 3).reshape(D_MODEL, 2 * D_FF)
    w_out = ffn_w_out[layer].astype(BF16)
    cw, cb, gain = ffn_conv_w[layer], ffn_conv_b[layer], norm_ffn[layer]
    xp, tail = _block_ffn(xp, att_p, w_o, gain, w_in, cw, cb, w_out, tm=ROW_TILE,
                          seq_tiles=seq_tiles, seq_rows=ROW_TILE * seq_tiles,
                          final_gain=final_gain)
    conv_p = tail.reshape(bp, SUBLANES, D_FF)[:, SUBLANES - (CONV_W - 1):, :]
    n_s = bs * ts
    st = state[layer]
    zeros = jnp.zeros((bs, ts - 2, D_FF), F32)
    fix1 = jnp.concatenate([st[:, 1:2], jnp.zeros((bs, 1, D_FF), F32), zeros], axis=1)
    fix2 = jnp.concatenate([st, zeros], axis=1)
    xs, a_s = _block_ffn(xs, att_s, w_o, gain, w_in, cw, cb, w_out, tm=n_s, seq_tiles=1,
                         seq_rows=ts, fixes=(fix1.reshape(n_s, D_FF), fix2.reshape(n_s, D_FF)),
                         final_gain=final_gain)
    conv_s = a_s.reshape(bs, ts, D_FF)[:, ts - (CONV_W - 1):, :]
    return xp, xs, conv_p, conv_s
```

```python
import functools
import math

import jax
import jax.numpy as jnp
from jax import lax
from jax.experimental import pallas as pl
from jax.experimental.pallas import tpu as pltpu

F32 = jnp.float32
BF16 = jnp.bfloat16

D_MODEL = 1024
CHUNK = 64
N_HEADS = 16
HEAD_DIM = 64
ATTN_WIDTH = N_HEADS * HEAD_DIM
NUM_BUCKETS = 32
T5_MAX_DISTANCE = 128
A_PAST = 512
A_CLIP = 64
B_WINDOW = 128
KV_HEADS = 4
C_TOPK = 256
C_IDX_HEADS = 8
C_IDX_DIM = 64
D_HEADS = 8
D_FF = 2816
CONV_W = 3
RMS_EPS = 1e-6
NEG_INF = -1e30
LOG2E = math.log2(math.e)

LANES = 128
SUBLANES = 8
ROW_TILE = 512
FLASH_TILE = 256
BAND_UNROLL = 4
SELECT_ROWS = FLASH_TILE
FF_CHUNK = 256
PROJ_CHUNK = 512
VMEM_LIMIT = 56 * 1024 * 1024

GQA_PERM = tuple(8 * j + 4 * half + t for j in range(2) for t in range(4) for half in range(2))


def _params(n_grid_dims):
    return pltpu.CompilerParams(
        dimension_semantics=("arbitrary",) * n_grid_dims, vmem_limit_bytes=VMEM_LIMIT)


def _resident(shape, index_map):
    return pl.BlockSpec(shape, index_map, pipeline_mode=pl.Buffered(1))


def _rms(x, gain):
    ms = jnp.mean(x * x, axis=-1, keepdims=True)
    return (x * lax.rsqrt(ms + RMS_EPS)) * gain


def _dot_nt(a, b):
    return lax.dot_general(a, b, (((1,), (1,)), ((), ())), preferred_element_type=F32)


def _norm_proj_body(sections, seq_tiles, n_out, x_ref, g_ref, w_ref, aux_ref, *refs):
    out_refs, scratch_refs = refs[:n_out], refs[n_out:]
    tm = x_ref.shape[0]
    h = _rms(x_ref[...], g_ref[...]).astype(BF16)
    tail_heads = []
    for col0, width, dests in sections:
        y = jnp.dot(h, w_ref[:, col0:col0 + width], preferred_element_type=F32)
        for dest in dests:
            kind = dest[0]
            if kind == "cast":
                _, o, oc, scale = dest
                val = y if scale is None else y * scale
                out_refs[o][:, oc:oc + width] = val.astype(out_refs[o].dtype)
            elif kind == "tail":
                _, o, oc, nrows = dest
                out_refs[o][:, oc:oc + width] = y[tm - nrows:, :]
            elif kind == "tail_heads":
                _, o, oc, nrows, dh, nh = dest
                scratch_refs[0][:, oc:oc + width] = y[tm - nrows:, :]
                tail_heads.append((o, oc, width, nrows, dh, nh))
            elif kind == "xpose":
                _, o, orow = dest
                out_refs[o][orow:orow + width, :] = y.T.astype(BF16)
            elif kind == "heads":
                _, o, head0, dh, nh, nrows = dest
                src = y if nrows is None else y[tm - nrows:, :]
                for hh in range(width // dh):
                    out_refs[o][pl.ds(head0 + hh, src.shape[0], stride=nh), :] = (
                        src[:, hh * dh:(hh + 1) * dh])
            elif kind == "kidx":
                _, o_dup, o_f32 = dest
                yn = _rms(y, aux_ref[...])
                out_refs[o_dup][...] = yn.astype(BF16)
                out_refs[o_f32][...] = yn[:, :C_IDX_DIM]
            else:
                raise ValueError(kind)
    if tail_heads:
        @pl.when((pl.program_id(0) % seq_tiles) == seq_tiles - 1)
        def _():
            for o, oc, width, nrows, dh, nh in tail_heads:
                for hh in range(width // dh):
                    c0 = oc + hh * dh
                    out_refs[o][pl.ds((c0 % (nh * dh)) // dh, nrows, stride=nh), :] = (
                        scratch_refs[0][:, c0:c0 + dh])


def _norm_proj(x, gain, w, aux, sections, outs, seq_tiles, tm, tail_scratch=None):
    n = x.shape[0]
    grid = (n // tm,)
    out_shape, out_specs = [], []
    for rows, width, dtype, blk_rows, mode in outs:
        out_shape.append(jax.ShapeDtypeStruct((rows, width), dtype))
        if mode == "all":
            out_specs.append(pl.BlockSpec((blk_rows, width), lambda i: (i, 0)))
        elif mode == "cols":
            out_specs.append(pl.BlockSpec((rows, tm), lambda i: (0, i)))
        else:
            out_specs.append(pl.BlockSpec((blk_rows, width), lambda i: (i // seq_tiles, 0)))
    body = functools.partial(_norm_proj_body, sections, seq_tiles, len(outs))
    return pl.pallas_call(
        body,
        grid=grid,
        in_specs=[
            pl.BlockSpec((tm, D_MODEL), lambda i: (i, 0)),
            _resident((1, D_MODEL), lambda i: (0, 0)),
            _resident(w.shape, lambda i: (0, 0)),
            _resident(aux.shape, lambda i: (0, 0)),
        ],
        out_specs=out_specs,
        out_shape=out_shape,
        scratch_shapes=[pltpu.VMEM(tail_scratch, F32)] if tail_scratch else [],
        compiler_params=_params(1),
        name="norm_proj",
    )(x, gain.reshape(1, D_MODEL), w, aux)


def _softmax_rows(s, sink, want_sum):
    m = jnp.max(s, axis=-1, keepdims=True)
    if sink is not None:
        m = jnp.maximum(m, sink)
    p = jnp.exp2(s - m)
    l = jnp.exp2(sink - m) if sink is not None else None
    if want_sum:
        total = jnp.sum(p, axis=-1, keepdims=True)
        l = total if l is None else total + l
    return p, l


def _band_body(cfg, *refs):
    n_pairs, win, n_past, n_chunks, unroll, mode, lambda_init, has_sel = cfg
    it = iter(refs)
    q_ref, k_ref, v_ref, bias_ref = next(it), next(it), next(it), next(it)
    sel_ref = next(it) if has_sel else None
    sink_ref = next(it) if mode == "sink" else None
    if mode == "diff":
        lam_ref, subln_ref = next(it), next(it)
    o_ref = next(it)
    if n_past:
        kp_ref, vp_ref = next(it), next(it)
        kp_ref[0:n_past, :] = jnp.zeros((n_past, LANES), BF16)
        vp_ref[0:n_past, :] = jnp.zeros((n_past, LANES), BF16)
        kp_ref[n_past:, :] = k_ref[...]
        vp_ref[n_past:, :] = v_ref[...]
    else:
        kp_ref, vp_ref = k_ref, v_ref

    n_heads = 2 * n_pairs
    rows = n_heads * CHUNK
    lane = lax.broadcasted_iota(jnp.int32, (CHUNK, LANES), 1)
    col = lax.broadcasted_iota(jnp.int32, (rows, win), 1)
    low = lane < HEAD_DIM
    if mode == "diff":
        lv = lam_ref[...]
        lam = (jnp.exp(jnp.sum(lv[0:1] * lv[1:2], axis=-1, keepdims=True))
               - jnp.exp(jnp.sum(lv[2:3] * lv[3:4], axis=-1, keepdims=True)) + lambda_init)

    def logits(c):
        r0 = pl.multiple_of(c * CHUNK, CHUNK)
        kwin = kp_ref[pl.ds(r0, win), :]
        qs = []
        for p in range(n_pairs):
            qp = q_ref[pl.ds(r0, CHUNK), p * LANES:(p + 1) * LANES]
            qs += [jnp.where(low, qp, jnp.zeros_like(qp)), jnp.where(low, jnp.zeros_like(qp), qp)]
        s = _dot_nt(jnp.concatenate(qs, axis=0), kwin) + bias_ref[...].reshape(rows, win)
        if n_past:
            s = jnp.where(col >= n_past - c * CHUNK, s, NEG_INF)
        if has_sel:
            s = (s.reshape(n_heads, CHUNK, win) + sel_ref[...].astype(F32)[None]).reshape(rows, win)
        return s

    def attend(c, pr, l):
        r0 = pl.multiple_of(c * CHUNK, CHUNK)
        vwin = vp_ref[pl.ds(r0, win), :]
        if mode == "diff":
            pn = pr / l
            a = (pn[0:CHUNK] - lam * pn[CHUNK:2 * CHUNK]).astype(BF16)
            o = jnp.dot(a, vwin, preferred_element_type=F32)
            o_ref[pl.ds(r0, CHUNK), :] = (_rms(o, subln_ref[...]) * (1.0 - lambda_init)).astype(BF16)
            return
        v_ones = jnp.concatenate([vwin, jnp.ones((win, LANES), BF16)], axis=1)
        ol = jnp.dot(pr.astype(BF16), v_ones, preferred_element_type=F32)
        denom = ol[:, LANES:] if l is None else ol[:, LANES:] + l
        o = ol[:, :LANES] / denom
        for p in range(n_pairs):
            lo = o[2 * p * CHUNK:(2 * p + 1) * CHUNK]
            hi = o[(2 * p + 1) * CHUNK:(2 * p + 2) * CHUNK]
            o_ref[pl.ds(r0, CHUNK), p * LANES:(p + 1) * LANES] = jnp.where(low, lo, hi).astype(BF16)

    sink = sink_ref[...] if mode == "sink" else None

    def chunks(cc, carry):
        ids = [cc * unroll + u for u in range(unroll)]
        ss = [logits(c) for c in ids]
        pls = [_softmax_rows(s, sink, mode == "diff") for s in ss]
        for c, (pr, l) in zip(ids, pls):
            attend(c, pr, l)
        return carry

    if n_chunks == unroll:
        chunks(0, 0)
    else:
        lax.fori_loop(0, n_chunks // unroll, chunks, 0)


def _band_attention(q_arr, q_blk0, k_arr, k_blk0, v_arr, v_blk0, bias, *, batch, q_rows, k_rows,
                    n_kv_pairs, n_pairs, win, n_past, mode="plain", sel=None, sinks=None,
                    lam_vecs=None, subln=None, lambda_init=0.0, chunks_in_flight=BAND_UNROLL):
    n_chunks = q_rows // CHUNK
    qw = n_pairs * LANES
    rows = 2 * n_pairs * CHUNK
    unroll = math.gcd(n_chunks, chunks_in_flight)
    cfg = (n_pairs, win, n_past, n_chunks, unroll, mode, lambda_init, sel is not None)
    in_specs = [
        pl.BlockSpec((q_rows, qw), lambda b, j: (b, q_blk0 + j)),
        pl.BlockSpec((k_rows, LANES), lambda b, j: (b, k_blk0 + j)),
        pl.BlockSpec((k_rows, LANES), lambda b, j: (b, v_blk0 + j)),
        pl.BlockSpec((2 * n_pairs, CHUNK, win), lambda b, j: (j, 0, 0)),
    ]
    args = [q_arr, k_arr, v_arr, bias]
    if sel is not None:
        in_specs.append(pl.BlockSpec((None, CHUNK, win), lambda b, j: (b, 0, 0)))
        args.append(sel)
    if mode == "sink":
        in_specs.append(pl.BlockSpec((rows, 1), lambda b, j: (j, 0)))
        args.append(jnp.repeat(sinks, CHUNK).reshape(n_kv_pairs * rows, 1))
    if mode == "diff":
        in_specs.append(pl.BlockSpec((4, HEAD_DIM), lambda b, j: (0, 0)))
        in_specs.append(pl.BlockSpec((1, LANES), lambda b, j: (0, 0)))
        args += [lam_vecs, subln.reshape(1, LANES)]
    scratch = []
    if n_past:
        scratch = [pltpu.VMEM((n_past + k_rows, LANES), BF16)] * 2
    return pl.pallas_call(
        functools.partial(_band_body, cfg),
        grid=(batch, n_kv_pairs),
        in_specs=in_specs,
        out_specs=pl.BlockSpec((q_rows, qw), lambda b, j: (b, j)),
        out_shape=jax.ShapeDtypeStruct((batch * q_rows, n_kv_pairs * qw), BF16),
        scratch_shapes=scratch,
        compiler_params=_params(2),
        name="band_attention",
    )(*args)


def _flash_t_body(cfg, *refs):
    n_groups, n_pairs, mode, lambda_init, has_sel = cfg
    it = iter(refs)
    q_ref, k_ref, vt_in_ref, bias_ref = next(it), next(it), next(it), next(it)
    sel_ref = next(it) if has_sel else None
    if mode == "diff":
        lam_ref, subln_ref = next(it), next(it)
    o_ref = next(it)
    qs_ref, m_ref, acc_ref = next(it), next(it), next(it)
    sa_ref, sb_ref, pa_ref, pb_ref, ala_ref, alb_ref = (next(it), next(it), next(it), next(it),
                                                        next(it), next(it))
    t = FLASH_TILE
    n_heads = 2 * n_pairs
    rows = n_heads * t
    i = pl.program_id(2)
    lane = lax.broadcasted_iota(jnp.int32, (t, LANES), 1)
    low = lane < HEAD_DIM

    for g in range(n_groups):
        qs = []
        for p in range(n_pairs):
            qp = q_ref[:, (g * n_pairs + p) * LANES:(g * n_pairs + p + 1) * LANES]
            qs += [jnp.where(low, qp, jnp.zeros_like(qp)), jnp.where(low, jnp.zeros_like(qp), qp)]
        qs_ref[g] = jnp.concatenate(qs, axis=0)
    m_ref[...] = jnp.full(m_ref.shape, NEG_INF, F32)
    acc_ref[...] = jnp.zeros(acc_ref.shape, F32)

    def logits_into(s_ref, jt):
        jc = jnp.minimum(jt, i)
        kind = jnp.where(jt > i, 3, jnp.minimum(i - jt, 2))
        r0 = pl.multiple_of(jc * t, t)
        for g in range(n_groups):
            kb = k_ref[pl.ds(r0, t), g * LANES:(g + 1) * LANES]
            s = _dot_nt(kb, qs_ref[g]) + bias_ref[kind, :, g * rows:(g + 1) * rows]
            if has_sel:
                s = s + jnp.concatenate([sel_ref[jc].astype(F32)] * n_heads, axis=1)
            s_ref[g] = s

    ones_rows = jnp.ones((FLASH_SUM_ROWS, t), BF16)

    def attend(p_ref, al_ref, jt):
        r0 = pl.multiple_of(jnp.clip(jt, 0, i) * t, t)
        for g in range(n_groups):
            vt = jnp.concatenate([vt_in_ref[g * LANES:(g + 1) * LANES, pl.ds(r0, t)], ones_rows],
                                 axis=0)
            pv = jnp.dot(vt, p_ref[g], preferred_element_type=F32)
            acc_ref[g] = al_ref[g] * acc_ref[g] + pv

    def softmax_from(s_ref, p_ref, al_ref):
        for g in range(n_groups):
            s = s_ref[g]
            m_old = m_ref[g]
            m_new = jnp.maximum(m_old, jnp.max(s, axis=0, keepdims=True))
            m_ref[g] = m_new
            p_ref[g] = jnp.exp2(s - m_new).astype(BF16)
            al_ref[g] = jnp.exp2(m_old - m_new)

    logits_into(sa_ref, 0)
    pb_ref[...] = jnp.zeros(pb_ref.shape, BF16)
    alb_ref[...] = jnp.ones(alb_ref.shape, F32)

    def two_tiles(u, carry):
        k0 = 2 * u
        logits_into(sb_ref, k0 + 1)
        attend(pb_ref, alb_ref, k0 - 1)
        softmax_from(sa_ref, pa_ref, ala_ref)
        logits_into(sa_ref, k0 + 2)
        attend(pa_ref, ala_ref, k0)
        softmax_from(sb_ref, pb_ref, alb_ref)
        return carry

    n_trips = lax.shift_right_logical(i + 2, 1)
    lax.fori_loop(0, n_trips, two_tiles, 0)
    attend(pb_ref, alb_ref, 2 * n_trips - 1)

    if mode == "diff":
        lv = lam_ref[...]
        lam = (jnp.exp(jnp.sum(lv[0:1] * lv[1:2], axis=-1, keepdims=True))
               - jnp.exp(jnp.sum(lv[2:3] * lv[3:4], axis=-1, keepdims=True)) + lambda_init)
    drow_low = lax.broadcasted_iota(jnp.int32, (LANES, t), 0) < HEAD_DIM
    for g in range(n_groups):
        acc = acc_ref[g]
        res = acc[:LANES] / acc[LANES:LANES + 1]
        for p in range(n_pairs):
            lo = res[:, 2 * p * t:(2 * p + 1) * t]
            hi = res[:, (2 * p + 1) * t:(2 * p + 2) * t]
            if mode == "diff":
                d = lo - lam * hi
                ms = jnp.mean(d * d, axis=0, keepdims=True)
                o = ((d * lax.rsqrt(ms + RMS_EPS)) * subln_ref[...] * (1.0 - lambda_init)).T
            else:
                o = jnp.where(drow_low, lo, hi).T
            o_ref[:, (g * n_pairs + p) * LANES:(g * n_pairs + p + 1) * LANES] = o.astype(BF16)


def _flash_attention(q_arr, q_blk0, k_arr, k_blk0, vt_arr, v_blk0, bias, *, batch, seq, n_kv_pairs,
                     n_groups, n_pairs, mode="plain", sel=None, lam_vecs=None, subln=None,
                     lambda_init=0.0):
    t = FLASH_TILE
    nq = seq // t
    qw = n_groups * n_pairs * LANES
    kw = n_groups * LANES
    rows = 2 * n_pairs * t
    n_steps = n_kv_pairs // n_groups
    cfg = (n_groups, n_pairs, mode, lambda_init, sel is not None)
    in_specs = [
        pl.BlockSpec((t, qw), lambda b, j, i: (b * nq + i, q_blk0 + j)),
        pl.BlockSpec((seq, kw), lambda b, j, i: (b, k_blk0 + j)),
        pl.BlockSpec((kw, seq), lambda b, j, i: (v_blk0 + j, b)),
        pl.BlockSpec((4, t, n_groups * rows), lambda b, j, i: (0, 0, j)),
    ]
    args = [q_arr, k_arr, vt_arr, bias]
    if sel is not None:
        in_specs.append(pl.BlockSpec((None, nq, t, t), lambda b, j, i: (b * nq + i, 0, 0, 0)))
        args.append(sel)
    if mode == "diff":
        in_specs.append(pl.BlockSpec((4, HEAD_DIM), lambda b, j, i: (0, 0)))
        in_specs.append(pl.BlockSpec((LANES, t), lambda b, j, i: (0, 0)))
        args += [lam_vecs, jnp.broadcast_to(subln.reshape(LANES, 1), (LANES, t))]
    return pl.pallas_call(
        functools.partial(_flash_t_body, cfg),
        grid=(batch, n_steps, nq),
        in_specs=in_specs,
        out_specs=pl.BlockSpec((t, qw), lambda b, j, i: (b * nq + i, j)),
        out_shape=jax.ShapeDtypeStruct((batch * seq, n_steps * qw), BF16),
        scratch_shapes=[pltpu.VMEM((n_groups, rows, LANES), BF16),
                        pltpu.VMEM((n_groups, 1, rows), F32),
                        pltpu.VMEM((n_groups, LANES + FLASH_SUM_ROWS, rows), F32),
                        pltpu.VMEM((n_groups, t, rows), F32),
                        pltpu.VMEM((n_groups, t, rows), F32),
                        pltpu.VMEM((n_groups, t, rows), BF16),
                        pltpu.VMEM((n_groups, t, rows), BF16),
                        pltpu.VMEM((n_groups, 1, rows), F32),
                        pltpu.VMEM((n_groups, 1, rows), F32)],
        compiler_params=_params(3),
        name="flash_attention",
    )(*args)


def _select_t_body(cfg, qi_ref, wi_ref, ki_ref, o_ref, key_ref, qs_ref, cut_ref):
    n_keys, kpos0, qpos0, topk = cfg
    r = qi_ref.shape[0]
    t = FLASH_TILE
    n_tiles = n_keys // t
    qrow0 = qpos0 + pl.program_id(1) * r
    last_kpos = (lax.shift_right_arithmetic(qrow0 + r - 1, 6) + 1) * CHUNK - 1
    n_adm = jnp.minimum(n_tiles,
                        lax.shift_right_arithmetic(last_kpos - kpos0, t.bit_length() - 1) + 1)
    lane = lax.broadcasted_iota(jnp.int32, (r, LANES), 1)
    low = lane < C_IDX_DIM
    wit = wi_ref[...].T
    qchunk = lax.shift_right_arithmetic(qrow0 + lax.broadcasted_iota(jnp.int32, (t, r), 1), 6)
    krow = lax.broadcasted_iota(jnp.int32, (t, r), 0)

    def admissible(kb):
        kpos = kpos0 + kb * t + krow
        return jnp.logical_and(kpos >= 0, lax.shift_right_arithmetic(kpos, 6) <= qchunk)

    qs = []
    for h in range(C_IDX_HEADS):
        qp = qi_ref[:, (h // 2) * LANES:(h // 2 + 1) * LANES]
        zero = jnp.zeros_like(qp)
        qs.append(jnp.where(low, qp, zero) if h % 2 == 0 else jnp.where(low, zero, qp))
    qs_ref[...] = jnp.concatenate(qs, axis=0)

    def score_tile(kb, carry):
        r0 = pl.multiple_of(kb * t, t)
        dots = jnp.maximum(_dot_nt(ki_ref[pl.ds(r0, t), :], qs_ref[...]), 0.0)
        score = jnp.zeros((t, r), F32)
        for h in range(C_IDX_HEADS):
            score = score + wit[h:h + 1, :] * dots[:, h * r:(h + 1) * r]
        score = jnp.where(admissible(kb), score, NEG_INF)
        bits = pltpu.bitcast(score, jnp.int32)
        key_ref[kb] = jnp.where(bits >= 0, bits, bits ^ jnp.int32(0x7FFFFFFF))
        return carry

    lax.fori_loop(0, n_adm, score_tile, 0)
    key_ref[n_adm] = jnp.full((t, r), jnp.int32(-2 ** 31), jnp.int32)

    def count(pred):
        def two_tiles(u, acc):
            for kb in (2 * u, 2 * u + 1):
                hit = jnp.where(pred(key_ref[kb], kb), 1.0, 0.0)
                acc = acc + jnp.sum(hit.reshape(t // SUBLANES, SUBLANES, r), axis=0)
            return acc
        acc = lax.fori_loop(0, lax.shift_right_logical(n_adm + 1, 1), two_tiles,
                            jnp.zeros((SUBLANES, r), F32))
        return jnp.sum(acc, axis=0, keepdims=True)

    thr = jnp.full((1, r), jnp.int32(-2 ** 31), jnp.int32)
    cand0 = jnp.zeros((1, r), jnp.int32)
    thr = jnp.where(count(lambda k, kb: k >= cand0) >= topk, cand0, thr)

    def value_bit(it, thr):
        cand = thr | lax.shift_left(jnp.int32(1), 30 - it)
        return jnp.where(count(lambda k, kb: k >= cand) >= topk, cand, thr)

    thr = lax.fori_loop(0, 31, value_bit, thr)

    n_ge = count(lambda k, kb: k >= thr)
    cut_ref[...] = jnp.full((1, r), n_keys, jnp.int32)
    n_bits = max(1, (n_keys - 1).bit_length())

    @pl.when(jnp.max(n_ge) > topk)
    def _():
        ties_wanted = topk - count(lambda k, kb: k > thr)

        def index_bit(it, cut):
            cand = cut | lax.shift_left(jnp.int32(1), n_bits - 1 - it)
            before = count(lambda k, kb: jnp.logical_and(k == thr, kb * t + krow < cand))
            return jnp.where(before <= ties_wanted - 1.0, cand, cut)

        cut_ref[...] = lax.fori_loop(0, n_bits, index_bit, jnp.zeros((1, r), jnp.int32))

    cut = cut_ref[...]
    for kb in range(n_tiles):
        @pl.when(kb < n_adm)
        def _():
            kk = key_ref[kb]
            chosen = jnp.logical_or(kk > thr, jnp.logical_and(kk == thr, kb * t + krow <= cut))
            valid = jnp.logical_and(chosen, admissible(kb))
            o_ref[kb] = jnp.where(valid, 0.0, NEG_INF).astype(BF16)

        @pl.when(kb >= n_adm)
        def _():
            o_ref[kb] = jnp.full((t, r), NEG_INF, BF16)


def _select_mask(qi_arr, qi_blk, wi_arr, ki_arr, *, batch, q_rows, n_keys, kpos0, qpos0, topk):
    r = min(SELECT_ROWS, q_rows)
    t = FLASH_TILE
    nq = q_rows // r
    cfg = (n_keys, kpos0, qpos0, topk)
    return pl.pallas_call(
        functools.partial(_select_t_body, cfg),
        grid=(batch, nq),
        in_specs=[
            pl.BlockSpec((r, C_IDX_HEADS * C_IDX_DIM), lambda b, i: (b * nq + i, qi_blk)),
            pl.BlockSpec((r, LANES), lambda b, i: (b * nq + i, 0)),
            pl.BlockSpec((n_keys, LANES), lambda b, i: (b, 0)),
        ],
        out_specs=pl.BlockSpec((None, n_keys // t, t, r), lambda b, i: (b * nq + i, 0, 0, 0)),
        out_shape=jax.ShapeDtypeStruct((batch * nq, n_keys // t, t, r), BF16),
        scratch_shapes=[pltpu.VMEM((n_keys // t + 1, t, r), jnp.int32),
                        pltpu.VMEM((C_IDX_HEADS * r, LANES), BF16),
                        pltpu.VMEM((1, r), jnp.int32)],
        compiler_params=_params(2),
        name="select_mask",
    )(qi_arr, wi_arr, ki_arr)


FLASH_SUM_ROWS = 16
FFN_PAD = 16


def _block_ffn_body(cfg, *refs):
    seq_tiles, seq_rows, stream, final_norm = cfg
    it = iter(refs)
    x_ref, att_ref = next(it), next(it)
    if stream:
        xprev_ref, attprev_ref = next(it), next(it)
    else:
        fix1_ref, fix2_ref = next(it), next(it)
    wo_ref, g_ref, win_ref, cw_ref, cb_ref, wout_ref = (next(it), next(it), next(it), next(it),
                                                        next(it), next(it))
    gfin_ref = next(it) if final_norm else None
    o_ref, st_ref = next(it), next(it)
    hext_ref, act_ref = next(it), next(it)
    tm = x_ref.shape[0]
    pad = FFN_PAD
    gain = g_ref[...]
    if stream:
        aext_ref = next(it)
        aext_ref[0:pad, :] = attprev_ref[...]
        aext_ref[pad:, :] = att_ref[...]
        xext = jnp.concatenate([xprev_ref[...], x_ref[...]], axis=0)
        x1 = xext + jnp.dot(aext_ref[...], wo_ref[...], preferred_element_type=F32)
        o_ref[...] = x1[pad:, :]
        not_start = (pl.program_id(0) % seq_tiles) != 0
        rowi = lax.broadcasted_iota(jnp.int32, (tm + pad, 1), 0)
        keep = jnp.logical_or(rowi >= pad, not_start)
        hext_ref[...] = jnp.where(keep, _rms(x1, gain), 0.0).astype(BF16)
    else:
        x1 = x_ref[...] + jnp.dot(att_ref[...], wo_ref[...], preferred_element_type=F32)
        o_ref[...] = x1
        hext_ref[pad:, :] = _rms(x1, gain).astype(BF16)
        hext_ref[0:pad, :] = jnp.zeros((pad, D_MODEL), BF16)
        assert seq_rows & (seq_rows - 1) == 0
        rmod = lax.broadcasted_iota(jnp.int32, (tm, FF_CHUNK), 0) & (seq_rows - 1)
    hext = hext_ref[...]
    for c0 in range(0, D_FF, FF_CHUNK):
        w_ag = jnp.concatenate([win_ref[:, c0:c0 + FF_CHUNK],
                                win_ref[:, D_FF + c0:D_FF + c0 + FF_CHUNK]], axis=1)
        ag = jnp.dot(hext, w_ag, preferred_element_type=F32)
        a_ext = ag[:, :FF_CHUNK]
        gate = ag[pad:, FF_CHUNK:]
        a = a_ext[pad:, :]
        a1 = pltpu.roll(a_ext, 1, 0)[pad:, :]
        a2 = pltpu.roll(a_ext, 2, 0)[pad:, :]
        if not stream:
            a1 = jnp.where(rmod == 0, fix1_ref[:, c0:c0 + FF_CHUNK], a1)
            a2 = jnp.where(rmod <= 1, fix2_ref[:, c0:c0 + FF_CHUNK], a2)
        cw = cw_ref[:, c0:c0 + FF_CHUNK]
        u = cb_ref[:, c0:c0 + FF_CHUNK] + (cw[0:1] * a2 + cw[1:2] * a1 + cw[2:3] * a)
        act_ref[:, c0:c0 + FF_CHUNK] = (jax.nn.silu(u) * gate).astype(BF16)
        st_ref[:, c0:c0 + FF_CHUNK] = a[tm - SUBLANES:, :] if stream else a
    y = o_ref[...] + jnp.dot(act_ref[...], wout_ref[...], preferred_element_type=F32)
    if final_norm:
        y = _rms(y, gfin_ref[...])
    o_ref[...] = y


def _block_ffn(x, attn, w_o, gain, w_in, conv_w, conv_b, w_out, *, tm, seq_tiles, seq_rows,
               fixes=None, final_gain=None):
    n = x.shape[0]
    stream = fixes is None
    final_norm = final_gain is not None
    cfg = (seq_tiles, seq_rows, stream, final_norm)
    in_specs = [pl.BlockSpec((tm, D_MODEL), lambda i: (i, 0)),
                pl.BlockSpec((tm, ATTN_WIDTH), lambda i: (i, 0))]
    args = [x, attn]
    if stream:
        per = tm // FFN_PAD

        def prev(i):
            return (jnp.maximum(i * per - 1, 0), 0)
        in_specs += [pl.BlockSpec((FFN_PAD, D_MODEL), prev), pl.BlockSpec((FFN_PAD, ATTN_WIDTH), prev)]
        args += [x, attn]
    else:
        in_specs += [pl.BlockSpec((tm, D_FF), lambda i: (i, 0))] * 2
        args += list(fixes)
    in_specs += [
        _resident((ATTN_WIDTH, D_MODEL), lambda i: (0, 0)),
        _resident((1, D_MODEL), lambda i: (0, 0)),
        _resident((D_MODEL, 2 * D_FF), lambda i: (0, 0)),
        _resident((CONV_W, D_FF), lambda i: (0, 0)),
        _resident((1, D_FF), lambda i: (0, 0)),
        _resident((D_FF, D_MODEL), lambda i: (0, 0)),
    ]
    args += [w_o, gain.reshape(1, D_MODEL), w_in, conv_w, conv_b.reshape(1, D_FF), w_out]
    if final_norm:
        in_specs.append(_resident((1, D_MODEL), lambda i: (0, 0)))
        args.append(final_gain.reshape(1, D_MODEL))
    if stream:
        st_shape = jax.ShapeDtypeStruct((n // (tm * seq_tiles) * SUBLANES, D_FF), F32)
        st_spec = pl.BlockSpec((SUBLANES, D_FF), lambda i: (i // seq_tiles, 0))
    else:
        st_shape = jax.ShapeDtypeStruct((n, D_FF), F32)
        st_spec = pl.BlockSpec((tm, D_FF), lambda i: (i, 0))
    return pl.pallas_call(
        functools.partial(_block_ffn_body, cfg),
        grid=(n // tm,),
        in_specs=in_specs,
        out_specs=[pl.BlockSpec((tm, D_MODEL), lambda i: (i, 0)), st_spec],
        out_shape=[jax.ShapeDtypeStruct((n, D_MODEL), F32), st_shape],
        scratch_shapes=[pltpu.VMEM((tm + FFN_PAD, D_MODEL), BF16), pltpu.VMEM((tm, D_FF), BF16)]
        + ([pltpu.VMEM((tm + FFN_PAD, ATTN_WIDTH), BF16)] if stream else []),
        compiler_params=_params(1),
        name="block_ffn",
    )(*args)


def _t5_bucket(rel):
    half = NUM_BUCKETS // 2
    max_exact = half // 2
    base = jnp.where(rel > 0, half, 0)
    n = jnp.abs(rel)
    nf = jnp.maximum(n, 1).astype(F32)
    large = max_exact + (jnp.log(nf / max_exact) / math.log(T5_MAX_DISTANCE / max_exact)
                         * (half - max_exact)).astype(jnp.int32)
    large = jnp.minimum(large, half - 1)
    return base + jnp.where(n < max_exact, n, large)


def _lookup_heads(table, idx):
    onehot = (idx[..., None] == jnp.arange(table.shape[0])).astype(F32)
    return jnp.einsum("...n,nh->h...", onehot, table.astype(F32), precision=lax.Precision.HIGHEST)


def _t5_bias(table, rel):
    return _lookup_heads(table, _t5_bucket(rel))


def _band_rel(win, n_past_eff):
    i = jnp.arange(CHUNK)[:, None]
    j = jnp.arange(win)[None, :]
    return (j - n_past_eff) - i


def _flash_rel():
    t = FLASH_TILE
    r = jnp.arange(t)[:, None]
    c = jnp.arange(t)[None, :]
    return jnp.stack([c - r - d * t for d in range(3)])


def _flash_bias(bias):
    h, kinds, tq, tk = bias.shape
    ok = (jnp.arange(tk)[None, :] // CHUNK) <= (jnp.arange(tq)[:, None] // CHUNK)
    bias = bias.at[:, 0].set(jnp.where(ok[None], bias[:, 0], NEG_INF))
    bias = jnp.concatenate([bias, jnp.full((h, 1, tq, tk), NEG_INF, F32)], axis=1)
    return bias.transpose(1, 3, 0, 2).reshape(kinds + 1, tk, h * tq)


def _perm_cols(w, perm):
    return w.reshape(w.shape[0], len(perm), HEAD_DIM)[:, perm, :].reshape(w.shape[0], -1)


def _perm_rows(w, perm):
    return w.reshape(len(perm), HEAD_DIM, w.shape[1])[perm, :, :].reshape(-1, w.shape[1])


def _pad_keys(cache, new, pad, width):
    b = cache.shape[0]
    allk = jnp.concatenate([cache.reshape(b, -1, width), new.reshape(b, -1, width)], axis=1)
    padded = jnp.pad(allk, ((0, 0), (pad, 0), (0, 0))).astype(BF16)
    return allk, padded.reshape(-1, width)


def kernel(x_prompt, x_sample, cache_a_k, cache_a_v, cache_b_k, cache_b_v, cache_c_k, cache_c_v,
           cache_c_kidx, cache_d_k, cache_d_v, state_ffn_conv, t5_table, norm_mix, norm_ffn,
           norm_final, a_w_qkv, a_w_o, a_rel_bias, b_w_qkv, b_w_o, b_sinks, c_w_qkv, c_w_o,
           c_w_idx_q, c_w_idx_k, c_idx_k_norm, c_w_idx_w, d_w_qkv, d_w_o, d_lambda_q1,
           d_lambda_k1, d_lambda_q2, d_lambda_k2, d_subln, ffn_w_in, ffn_conv_w, ffn_conv_b,
           ffn_w_out):
    bp, seq, d = x_prompt.shape
    bs, ts, _ = x_sample.shape
    past = cache_c_k.shape[1]
    assert d == D_MODEL and ts == CHUNK and seq % ROW_TILE == 0 and past % CHUNK == 0
    n_p, n_s = bp * seq, bs * ts
    seq_tiles = seq // ROW_TILE
    depth = norm_mix.shape[0]
    scale = HEAD_DIM ** -0.5 * LOG2E
    t5_table = t5_table.astype(F32) * LOG2E
    perm = jnp.array(GQA_PERM)
    none_aux = jnp.zeros((1, LANES), F32)
    kvw = KV_HEADS * HEAD_DIM

    xp = x_prompt.reshape(n_p, d)
    xs = x_sample.reshape(n_s, d)

    def cast_dests(col_dests):
        secs = []
        for c0, c1, fn in col_dests:
            for c in range(c0, c1, PROJ_CHUNK):
                secs.append((c, min(PROJ_CHUNK, c1 - c), fn(c)))
        return secs

    layer = 0
    w = a_w_qkv.astype(BF16)
    aw = ATTN_WIDTH
    a_keep = min(A_PAST, seq)
    secs_p = cast_dests([
        (0, aw, lambda c: [("cast", 0, c, scale)]),
        (aw, 2 * aw, lambda c: [("cast", 0, c, None),
                                ("tail_heads", 1, c - aw, a_keep, HEAD_DIM, N_HEADS)]),
        (2 * aw, 3 * aw, lambda c: [("cast", 0, c, None),
                                    ("tail_heads", 2, c - aw, a_keep, HEAD_DIM, N_HEADS)]),
    ])
    a_cache_out = (bp * a_keep * N_HEADS, HEAD_DIM, F32, a_keep * N_HEADS, "tail")
    qkv, a_k_p, a_v_p = _norm_proj(
        xp, norm_mix[layer], w, none_aux, secs_p,
        [(n_p, 3 * aw, BF16, ROW_TILE, "all"), a_cache_out, a_cache_out], seq_tiles, ROW_TILE,
        tail_scratch=(a_keep, 2 * aw))
    secs_s = cast_dests([
        (0, aw, lambda c: [("cast", 0, c, scale)]),
        (aw, 3 * aw, lambda c: [("cast", 1, c - aw, None)]),
    ])
    q_s, kv_s = _norm_proj(xs, norm_mix[layer], w, none_aux, secs_s,
                           [(n_s, aw, BF16, n_s, "all"), (n_s, 2 * aw, F32, n_s, "all")], 1, n_s)
    a_pad = CHUNK
    a_win = A_PAST + CHUNK + a_pad
    rel = _band_rel(a_win, A_PAST + a_pad)
    bias_a = _lookup_heads(a_rel_bias.astype(F32) * LOG2E,jnp.clip(rel, -A_CLIP, A_CLIP) + A_CLIP)
    bias_a = jnp.where((jnp.arange(a_win) >= a_pad)[None, None, :], bias_a, NEG_INF)
    att_p = _band_attention(qkv, 0, qkv, aw // LANES, qkv, 2 * aw // LANES, bias_a, batch=bp,
                            q_rows=seq, k_rows=seq, n_kv_pairs=N_HEADS // 2, n_pairs=1,
                            win=a_win, n_past=A_PAST + a_pad, chunks_in_flight=8)
    ks = kv_s[:, :aw].reshape(bs, ts, aw)
    vs = kv_s[:, aw:].reshape(bs, ts, aw)
    k_all, k_in = _pad_keys(cache_a_k, ks, a_pad, aw)
    v_all, v_in = _pad_keys(cache_a_v, vs, a_pad, aw)
    assert k_all.shape[1] + a_pad == a_win
    att_s = _band_attention(q_s, 0, k_in, 0, v_in, 0, bias_a, batch=bs, q_rows=ts, k_rows=a_win,
                            n_kv_pairs=N_HEADS // 2, n_pairs=1, win=a_win, n_past=0)
    a_keep_s = min(A_PAST, k_all.shape[1])
    a_k_prompt = a_k_p.reshape(bp, a_keep, N_HEADS, HEAD_DIM)
    a_v_prompt = a_v_p.reshape(bp, a_keep, N_HEADS, HEAD_DIM)
    a_k_sample = k_all[:, -a_keep_s:].reshape(bs, a_keep_s, N_HEADS, HEAD_DIM)
    a_v_sample = v_all[:, -a_keep_s:].reshape(bs, a_keep_s, N_HEADS, HEAD_DIM)
    w_o = a_w_o.astype(BF16)
    xp, xs, conv_p0, conv_s0 = _ffn_layer(xp, xs, att_p, att_s, w_o, layer, bp, bs, seq_tiles, ts, norm_ffn,
                                          ffn_w_in, ffn_conv_w, ffn_conv_b, ffn_w_out,
                                          state_ffn_conv, None)

    layer = 1
    w = jnp.concatenate([_perm_cols(b_w_qkv[:, :aw], perm), b_w_qkv[:, aw:]], axis=1).astype(BF16)
    b_keep = min(B_WINDOW, seq)
    secs_p = cast_dests([
        (0, aw, lambda c: [("cast", 0, c, scale)]),
        (aw, aw + kvw, lambda c: [("cast", 0, c, None), ("tail", 1, c - aw, b_keep)]),
        (aw + kvw, aw + 2 * kvw, lambda c: [("cast", 0, c, None), ("tail", 2, c - aw - kvw, b_keep)]),
    ])
    qkv, b_k_p, b_v_p = _norm_proj(
        xp, norm_mix[layer], w, none_aux, secs_p,
        [(n_p, aw + 2 * kvw, BF16, ROW_TILE, "all"), (bp * b_keep, kvw, F32, b_keep, "tail"),
         (bp * b_keep, kvw, F32, b_keep, "tail")], seq_tiles, ROW_TILE)
    secs_s = cast_dests([
        (0, aw, lambda c: [("cast", 0, c, scale)]),
        (aw, aw + 2 * kvw, lambda c: [("cast", 1, c - aw, None)]),
    ])
    q_s, kv_s = _norm_proj(xs, norm_mix[layer], w, none_aux, secs_s,
                           [(n_s, aw, BF16, n_s, "all"), (n_s, 2 * kvw, F32, n_s, "all")], 1, n_s)
    b_pad = CHUNK
    b_win = B_WINDOW + CHUNK + b_pad
    bias_b = _t5_bias(t5_table, _band_rel(b_win, B_WINDOW + b_pad))[perm]
    bias_b = jnp.where((jnp.arange(b_win) >= b_pad)[None, None, :], bias_b, NEG_INF)
    sinks = b_sinks.astype(F32)[perm] * LOG2E
    n_kvp = KV_HEADS // 2
    gq = N_HEADS // KV_HEADS
    att_p = _band_attention(qkv, 0, qkv, aw // LANES, qkv, (aw + kvw) // LANES, bias_b, batch=bp,
                            q_rows=seq, k_rows=seq, n_kv_pairs=n_kvp, n_pairs=gq, win=b_win,
                            n_past=B_WINDOW + b_pad, mode="sink", sinks=sinks)
    ks = kv_s[:, :kvw].reshape(bs, ts, kvw)
    vs = kv_s[:, kvw:].reshape(bs, ts, kvw)
    k_all, k_in = _pad_keys(cache_b_k, ks, b_pad, kvw)
    v_all, v_in = _pad_keys(cache_b_v, vs, b_pad, kvw)
    assert k_all.shape[1] + b_pad == b_win
    att_s = _band_attention(q_s, 0, k_in, 0, v_in, 0, bias_b, batch=bs, q_rows=ts, k_rows=b_win,
                            n_kv_pairs=n_kvp, n_pairs=gq, win=b_win, n_past=0, mode="sink",
                            sinks=sinks)
    b_keep_s = min(B_WINDOW, k_all.shape[1])
    b_k_prompt = b_k_p.reshape(bp, b_keep, KV_HEADS, HEAD_DIM)
    b_v_prompt = b_v_p.reshape(bp, b_keep, KV_HEADS, HEAD_DIM)
    b_k_sample = k_all[:, -b_keep_s:].reshape(bs, b_keep_s, KV_HEADS, HEAD_DIM)
    b_v_sample = v_all[:, -b_keep_s:].reshape(bs, b_keep_s, KV_HEADS, HEAD_DIM)
    w_o = _perm_rows(b_w_o, perm).astype(BF16)
    xp, xs, conv_p1, conv_s1 = _ffn_layer(xp, xs, att_p, att_s, w_o, layer, bp, bs, seq_tiles, ts, norm_ffn,
                                          ffn_w_in, ffn_conv_w, ffn_conv_b, ffn_w_out,
                                          state_ffn_conv, None)

    layer = 2
    iw = C_IDX_HEADS * C_IDX_DIM
    w_idx_w = jnp.pad(c_w_idx_w, ((0, 0), (0, LANES - C_IDX_HEADS)))
    w = jnp.concatenate([_perm_cols(c_w_qkv[:, :aw], perm), c_w_qkv[:, aw:], c_w_idx_q,
                         c_w_idx_k, c_w_idx_k, w_idx_w], axis=1).astype(BF16)
    c_qkv_w = aw + 2 * kvw
    col_ki = c_qkv_w + iw
    col_wi = col_ki + LANES
    knorm = jnp.concatenate([c_idx_k_norm, c_idx_k_norm]).astype(F32).reshape(1, LANES)
    wi_scale = C_IDX_HEADS ** -0.5

    def c_sections(k_out, v_out, qkv_out, by_head):
        def cache(o, c0):
            if by_head:
                return lambda c: ("heads", o, (c - c0) // HEAD_DIM, HEAD_DIM, KV_HEADS, None)
            return lambda c: ("cast", o, c - c0, None)
        k_dest, v_dest = cache(k_out, aw), cache(v_out, aw + kvw)

        def v_dests(c):
            dests = [("cast", qkv_out, c, None), v_dest(c)]
            return dests + [("xpose", 6, c - aw - kvw)] if by_head else dests
        secs = cast_dests([
            (0, aw, lambda c: [("cast", qkv_out, c, scale)]),
            (aw, aw + kvw, lambda c: [("cast", qkv_out, c, None), k_dest(c)]),
            (aw + kvw, c_qkv_w, v_dests),
            (c_qkv_w, col_ki, lambda c: [("cast", qkv_out, c, C_IDX_DIM ** -0.5)]),
        ])
        secs.append((col_ki, LANES, [("kidx", 3, 4)]))
        secs.append((col_wi, LANES, [("cast", 5, 0, wi_scale)]))
        return secs

    def c_outs(n, tm, by_head):
        cache = ((n * KV_HEADS, HEAD_DIM, F32, tm * KV_HEADS, "all") if by_head
                 else (n, kvw, F32, tm, "all"))
        outs = [(n, c_qkv_w + iw, BF16, tm, "all"), cache, cache, (n, LANES, BF16, tm, "all"),
                (n, C_IDX_DIM, F32, tm, "all"), (n, LANES, F32, tm, "all")]
        return outs + [(kvw, n, BF16, kvw, "cols")] if by_head else outs

    qkv, c_k_p, c_v_p, ki_p, kidx_p, wi_p, vt_p = _norm_proj(
        xp, norm_mix[layer], w, knorm, c_sections(1, 2, 0, True), c_outs(n_p, ROW_TILE, True),
        seq_tiles, ROW_TILE)
    qkv_s, c_k_s, c_v_s, ki_s, kidx_s, wi_s = _norm_proj(
        xs, norm_mix[layer], w, knorm, c_sections(1, 2, 0, False), c_outs(n_s, n_s, False), 1, n_s)
    qi_blk = c_qkv_w // iw
    assert qi_blk * iw == c_qkv_w
    t = FLASH_TILE
    assert SELECT_ROWS == t
    sel_p = _select_mask(qkv, qi_blk, wi_p, ki_p, batch=bp, q_rows=seq, n_keys=seq, kpos0=0,
                         qpos0=0, topk=min(C_TOPK, seq // 4))
    bias_c = _flash_bias(_t5_bias(t5_table, _flash_rel())[perm])
    att_p = _flash_attention(qkv, 0, qkv, aw // LANES, vt_p, 0, bias_c, batch=bp,
                             seq=seq, n_kv_pairs=n_kvp, n_groups=1, n_pairs=gq, sel=sel_p)
    n_keys_s = past + ts
    c_pad = (-n_keys_s) % t
    c_win = n_keys_s + c_pad
    k_all, k_in = _pad_keys(cache_c_k, c_k_s.reshape(bs, ts, kvw), c_pad, kvw)
    v_all, v_in = _pad_keys(cache_c_v, c_v_s.reshape(bs, ts, kvw), c_pad, kvw)
    del ki_s
    _, ki_in = _pad_keys(cache_c_kidx, kidx_s.reshape(bs, ts, C_IDX_DIM), c_pad, C_IDX_DIM)
    ki_in = jnp.concatenate([ki_in, ki_in], axis=1)
    qi_s = jnp.pad(qkv_s[:, c_qkv_w:].reshape(bs, ts, iw), ((0, 0), (0, LANES - ts), (0, 0)))
    wi_s = jnp.pad(wi_s.reshape(bs, ts, LANES), ((0, 0), (0, LANES - ts), (0, 0)))
    sel_s = _select_mask(qi_s.reshape(bs * LANES, iw), 0, wi_s.reshape(bs * LANES, LANES), ki_in,
                         batch=bs, q_rows=LANES, n_keys=c_win, kpos0=-c_pad, qpos0=past,
                         topk=min(C_TOPK, n_keys_s // 4))
    sel_s = sel_s[..., :ts].transpose(0, 3, 1, 2).reshape(bs, ts, c_win)
    rel_s = (jnp.arange(c_win)[None, :] - c_pad) - (past + jnp.arange(ts)[:, None])
    pad_ok = (jnp.arange(c_win) >= c_pad)[None, None, :]
    bias_s_t5 = jnp.where(pad_ok, _t5_bias(t5_table, rel_s), NEG_INF)
    att_s = _band_attention(qkv_s, 0, k_in, 0, v_in, 0, bias_s_t5[perm], batch=bs, q_rows=ts,
                            k_rows=c_win, n_kv_pairs=n_kvp, n_pairs=gq, win=c_win, n_past=0,
                            sel=sel_s)
    c_k_prompt = c_k_p.reshape(bp, seq, KV_HEADS, HEAD_DIM)
    c_v_prompt = c_v_p.reshape(bp, seq, KV_HEADS, HEAD_DIM)
    c_kidx_prompt = kidx_p.reshape(bp, seq, C_IDX_DIM)
    c_k_sample = c_k_s.reshape(bs, ts, KV_HEADS, HEAD_DIM)
    c_v_sample = c_v_s.reshape(bs, ts, KV_HEADS, HEAD_DIM)
    c_kidx_sample = kidx_s.reshape(bs, ts, C_IDX_DIM)
    w_o = _perm_rows(c_w_o, perm).astype(BF16)
    xp, xs, conv_p2, conv_s2 = _ffn_layer(xp, xs, att_p, att_s, w_o, layer, bp, bs, seq_tiles, ts, norm_ffn,
                                          ffn_w_in, ffn_conv_w, ffn_conv_b, ffn_w_out,
                                          state_ffn_conv, None)

    layer = 3
    lambda_init = 0.8 - 0.6 * math.exp(-0.3 * layer)
    w = d_w_qkv.astype(BF16)
    lam_vecs = jnp.stack([d_lambda_q1, d_lambda_k1, d_lambda_q2, d_lambda_k2]).astype(F32)
    secs = cast_dests([
        (0, aw, lambda c: [("cast", 0, c, scale)]),
        (aw, 2 * aw, lambda c: [("cast", 0, c, None), ("cast", 1, c - aw, None)]),
        (2 * aw, 3 * aw, lambda c: [("cast", 0, c, None), ("cast", 2, c - 2 * aw, None)]),
    ])

    def d_outs(n, tm):
        return [(n, 3 * aw, BF16, tm, "all"), (n, aw, F32, tm, "all"), (n, aw, F32, tm, "all")]

    secs_p = cast_dests([
        (0, aw, lambda c: [("cast", 0, c, scale)]),
        (aw, 2 * aw, lambda c: [("cast", 0, c, None),
                                ("heads", 1, (c - aw) // HEAD_DIM, HEAD_DIM, 2 * D_HEADS, None)]),
        (2 * aw, 3 * aw, lambda c: [("xpose", 3, c - 2 * aw),
                                    ("heads", 2, (c - 2 * aw) // LANES, LANES, D_HEADS, None)]),
    ])
    qkv, d_k_p, d_v_p, vt_p = _norm_proj(
        xp, norm_mix[layer], w, none_aux, secs_p,
        [(n_p, 2 * aw, BF16, ROW_TILE, "all"),
         (n_p * 2 * D_HEADS, HEAD_DIM, F32, ROW_TILE * 2 * D_HEADS, "all"),
         (n_p * D_HEADS, LANES, F32, ROW_TILE * D_HEADS, "all"),
         (aw, n_p, BF16, aw, "cols")], seq_tiles, ROW_TILE)
    qkv_s, d_k_s, d_v_s = _norm_proj(xs, norm_mix[layer], w, none_aux, secs, d_outs(n_s, n_s), 1,
                                     n_s)
    bias_d = _flash_bias(_t5_bias(t5_table, _flash_rel()))
    d_grp = 4
    att_p = _flash_attention(qkv, 0, qkv, aw // (d_grp * LANES), vt_p, 0,
                             bias_d, batch=bp, seq=seq, n_kv_pairs=D_HEADS, n_groups=d_grp,
                             n_pairs=1, mode="diff",
                             lam_vecs=lam_vecs, subln=d_subln.astype(F32), lambda_init=lambda_init)
    k_all, k_in = _pad_keys(cache_d_k, d_k_s.reshape(bs, ts, aw), c_pad, aw)
    v_all, v_in = _pad_keys(cache_d_v, d_v_s.reshape(bs, ts, aw), c_pad, aw)
    att_s = _band_attention(qkv_s, 0, k_in, 0, v_in, 0, bias_s_t5, batch=bs, q_rows=ts,
                            k_rows=c_win, n_kv_pairs=D_HEADS, n_pairs=1, win=c_win, n_past=0,
                            mode="diff", lam_vecs=lam_vecs, subln=d_subln.astype(F32),
                            lambda_init=lambda_init)
    d_k_prompt = d_k_p.reshape(bp, seq, 2 * D_HEADS, HEAD_DIM)
    d_v_prompt = d_v_p.reshape(bp, seq, D_HEADS, 2 * HEAD_DIM)
    d_k_sample = d_k_s.reshape(bs, ts, 2 * D_HEADS, HEAD_DIM)
    d_v_sample = d_v_s.reshape(bs, ts, D_HEADS, 2 * HEAD_DIM)
    w_o = d_w_o.astype(BF16)
    xp, xs, conv_p3, conv_s3 = _ffn_layer(xp, xs, att_p, att_s, w_o, layer, bp, bs, seq_tiles, ts, norm_ffn,
                                          ffn_w_in, ffn_conv_w, ffn_conv_b, ffn_w_out,
                                          state_ffn_conv, norm_final)
    assert depth == 4

    y_prompt = xp.reshape(bp, seq, d)
    y_sample = xs.reshape(bs, ts, d)
    ffn_conv_prompt = jnp.stack([conv_p0, conv_p1, conv_p2, conv_p3])
    ffn_conv_sample = jnp.stack([conv_s0, conv_s1, conv_s2, conv_s3])
    return (y_prompt, y_sample,
            a_k_prompt, a_v_prompt, a_k_sample, a_v_sample,
            b_k_prompt, b_v_prompt, b_k_sample, b_v_sample,
            c_k_prompt, c_v_prompt, c_kidx_prompt, c_k_sample, c_v_sample, c_kidx_sample,
            d_k_prompt, d_v_prompt, d_k_sample, d_v_sample,
            ffn_conv_prompt, ffn_conv_sample)


def _ffn_layer(xp, xs, att_p, att_s, w_o, layer, bp, bs, seq_tiles, ts, norm_ffn, ffn_w_in,
               ffn_conv_w, ffn_conv_b, ffn_w_out, state, final_gain):
    w_in = ffn_w_in[layer].astype(BF16)
    w_out = ffn_w_out[layer].astype(BF16)
    cw, cb, gain = ffn_conv_w[layer], ffn_conv_b[layer], norm_ffn[layer]
    xp, tail = _block_ffn(xp, att_p, w_o, gain, w_in, cw, cb, w_out, tm=ROW_TILE,
                          seq_tiles=seq_tiles, seq_rows=ROW_TILE * seq_tiles,
                          final_gain=final_gain)
    conv_p = tail.reshape(bp, SUBLANES, D_FF)[:, SUBLANES - (CONV_W - 1):, :]
    n_s = bs * ts
    st = state[layer]
    zeros = jnp.zeros((bs, ts - 2, D_FF), F32)
    fix1 = jnp.concatenate([st[:, 1:2], jnp.zeros((bs, 1, D_FF), F32), zeros], axis=1)
    fix2 = jnp.concatenate([st, zeros], axis=1)
    xs, a_s = _block_ffn(xs, att_s, w_o, gain, w_in, cw, cb, w_out, tm=n_s, seq_tiles=1,
                         seq_rows=ts, fixes=(fix1.reshape(n_s, D_FF), fix2.reshape(n_s, D_FF)),
                         final_gain=final_gain)
    conv_s = a_s.reshape(bs, ts, D_FF)[:, ts - (CONV_W - 1):, :]
    return xp, xs, conv_p, conv_s
```

```python
import functools
import math

import jax
import jax.numpy as jnp
from jax import lax
from jax.experimental import pallas as pl
from jax.experimental.pallas import tpu as pltpu

F32 = jnp.float32
BF16 = jnp.bfloat16

D_MODEL = 1024
CHUNK = 64
N_HEADS = 16
HEAD_DIM = 64
ATTN_WIDTH = N_HEADS * HEAD_DIM
NUM_BUCKETS = 32
T5_MAX_DISTANCE = 128
A_PAST = 512
A_CLIP = 64
B_WINDOW = 128
KV_HEADS = 4
C_TOPK = 256
C_IDX_HEADS = 8
C_IDX_DIM = 64
D_HEADS = 8
D_FF = 2816
CONV_W = 3
RMS_EPS = 1e-6
NEG_INF = -1e30
LOG2E = math.log2(math.e)

LANES = 128
SUBLANES = 8
ROW_TILE = 512
FLASH_TILE = 256
BAND_UNROLL = 4
SELECT_ROWS = FLASH_TILE
FF_CHUNK = 256
PROJ_CHUNK = 512
VMEM_LIMIT = 56 * 1024 * 1024

GQA_PERM = tuple(8 * j + 4 * half + t for j in range(2) for t in range(4) for half in range(2))


def _params(n_grid_dims):
    return pltpu.CompilerParams(
        dimension_semantics=("arbitrary",) * n_grid_dims, vmem_limit_bytes=VMEM_LIMIT)


def _resident(shape, index_map):
    return pl.BlockSpec(shape, index_map, pipeline_mode=pl.Buffered(1))


def _rms(x, gain):
    ms = jnp.mean(x * x, axis=-1, keepdims=True)
    return (x * lax.rsqrt(ms + RMS_EPS)) * gain


def _dot_nt(a, b):
    return lax.dot_general(a, b, (((1,), (1,)), ((), ())), preferred_element_type=F32)


def _norm_proj_body(sections, seq_tiles, n_out, x_ref, g_ref, w_ref, aux_ref, *refs):
    out_refs, scratch_refs = refs[:n_out], refs[n_out:]
    tm = x_ref.shape[0]
    h = _rms(x_ref[...], g_ref[...]).astype(BF16)
    tail_heads = []
    for col0, width, dests in sections:
        y = jnp.dot(h, w_ref[:, col0:col0 + width], preferred_element_type=F32)
        for dest in dests:
            kind = dest[0]
            if kind == "cast":
                _, o, oc, scale = dest
                val = y if scale is None else y * scale
                out_refs[o][:, oc:oc + width] = val.astype(out_refs[o].dtype)
            elif kind == "tail":
                _, o, oc, nrows = dest
                out_refs[o][:, oc:oc + width] = y[tm - nrows:, :]
            elif kind == "tail_heads":
                _, o, oc, nrows, dh, nh = dest
                scratch_refs[0][:, oc:oc + width] = y[tm - nrows:, :]
                tail_heads.append((o, oc, width, nrows, dh, nh))
            elif kind == "xpose":
                _, o, orow = dest
                out_refs[o][orow:orow + width, :] = y.T.astype(BF16)
            elif kind == "heads":
                _, o, head0, dh, nh, nrows = dest
                src = y if nrows is None else y[tm - nrows:, :]
                for hh in range(width // dh):
                    out_refs[o][pl.ds(head0 + hh, src.shape[0], stride=nh), :] = (
                        src[:, hh * dh:(hh + 1) * dh])
            elif kind == "kidx":
                _, o_dup, o_f32 = dest
                yn = _rms(y, aux_ref[...])
                out_refs[o_dup][...] = yn.astype(BF16)
                out_refs[o_f32][...] = yn[:, :C_IDX_DIM]
            else:
                raise ValueError(kind)
    if tail_heads:
        @pl.when((pl.program_id(0) % seq_tiles) == seq_tiles - 1)
        def _():
            for o, oc, width, nrows, dh, nh in tail_heads:
                for hh in range(width // dh):
                    c0 = oc + hh * dh
                    out_refs[o][pl.ds((c0 % (nh * dh)) // dh, nrows, stride=nh), :] = (
                        scratch_refs[0][:, c0:c0 + dh])


def _norm_proj(x, gain, w, aux, sections, outs, seq_tiles, tm, tail_scratch=None):
    n = x.shape[0]
    grid = (n // tm,)
    out_shape, out_specs = [], []
    for rows, width, dtype, blk_rows, mode in outs:
        out_shape.append(jax.ShapeDtypeStruct((rows, width), dtype))
        if mode == "all":
            out_specs.append(pl.BlockSpec((blk_rows, width), lambda i: (i, 0)))
        elif mode == "cols":
            out_specs.append(pl.BlockSpec((rows, tm), lambda i: (0, i)))
        else:
            out_specs.append(pl.BlockSpec((blk_rows, width), lambda i: (i // seq_tiles, 0)))
    body = functools.partial(_norm_proj_body, sections, seq_tiles, len(outs))
    return pl.pallas_call(
        body,
        grid=grid,
        in_specs=[
            pl.BlockSpec((tm, D_MODEL), lambda i: (i, 0)),
            _resident((1, D_MODEL), lambda i: (0, 0)),
            _resident(w.shape, lambda i: (0, 0)),
            _resident(aux.shape, lambda i: (0, 0)),
        ],
        out_specs=out_specs,
        out_shape=out_shape,
        scratch_shapes=[pltpu.VMEM(tail_scratch, F32)] if tail_scratch else [],
        compiler_params=_params(1),
        name="norm_proj",
    )(x, gain.reshape(1, D_MODEL), w, aux)


def _softmax_rows(s, sink, want_sum):
    m = jnp.max(s, axis=-1, keepdims=True)
    if sink is not None:
        m = jnp.maximum(m, sink)
    p = jnp.exp2(s - m)
    l = jnp.exp2(sink - m) if sink is not None else None
    if want_sum:
        total = jnp.sum(p, axis=-1, keepdims=True)
        l = total if l is None else total + l
    return p, l


def _band_body(cfg, *refs):
    n_pairs, win, n_past, n_chunks, unroll, mode, lambda_init, has_sel = cfg
    it = iter(refs)
    q_ref, k_ref, v_ref, bias_ref = next(it), next(it), next(it), next(it)
    sel_ref = next(it) if has_sel else None
    sink_ref = next(it) if mode == "sink" else None
    if mode == "diff":
        lam_ref, subln_ref = next(it), next(it)
    o_ref = next(it)
    if n_past:
        kp_ref, vp_ref = next(it), next(it)
        kp_ref[0:n_past, :] = jnp.zeros((n_past, LANES), BF16)
        vp_ref[0:n_past, :] = jnp.zeros((n_past, LANES), BF16)
        kp_ref[n_past:, :] = k_ref[...]
        vp_ref[n_past:, :] = v_ref[...]
    else:
        kp_ref, vp_ref = k_ref, v_ref

    n_heads = 2 * n_pairs
    rows = n_heads * CHUNK
    lane = lax.broadcasted_iota(jnp.int32, (CHUNK, LANES), 1)
    col = lax.broadcasted_iota(jnp.int32, (rows, win), 1)
    low = lane < HEAD_DIM
    if mode == "diff":
        lv = lam_ref[...]
        lam = (jnp.exp(jnp.sum(lv[0:1] * lv[1:2], axis=-1, keepdims=True))
               - jnp.exp(jnp.sum(lv[2:3] * lv[3:4], axis=-1, keepdims=True)) + lambda_init)

    def logits(c):
        r0 = pl.multiple_of(c * CHUNK, CHUNK)
        kwin = kp_ref[pl.ds(r0, win), :]
        qs = []
        for p in range(n_pairs):
            qp = q_ref[pl.ds(r0, CHUNK), p * LANES:(p + 1) * LANES]
            qs += [jnp.where(low, qp, jnp.zeros_like(qp)), jnp.where(low, jnp.zeros_like(qp), qp)]
        s = _dot_nt(jnp.concatenate(qs, axis=0), kwin) + bias_ref[...].reshape(rows, win)
        if n_past:
            s = jnp.where(col >= n_past - c * CHUNK, s, NEG_INF)
        if has_sel:
            s = (s.reshape(n_heads, CHUNK, win) + sel_ref[...].astype(F32)[None]).reshape(rows, win)
        return s

    def attend(c, pr, l):
        r0 = pl.multiple_of(c * CHUNK, CHUNK)
        vwin = vp_ref[pl.ds(r0, win), :]
        if mode == "diff":
            pn = pr / l
            a = (pn[0:CHUNK] - lam * pn[CHUNK:2 * CHUNK]).astype(BF16)
            o = jnp.dot(a, vwin, preferred_element_type=F32)
            o_ref[pl.ds(r0, CHUNK), :] = (_rms(o, subln_ref[...]) * (1.0 - lambda_init)).astype(BF16)
            return
        v_ones = jnp.concatenate([vwin, jnp.ones((win, LANES), BF16)], axis=1)
        ol = jnp.dot(pr.astype(BF16), v_ones, preferred_element_type=F32)
        denom = ol[:, LANES:] if l is None else ol[:, LANES:] + l
        o = ol[:, :LANES] / denom
        for p in range(n_pairs):
            lo = o[2 * p * CHUNK:(2 * p + 1) * CHUNK]
            hi = o[(2 * p + 1) * CHUNK:(2 * p + 2) * CHUNK]
            o_ref[pl.ds(r0, CHUNK), p * LANES:(p + 1) * LANES] = jnp.where(low, lo, hi).astype(BF16)

    sink = sink_ref[...] if mode == "sink" else None

    def chunks(cc, carry):
        ids = [cc * unroll + u for u in range(unroll)]
        ss = [logits(c) for c in ids]
        pls = [_softmax_rows(s, sink, mode == "diff") for s in ss]
        for c, (pr, l) in zip(ids, pls):
            attend(c, pr, l)
        return carry

    if n_chunks == unroll:
        chunks(0, 0)
    else:
        lax.fori_loop(0, n_chunks // unroll, chunks, 0)


def _band_attention(q_arr, q_blk0, k_arr, k_blk0, v_arr, v_blk0, bias, *, batch, q_rows, k_rows,
                    n_kv_pairs, n_pairs, win, n_past, mode="plain", sel=None, sinks=None,
                    lam_vecs=None, subln=None, lambda_init=0.0, chunks_in_flight=BAND_UNROLL):
    n_chunks = q_rows // CHUNK
    qw = n_pairs * LANES
    rows = 2 * n_pairs * CHUNK
    unroll = math.gcd(n_chunks, chunks_in_flight)
    cfg = (n_pairs, win, n_past, n_chunks, unroll, mode, lambda_init, sel is not None)
    in_specs = [
        pl.BlockSpec((q_rows, qw), lambda b, j: (b, q_blk0 + j)),
        pl.BlockSpec((k_rows, LANES), lambda b, j: (b, k_blk0 + j)),
        pl.BlockSpec((k_rows, LANES), lambda b, j: (b, v_blk0 + j)),
        pl.BlockSpec((2 * n_pairs, CHUNK, win), lambda b, j: (j, 0, 0)),
    ]
    args = [q_arr, k_arr, v_arr, bias]
    if sel is not None:
        in_specs.append(pl.BlockSpec((None, CHUNK, win), lambda b, j: (b, 0, 0)))
        args.append(sel)
    if mode == "sink":
        in_specs.append(pl.BlockSpec((rows, 1), lambda b, j: (j, 0)))
        args.append(jnp.repeat(sinks, CHUNK).reshape(n_kv_pairs * rows, 1))
    if mode == "diff":
        in_specs.append(pl.BlockSpec((4, HEAD_DIM), lambda b, j: (0, 0)))
        in_specs.append(pl.BlockSpec((1, LANES), lambda b, j: (0, 0)))
        args += [lam_vecs, subln.reshape(1, LANES)]
    scratch = []
    if n_past:
        scratch = [pltpu.VMEM((n_past + k_rows, LANES), BF16)] * 2
    return pl.pallas_call(
        functools.partial(_band_body, cfg),
        grid=(batch, n_kv_pairs),
        in_specs=in_specs,
        out_specs=pl.BlockSpec((q_rows, qw), lambda b, j: (b, j)),
        out_shape=jax.ShapeDtypeStruct((batch * q_rows, n_kv_pairs * qw), BF16),
        scratch_shapes=scratch,
        compiler_params=_params(2),
        name="band_attention",
    )(*args)


def _flash_t_body(cfg, *refs):
    n_groups, n_pairs, mode, lambda_init, has_sel = cfg
    it = iter(refs)
    q_ref, k_ref, vt_in_ref, bias_ref = next(it), next(it), next(it), next(it)
    sel_ref = next(it) if has_sel else None
    if mode == "diff":
        lam_ref, subln_ref = next(it), next(it)
    o_ref = next(it)
    qs_ref, m_ref, acc_ref = next(it), next(it), next(it)
    sa_ref, sb_ref, pa_ref, pb_ref, ala_ref, alb_ref = (next(it), next(it), next(it), next(it),
                                                        next(it), next(it))
    t = FLASH_TILE
    n_heads = 2 * n_pairs
    rows = n_heads * t
    i = pl.program_id(2)
    lane = lax.broadcasted_iota(jnp.int32, (t, LANES), 1)
    low = lane < HEAD_DIM

    for g in range(n_groups):
        qs = []
        for p in range(n_pairs):
            qp = q_ref[:, (g * n_pairs + p) * LANES:(g * n_pairs + p + 1) * LANES]
            qs += [jnp.where(low, qp, jnp.zeros_like(qp)), jnp.where(low, jnp.zeros_like(qp), qp)]
        qs_ref[g] = jnp.concatenate(qs, axis=0)
    m_ref[...] = jnp.full(m_ref.shape, NEG_INF, F32)
    acc_ref[...] = jnp.zeros(acc_ref.shape, F32)

    def logits_into(s_ref, jt):
        jc = jnp.minimum(jt, i)
        kind = jnp.where(jt > i, 3, jnp.minimum(i - jt, 2))
        r0 = pl.multiple_of(jc * t, t)
        for g in range(n_groups):
            kb = k_ref[pl.ds(r0, t), g * LANES:(g + 1) * LANES]
            s = _dot_nt(kb, qs_ref[g]) + bias_ref[kind, :, g * rows:(g + 1) * rows]
            if has_sel:
                s = s + jnp.concatenate([sel_ref[jc].astype(F32)] * n_heads, axis=1)
            s_ref[g] = s

    ones_rows = jnp.ones((FLASH_SUM_ROWS, t), BF16)

    def attend(p_ref, al_ref, jt):
        r0 = pl.multiple_of(jnp.clip(jt, 0, i) * t, t)
        for g in range(n_groups):
            vt = jnp.concatenate([vt_in_ref[g * LANES:(g + 1) * LANES, pl.ds(r0, t)], ones_rows],
                                 axis=0)
            pv = jnp.dot(vt, p_ref[g], preferred_element_type=F32)
            acc_ref[g] = al_ref[g] * acc_ref[g] + pv

    def softmax_from(s_ref, p_ref, al_ref):
        for g in range(n_groups):
            s = s_ref[g]
            m_old = m_ref[g]
            m_new = jnp.maximum(m_old, jnp.max(s, axis=0, keepdims=True))
            m_ref[g] = m_new
            p_ref[g] = jnp.exp2(s - m_new).astype(BF16)
            al_ref[g] = jnp.exp2(m_old - m_new)

    logits_into(sa_ref, 0)
    pb_ref[...] = jnp.zeros(pb_ref.shape, BF16)
    alb_ref[...] = jnp.ones(alb_ref.shape, F32)

    def two_tiles(u, carry):
        k0 = 2 * u
        logits_into(sb_ref, k0 + 1)
        attend(pb_ref, alb_ref, k0 - 1)
        softmax_from(sa_ref, pa_ref, ala_ref)
        logits_into(sa_ref, k0 + 2)
        attend(pa_ref, ala_ref, k0)
        softmax_from(sb_ref, pb_ref, alb_ref)
        return carry

    n_trips = lax.shift_right_logical(i + 1, 1)
    lax.fori_loop(0, n_trips, two_tiles, 0)
    odd_count = ((i + 1) & 1) == 1

    @pl.when(odd_count)
    def _():
        softmax_from(sa_ref, pa_ref, ala_ref)
        attend(pb_ref, alb_ref, i - 1)
        attend(pa_ref, ala_ref, i)

    @pl.when(jnp.logical_not(odd_count))
    def _():
        attend(pb_ref, alb_ref, i)

    if mode == "diff":
        lv = lam_ref[...]
        lam = (jnp.exp(jnp.sum(lv[0:1] * lv[1:2], axis=-1, keepdims=True))
               - jnp.exp(jnp.sum(lv[2:3] * lv[3:4], axis=-1, keepdims=True)) + lambda_init)
    drow_low = lax.broadcasted_iota(jnp.int32, (LANES, t), 0) < HEAD_DIM
    for g in range(n_groups):
        acc = acc_ref[g]
        res = acc[:LANES] / acc[LANES:LANES + 1]
        for p in range(n_pairs):
            lo = res[:, 2 * p * t:(2 * p + 1) * t]
            hi = res[:, (2 * p + 1) * t:(2 * p + 2) * t]
            if mode == "diff":
                d = lo - lam * hi
                ms = jnp.mean(d * d, axis=0, keepdims=True)
                o = ((d * lax.rsqrt(ms + RMS_EPS)) * subln_ref[...] * (1.0 - lambda_init)).T
            else:
                o = jnp.where(drow_low, lo, hi).T
            o_ref[:, (g * n_pairs + p) * LANES:(g * n_pairs + p + 1) * LANES] = o.astype(BF16)


def _flash_attention(q_arr, q_blk0, k_arr, k_blk0, vt_arr, v_blk0, bias, *, batch, seq, n_kv_pairs,
                     n_groups, n_pairs, mode="plain", sel=None, lam_vecs=None, subln=None,
                     lambda_init=0.0):
    t = FLASH_TILE
    nq = seq // t
    qw = n_groups * n_pairs * LANES
    kw = n_groups * LANES
    rows = 2 * n_pairs * t
    n_steps = n_kv_pairs // n_groups
    cfg = (n_groups, n_pairs, mode, lambda_init, sel is not None)
    in_specs = [
        pl.BlockSpec((t, qw), lambda b, j, i: (b * nq + i, q_blk0 + j)),
        pl.BlockSpec((seq, kw), lambda b, j, i: (b, k_blk0 + j)),
        pl.BlockSpec((kw, seq), lambda b, j, i: (v_blk0 + j, b)),
        pl.BlockSpec((4, t, n_groups * rows), lambda b, j, i: (0, 0, j)),
    ]
    args = [q_arr, k_arr, vt_arr, bias]
    if sel is not None:
        in_specs.append(pl.BlockSpec((None, nq, t, t), lambda b, j, i: (b * nq + i, 0, 0, 0)))
        args.append(sel)
    if mode == "diff":
        in_specs.append(pl.BlockSpec((4, HEAD_DIM), lambda b, j, i: (0, 0)))
        in_specs.append(pl.BlockSpec((LANES, t), lambda b, j, i: (0, 0)))
        args += [lam_vecs, jnp.broadcast_to(subln.reshape(LANES, 1), (LANES, t))]
    return pl.pallas_call(
        functools.partial(_flash_t_body, cfg),
        grid=(batch, n_steps, nq),
        in_specs=in_specs,
        out_specs=pl.BlockSpec((t, qw), lambda b, j, i: (b * nq + i, j)),
        out_shape=jax.ShapeDtypeStruct((batch * seq, n_steps * qw), BF16),
        scratch_shapes=[pltpu.VMEM((n_groups, rows, LANES), BF16),
                        pltpu.VMEM((n_groups, 1, rows), F32),
                        pltpu.VMEM((n_groups, LANES + FLASH_SUM_ROWS, rows), F32),
                        pltpu.VMEM((n_groups, t, rows), F32),
                        pltpu.VMEM((n_groups, t, rows), F32),
                        pltpu.VMEM((n_groups, t, rows), BF16),
                        pltpu.VMEM((n_groups, t, rows), BF16),
                        pltpu.VMEM((n_groups, 1, rows), F32),
                        pltpu.VMEM((n_groups, 1, rows), F32)],
        compiler_params=_params(3),
        name="flash_attention",
    )(*args)


def _select_t_body(cfg, qi_ref, wi_ref, ki_ref, o_ref, key_ref, qs_ref, cut_ref):
    n_keys, kpos0, qpos0, topk = cfg
    r = qi_ref.shape[0]
    t = FLASH_TILE
    n_tiles = n_keys // t
    qrow0 = qpos0 + pl.program_id(1) * r
    last_kpos = (lax.shift_right_arithmetic(qrow0 + r - 1, 6) + 1) * CHUNK - 1
    n_adm = jnp.minimum(n_tiles,
                        lax.shift_right_arithmetic(last_kpos - kpos0, t.bit_length() - 1) + 1)
    lane = lax.broadcasted_iota(jnp.int32, (r, LANES), 1)
    low = lane < C_IDX_DIM
    wit = wi_ref[...].T
    qchunk = lax.shift_right_arithmetic(qrow0 + lax.broadcasted_iota(jnp.int32, (t, r), 1), 6)
    krow = lax.broadcasted_iota(jnp.int32, (t, r), 0)

    def admissible(kb):
        kpos = kpos0 + kb * t + krow
        return jnp.logical_and(kpos >= 0, lax.shift_right_arithmetic(kpos, 6) <= qchunk)

    qs = []
    for h in range(C_IDX_HEADS):
        qp = qi_ref[:, (h // 2) * LANES:(h // 2 + 1) * LANES]
        zero = jnp.zeros_like(qp)
        qs.append(jnp.where(low, qp, zero) if h % 2 == 0 else jnp.where(low, zero, qp))
    qs_ref[...] = jnp.concatenate(qs, axis=0)

    def score_tile(kb, carry):
        r0 = pl.multiple_of(kb * t, t)
        dots = jnp.maximum(_dot_nt(ki_ref[pl.ds(r0, t), :], qs_ref[...]), 0.0)
        score = jnp.zeros((t, r), F32)
        for h in range(C_IDX_HEADS):
            score = score + wit[h:h + 1, :] * dots[:, h * r:(h + 1) * r]
        score = jnp.where(admissible(kb), score, NEG_INF)
        bits = pltpu.bitcast(score, jnp.int32)
        key_ref[kb] = jnp.where(bits >= 0, bits, bits ^ jnp.int32(0x7FFFFFFF))
        return carry

    lax.fori_loop(0, n_adm, score_tile, 0)
    key_ref[n_adm] = jnp.full((t, r), jnp.int32(-2 ** 31), jnp.int32)

    def count(pred):
        def two_tiles(u, acc):
            for kb in (2 * u, 2 * u + 1):
                hit = jnp.where(pred(key_ref[kb], kb), 1.0, 0.0)
                acc = acc + jnp.sum(hit.reshape(t // SUBLANES, SUBLANES, r), axis=0)
            return acc
        acc = lax.fori_loop(0, lax.shift_right_logical(n_adm + 1, 1), two_tiles,
                            jnp.zeros((SUBLANES, r), F32))
        return jnp.sum(acc, axis=0, keepdims=True)

    thr = jnp.full((1, r), jnp.int32(-2 ** 31), jnp.int32)
    cand0 = jnp.zeros((1, r), jnp.int32)
    thr = jnp.where(count(lambda k, kb: k >= cand0) >= topk, cand0, thr)

    def value_bit(it, thr):
        cand = thr | lax.shift_left(jnp.int32(1), 30 - it)
        return jnp.where(count(lambda k, kb: k >= cand) >= topk, cand, thr)

    thr = lax.fori_loop(0, 31, value_bit, thr)

    n_ge = count(lambda k, kb: k >= thr)
    cut_ref[...] = jnp.full((1, r), n_keys, jnp.int32)
    n_bits = max(1, (n_keys - 1).bit_length())

    @pl.when(jnp.max(n_ge) > topk)
    def _():
        ties_wanted = topk - count(lambda k, kb: k > thr)

        def index_bit(it, cut):
            cand = cut | lax.shift_left(jnp.int32(1), n_bits - 1 - it)
            before = count(lambda k, kb: jnp.logical_and(k == thr, kb * t + krow < cand))
            return jnp.where(before <= ties_wanted - 1.0, cand, cut)

        cut_ref[...] = lax.fori_loop(0, n_bits, index_bit, jnp.zeros((1, r), jnp.int32))

    cut = cut_ref[...]
    for kb in range(n_tiles):
        @pl.when(kb < n_adm)
        def _():
            kk = key_ref[kb]
            chosen = jnp.logical_or(kk > thr, jnp.logical_and(kk == thr, kb * t + krow <= cut))
            valid = jnp.logical_and(chosen, admissible(kb))
            o_ref[kb] = jnp.where(valid, 0.0, NEG_INF).astype(BF16)

        @pl.when(kb >= n_adm)
        def _():
            o_ref[kb] = jnp.full((t, r), NEG_INF, BF16)


def _select_mask(qi_arr, qi_blk, wi_arr, ki_arr, *, batch, q_rows, n_keys, kpos0, qpos0, topk):
    r = min(SELECT_ROWS, q_rows)
    t = FLASH_TILE
    nq = q_rows // r
    cfg = (n_keys, kpos0, qpos0, topk)
    return pl.pallas_call(
        functools.partial(_select_t_body, cfg),
        grid=(batch, nq),
        in_specs=[
            pl.BlockSpec((r, C_IDX_HEADS * C_IDX_DIM), lambda b, i: (b * nq + i, qi_blk)),
            pl.BlockSpec((r, LANES), lambda b, i: (b * nq + i, 0)),
            pl.BlockSpec((n_keys, LANES), lambda b, i: (b, 0)),
        ],
        out_specs=pl.BlockSpec((None, n_keys // t, t, r), lambda b, i: (b * nq + i, 0, 0, 0)),
        out_shape=jax.ShapeDtypeStruct((batch * nq, n_keys // t, t, r), BF16),
        scratch_shapes=[pltpu.VMEM((n_keys // t + 1, t, r), jnp.int32),
                        pltpu.VMEM((C_IDX_HEADS * r, LANES), BF16),
                        pltpu.VMEM((1, r), jnp.int32)],
        compiler_params=_params(2),
        name="select_mask",
    )(qi_arr, wi_arr, ki_arr)


FLASH_SUM_ROWS = 16
FFN_PAD = 16


def _block_ffn_body(cfg, *refs):
    seq_tiles, seq_rows, stream, final_norm = cfg
    it = iter(refs)
    x_ref, att_ref = next(it), next(it)
    if stream:
        xprev_ref, attprev_ref = next(it), next(it)
    else:
        fix1_ref, fix2_ref = next(it), next(it)
    wo_ref, g_ref, win_ref, cw_ref, cb_ref, wout_ref = (next(it), next(it), next(it), next(it),
                                                        next(it), next(it))
    gfin_ref = next(it) if final_norm else None
    o_ref, st_ref = next(it), next(it)
    hext_ref, act_ref = next(it), next(it)
    tm = x_ref.shape[0]
    pad = FFN_PAD
    gain = g_ref[...]
    if stream:
        aext_ref = next(it)
        aext_ref[0:pad, :] = attprev_ref[...]
        aext_ref[pad:, :] = att_ref[...]
        xext = jnp.concatenate([xprev_ref[...], x_ref[...]], axis=0)
        x1 = xext + jnp.dot(aext_ref[...], wo_ref[...], preferred_element_type=F32)
        o_ref[...] = x1[pad:, :]
        not_start = (pl.program_id(0) % seq_tiles) != 0
        rowi = lax.broadcasted_iota(jnp.int32, (tm + pad, 1), 0)
        keep = jnp.logical_or(rowi >= pad, not_start)
        hext_ref[...] = jnp.where(keep, _rms(x1, gain), 0.0).astype(BF16)
    else:
        x1 = x_ref[...] + jnp.dot(att_ref[...], wo_ref[...], preferred_element_type=F32)
        o_ref[...] = x1
        hext_ref[pad:, :] = _rms(x1, gain).astype(BF16)
        hext_ref[0:pad, :] = jnp.zeros((pad, D_MODEL), BF16)
        assert seq_rows & (seq_rows - 1) == 0
        rmod = lax.broadcasted_iota(jnp.int32, (tm, FF_CHUNK), 0) & (seq_rows - 1)
    hext = hext_ref[...]
    for c0 in range(0, D_FF, FF_CHUNK):
        w_ag = jnp.concatenate([win_ref[:, c0:c0 + FF_CHUNK],
                                win_ref[:, D_FF + c0:D_FF + c0 + FF_CHUNK]], axis=1)
        ag = jnp.dot(hext, w_ag, preferred_element_type=F32)
        a_ext = ag[:, :FF_CHUNK]
        gate = ag[pad:, FF_CHUNK:]
        a = a_ext[pad:, :]
        a1 = pltpu.roll(a_ext, 1, 0)[pad:, :]
        a2 = pltpu.roll(a_ext, 2, 0)[pad:, :]
        if not stream:
            a1 = jnp.where(rmod == 0, fix1_ref[:, c0:c0 + FF_CHUNK], a1)
            a2 = jnp.where(rmod <= 1, fix2_ref[:, c0:c0 + FF_CHUNK], a2)
        cw = cw_ref[:, c0:c0 + FF_CHUNK]
        u = cb_ref[:, c0:c0 + FF_CHUNK] + (cw[0:1] * a2 + cw[1:2] * a1 + cw[2:3] * a)
        act_ref[:, c0:c0 + FF_CHUNK] = (jax.nn.silu(u) * gate).astype(BF16)
        st_ref[:, c0:c0 + FF_CHUNK] = a[tm - SUBLANES:, :] if stream else a
    y = o_ref[...] + jnp.dot(act_ref[...], wout_ref[...], preferred_element_type=F32)
    if final_norm:
        y = _rms(y, gfin_ref[...])
    o_ref[...] = y


def _block_ffn(x, attn, w_o, gain, w_in, conv_w, conv_b, w_out, *, tm, seq_tiles, seq_rows,
               fixes=None, final_gain=None):
    n = x.shape[0]
    stream = fixes is None
    final_norm = final_gain is not None
    cfg = (seq_tiles, seq_rows, stream, final_norm)
    in_specs = [pl.BlockSpec((tm, D_MODEL), lambda i: (i, 0)),
                pl.BlockSpec((tm, ATTN_WIDTH), lambda i: (i, 0))]
    args = [x, attn]
    if stream:
        per = tm // FFN_PAD

        def prev(i):
            return (jnp.maximum(i * per - 1, 0), 0)
        in_specs += [pl.BlockSpec((FFN_PAD, D_MODEL), prev), pl.BlockSpec((FFN_PAD, ATTN_WIDTH), prev)]
        args += [x, attn]
    else:
        in_specs += [pl.BlockSpec((tm, D_FF), lambda i: (i, 0))] * 2
        args += list(fixes)
    in_specs += [
        _resident((ATTN_WIDTH, D_MODEL), lambda i: (0, 0)),
        _resident((1, D_MODEL), lambda i: (0, 0)),
        _resident((D_MODEL, 2 * D_FF), lambda i: (0, 0)),
        _resident((CONV_W, D_FF), lambda i: (0, 0)),
        _resident((1, D_FF), lambda i: (0, 0)),
        _resident((D_FF, D_MODEL), lambda i: (0, 0)),
    ]
    args += [w_o, gain.reshape(1, D_MODEL), w_in, conv_w, conv_b.reshape(1, D_FF), w_out]
    if final_norm:
        in_specs.append(_resident((1, D_MODEL), lambda i: (0, 0)))
        args.append(final_gain.reshape(1, D_MODEL))
    if stream:
        st_shape = jax.ShapeDtypeStruct((n // (tm * seq_tiles) * SUBLANES, D_FF), F32)
        st_spec = pl.BlockSpec((SUBLANES, D_FF), lambda i: (i // seq_tiles, 0))
    else:
        st_shape = jax.ShapeDtypeStruct((n, D_FF), F32)
        st_spec = pl.BlockSpec((tm, D_FF), lambda i: (i, 0))
    return pl.pallas_call(
        functools.partial(_block_ffn_body, cfg),
        grid=(n // tm,),
        in_specs=in_specs,
        out_specs=[pl.BlockSpec((tm, D_MODEL), lambda i: (i, 0)), st_spec],
        out_shape=[jax.ShapeDtypeStruct((n, D_MODEL), F32), st_shape],
        scratch_shapes=[pltpu.VMEM((tm + FFN_PAD, D_MODEL), BF16), pltpu.VMEM((tm, D_FF), BF16)]
        + ([pltpu.VMEM((tm + FFN_PAD, ATTN_WIDTH), BF16)] if stream else []),
        compiler_params=_params(1),
        name="block_ffn",
    )(*args)


def _t5_bucket(rel):
    half = NUM_BUCKETS // 2
    max_exact = half // 2
    base = jnp.where(rel > 0, half, 0)
    n = jnp.abs(rel)
    nf = jnp.maximum(n, 1).astype(F32)
    large = max_exact + (jnp.log(nf / max_exact) / math.log(T5_MAX_DISTANCE / max_exact)
                         * (half - max_exact)).astype(jnp.int32)
    large = jnp.minimum(large, half - 1)
    return base + jnp.where(n < max_exact, n, large)


def _lookup_heads(table, idx):
    onehot = (idx[..., None] == jnp.arange(table.shape[0])).astype(F32)
    return jnp.einsum("...n,nh->h...", onehot, table.astype(F32), precision=lax.Precision.HIGHEST)


def _t5_bias(table, rel):
    return _lookup_heads(table, _t5_bucket(rel))


def _band_rel(win, n_past_eff):
    i = jnp.arange(CHUNK)[:, None]
    j = jnp.arange(win)[None, :]
    return (j - n_past_eff) - i


def _flash_rel():
    t = FLASH_TILE
    r = jnp.arange(t)[:, None]
    c = jnp.arange(t)[None, :]
    return jnp.stack([c - r - d * t for d in range(3)])


def _flash_bias(bias):
    h, kinds, tq, tk = bias.shape
    ok = (jnp.arange(tk)[None, :] // CHUNK) <= (jnp.arange(tq)[:, None] // CHUNK)
    bias = bias.at[:, 0].set(jnp.where(ok[None], bias[:, 0], NEG_INF))
    bias = jnp.concatenate([bias, jnp.full((h, 1, tq, tk), NEG_INF, F32)], axis=1)
    return bias.transpose(1, 3, 0, 2).reshape(kinds + 1, tk, h * tq)


def _perm_cols(w, perm):
    return w.reshape(w.shape[0], len(perm), HEAD_DIM)[:, perm, :].reshape(w.shape[0], -1)


def _perm_rows(w, perm):
    return w.reshape(len(perm), HEAD_DIM, w.shape[1])[perm, :, :].reshape(-1, w.shape[1])


def _pad_keys(cache, new, pad, width):
    b = cache.shape[0]
    allk = jnp.concatenate([cache.reshape(b, -1, width), new.reshape(b, -1, width)], axis=1)
    padded = jnp.pad(allk, ((0, 0), (pad, 0), (0, 0))).astype(BF16)
    return allk, padded.reshape(-1, width)


def kernel(x_prompt, x_sample, cache_a_k, cache_a_v, cache_b_k, cache_b_v, cache_c_k, cache_c_v,
           cache_c_kidx, cache_d_k, cache_d_v, state_ffn_conv, t5_table, norm_mix, norm_ffn,
           norm_final, a_w_qkv, a_w_o, a_rel_bias, b_w_qkv, b_w_o, b_sinks, c_w_qkv, c_w_o,
           c_w_idx_q, c_w_idx_k, c_idx_k_norm, c_w_idx_w, d_w_qkv, d_w_o, d_lambda_q1,
           d_lambda_k1, d_lambda_q2, d_lambda_k2, d_subln, ffn_w_in, ffn_conv_w, ffn_conv_b,
           ffn_w_out):
    bp, seq, d = x_prompt.shape
    bs, ts, _ = x_sample.shape
    past = cache_c_k.shape[1]
    assert d == D_MODEL and ts == CHUNK and seq % ROW_TILE == 0 and past % CHUNK == 0
    n_p, n_s = bp * seq, bs * ts
    seq_tiles = seq // ROW_TILE
    depth = norm_mix.shape[0]
    scale = HEAD_DIM ** -0.5 * LOG2E
    t5_table = t5_table.astype(F32) * LOG2E
    perm = jnp.array(GQA_PERM)
    none_aux = jnp.zeros((1, LANES), F32)
    kvw = KV_HEADS * HEAD_DIM

    xp = x_prompt.reshape(n_p, d)
    xs = x_sample.reshape(n_s, d)

    def cast_dests(col_dests):
        secs = []
        for c0, c1, fn in col_dests:
            for c in range(c0, c1, PROJ_CHUNK):
                secs.append((c, min(PROJ_CHUNK, c1 - c), fn(c)))
        return secs

    layer = 0
    w = a_w_qkv.astype(BF16)
    aw = ATTN_WIDTH
    a_keep = min(A_PAST, seq)
    secs_p = cast_dests([
        (0, aw, lambda c: [("cast", 0, c, scale)]),
        (aw, 2 * aw, lambda c: [("cast", 0, c, None),
                                ("tail_heads", 1, c - aw, a_keep, HEAD_DIM, N_HEADS)]),
        (2 * aw, 3 * aw, lambda c: [("cast", 0, c, None),
                                    ("tail_heads", 2, c - aw, a_keep, HEAD_DIM, N_HEADS)]),
    ])
    a_cache_out = (bp * a_keep * N_HEADS, HEAD_DIM, F32, a_keep * N_HEADS, "tail")
    qkv, a_k_p, a_v_p = _norm_proj(
        xp, norm_mix[layer], w, none_aux, secs_p,
        [(n_p, 3 * aw, BF16, ROW_TILE, "all"), a_cache_out, a_cache_out], seq_tiles, ROW_TILE,
        tail_scratch=(a_keep, 2 * aw))
    secs_s = cast_dests([
        (0, aw, lambda c: [("cast", 0, c, scale)]),
        (aw, 3 * aw, lambda c: [("cast", 1, c - aw, None)]),
    ])
    q_s, kv_s = _norm_proj(xs, norm_mix[layer], w, none_aux, secs_s,
                           [(n_s, aw, BF16, n_s, "all"), (n_s, 2 * aw, F32, n_s, "all")], 1, n_s)
    a_pad = CHUNK
    a_win = A_PAST + CHUNK + a_pad
    rel = _band_rel(a_win, A_PAST + a_pad)
    bias_a = _lookup_heads(a_rel_bias.astype(F32) * LOG2E,jnp.clip(rel, -A_CLIP, A_CLIP) + A_CLIP)
    bias_a = jnp.where((jnp.arange(a_win) >= a_pad)[None, None, :], bias_a, NEG_INF)
    att_p = _band_attention(qkv, 0, qkv, aw // LANES, qkv, 2 * aw // LANES, bias_a, batch=bp,
                            q_rows=seq, k_rows=seq, n_kv_pairs=N_HEADS // 2, n_pairs=1,
                            win=a_win, n_past=A_PAST + a_pad, chunks_in_flight=8)
    ks = kv_s[:, :aw].reshape(bs, ts, aw)
    vs = kv_s[:, aw:].reshape(bs, ts, aw)
    k_all, k_in = _pad_keys(cache_a_k, ks, a_pad, aw)
    v_all, v_in = _pad_keys(cache_a_v, vs, a_pad, aw)
    assert k_all.shape[1] + a_pad == a_win
    att_s = _band_attention(q_s, 0, k_in, 0, v_in, 0, bias_a, batch=bs, q_rows=ts, k_rows=a_win,
                            n_kv_pairs=N_HEADS // 2, n_pairs=1, win=a_win, n_past=0)
    a_keep_s = min(A_PAST, k_all.shape[1])
    a_k_prompt = a_k_p.reshape(bp, a_keep, N_HEADS, HEAD_DIM)
    a_v_prompt = a_v_p.reshape(bp, a_keep, N_HEADS, HEAD_DIM)
    a_k_sample = k_all[:, -a_keep_s:].reshape(bs, a_keep_s, N_HEADS, HEAD_DIM)
    a_v_sample = v_all[:, -a_keep_s:].reshape(bs, a_keep_s, N_HEADS, HEAD_DIM)
    w_o = a_w_o.astype(BF16)
    xp, xs, conv_p0, conv_s0 = _ffn_layer(xp, xs, att_p, att_s, w_o, layer, bp, bs, seq_tiles, ts, norm_ffn,
                                          ffn_w_in, ffn_conv_w, ffn_conv_b, ffn_w_out,
                                          state_ffn_conv, None)

    layer = 1
    w = jnp.concatenate([_perm_cols(b_w_qkv[:, :aw], perm), b_w_qkv[:, aw:]], axis=1).astype(BF16)
    b_keep = min(B_WINDOW, seq)
    secs_p = cast_dests([
        (0, aw, lambda c: [("cast", 0, c, scale)]),
        (aw, aw + kvw, lambda c: [("cast", 0, c, None), ("tail", 1, c - aw, b_keep)]),
        (aw + kvw, aw + 2 * kvw, lambda c: [("cast", 0, c, None), ("tail", 2, c - aw - kvw, b_keep)]),
    ])
    qkv, b_k_p, b_v_p = _norm_proj(
        xp, norm_mix[layer], w, none_aux, secs_p,
        [(n_p, aw + 2 * kvw, BF16, ROW_TILE, "all"), (bp * b_keep, kvw, F32, b_keep, "tail"),
         (bp * b_keep, kvw, F32, b_keep, "tail")], seq_tiles, ROW_TILE)
    secs_s = cast_dests([
        (0, aw, lambda c: [("cast", 0, c, scale)]),
        (aw, aw + 2 * kvw, lambda c: [("cast", 1, c - aw, None)]),
    ])
    q_s, kv_s = _norm_proj(xs, norm_mix[layer], w, none_aux, secs_s,
                           [(n_s, aw, BF16, n_s, "all"), (n_s, 2 * kvw, F32, n_s, "all")], 1, n_s)
    b_pad = CHUNK
    b_win = B_WINDOW + CHUNK + b_pad
    bias_b = _t5_bias(t5_table, _band_rel(b_win, B_WINDOW + b_pad))[perm]
    bias_b = jnp.where((jnp.arange(b_win) >= b_pad)[None, None, :], bias_b, NEG_INF)
    sinks = b_sinks.astype(F32)[perm] * LOG2E
    n_kvp = KV_HEADS // 2
    gq = N_HEADS // KV_HEADS
    att_p = _band_attention(qkv, 0, qkv, aw // LANES, qkv, (aw + kvw) // LANES, bias_b, batch=bp,
                            q_rows=seq, k_rows=seq, n_kv_pairs=n_kvp, n_pairs=gq, win=b_win,
                            n_past=B_WINDOW + b_pad, mode="sink", sinks=sinks)
    ks = kv_s[:, :kvw].reshape(bs, ts, kvw)
    vs = kv_s[:, kvw:].reshape(bs, ts, kvw)
    k_all, k_in = _pad_keys(cache_b_k, ks, b_pad, kvw)
    v_all, v_in = _pad_keys(cache_b_v, vs, b_pad, kvw)
    assert k_all.shape[1] + b_pad == b_win
    att_s = _band_attention(q_s, 0, k_in, 0, v_in, 0, bias_b, batch=bs, q_rows=ts, k_rows=b_win,
                            n_kv_pairs=n_kvp, n_pairs=gq, win=b_win, n_past=0, mode="sink",
                            sinks=sinks)
    b_keep_s = min(B_WINDOW, k_all.shape[1])
    b_k_prompt = b_k_p.reshape(bp, b_keep, KV_HEADS, HEAD_DIM)
    b_v_prompt = b_v_p.reshape(bp, b_keep, KV_HEADS, HEAD_DIM)
    b_k_sample = k_all[:, -b_keep_s:].reshape(bs, b_keep_s, KV_HEADS, HEAD_DIM)
    b_v_sample = v_all[:, -b_keep_s:].reshape(bs, b_keep_s, KV_HEADS, HEAD_DIM)
    w_o = _perm_rows(b_w_o, perm).astype(BF16)
    xp, xs, conv_p1, conv_s1 = _ffn_layer(xp, xs, att_p, att_s, w_o, layer, bp, bs, seq_tiles, ts, norm_ffn,
                                          ffn_w_in, ffn_conv_w, ffn_conv_b, ffn_w_out,
                                          state_ffn_conv, None)

    layer = 2
    iw = C_IDX_HEADS * C_IDX_DIM
    w_idx_w = jnp.pad(c_w_idx_w, ((0, 0), (0, LANES - C_IDX_HEADS)))
    w = jnp.concatenate([_perm_cols(c_w_qkv[:, :aw], perm), c_w_qkv[:, aw:], c_w_idx_q,
                         c_w_idx_k, c_w_idx_k, w_idx_w], axis=1).astype(BF16)
    c_qkv_w = aw + 2 * kvw
    col_ki = c_qkv_w + iw
    col_wi = col_ki + LANES
    knorm = jnp.concatenate([c_idx_k_norm, c_idx_k_norm]).astype(F32).reshape(1, LANES)
    wi_scale = C_IDX_HEADS ** -0.5

    def c_sections(k_out, v_out, qkv_out, by_head):
        def cache(o, c0):
            if by_head:
                return lambda c: ("heads", o, (c - c0) // HEAD_DIM, HEAD_DIM, KV_HEADS, None)
            return lambda c: ("cast", o, c - c0, None)
        k_dest, v_dest = cache(k_out, aw), cache(v_out, aw + kvw)

        def v_dests(c):
            dests = [("cast", qkv_out, c, None), v_dest(c)]
            return dests + [("xpose", 6, c - aw - kvw)] if by_head else dests
        secs = cast_dests([
            (0, aw, lambda c: [("cast", qkv_out, c, scale)]),
            (aw, aw + kvw, lambda c: [("cast", qkv_out, c, None), k_dest(c)]),
            (aw + kvw, c_qkv_w, v_dests),
            (c_qkv_w, col_ki, lambda c: [("cast", qkv_out, c, C_IDX_DIM ** -0.5)]),
        ])
        secs.append((col_ki, LANES, [("kidx", 3, 4)]))
        secs.append((col_wi, LANES, [("cast", 5, 0, wi_scale)]))
        return secs

    def c_outs(n, tm, by_head):
        cache = ((n * KV_HEADS, HEAD_DIM, F32, tm * KV_HEADS, "all") if by_head
                 else (n, kvw, F32, tm, "all"))
        outs = [(n, c_qkv_w + iw, BF16, tm, "all"), cache, cache, (n, LANES, BF16, tm, "all"),
                (n, C_IDX_DIM, F32, tm, "all"), (n, LANES, F32, tm, "all")]
        return outs + [(kvw, n, BF16, kvw, "cols")] if by_head else outs

    qkv, c_k_p, c_v_p, ki_p, kidx_p, wi_p, vt_p = _norm_proj(
        xp, norm_mix[layer], w, knorm, c_sections(1, 2, 0, True), c_outs(n_p, ROW_TILE, True),
        seq_tiles, ROW_TILE)
    qkv_s, c_k_s, c_v_s, ki_s, kidx_s, wi_s = _norm_proj(
        xs, norm_mix[layer], w, knorm, c_sections(1, 2, 0, False), c_outs(n_s, n_s, False), 1, n_s)
    qi_blk = c_qkv_w // iw
    assert qi_blk * iw == c_qkv_w
    t = FLASH_TILE
    assert SELECT_ROWS == t
    sel_p = _select_mask(qkv, qi_blk, wi_p, ki_p, batch=bp, q_rows=seq, n_keys=seq, kpos0=0,
                         qpos0=0, topk=min(C_TOPK, seq // 4))
    bias_c = _flash_bias(_t5_bias(t5_table, _flash_rel())[perm])
    att_p = _flash_attention(qkv, 0, qkv, aw // LANES, vt_p, 0, bias_c, batch=bp,
                             seq=seq, n_kv_pairs=n_kvp, n_groups=1, n_pairs=gq, sel=sel_p)
    n_keys_s = past + ts
    c_pad = (-n_keys_s) % t
    c_win = n_keys_s + c_pad
    k_all, k_in = _pad_keys(cache_c_k, c_k_s.reshape(bs, ts, kvw), c_pad, kvw)
    v_all, v_in = _pad_keys(cache_c_v, c_v_s.reshape(bs, ts, kvw), c_pad, kvw)
    del ki_s
    _, ki_in = _pad_keys(cache_c_kidx, kidx_s.reshape(bs, ts, C_IDX_DIM), c_pad, C_IDX_DIM)
    ki_in = jnp.concatenate([ki_in, ki_in], axis=1)
    qi_s = jnp.pad(qkv_s[:, c_qkv_w:].reshape(bs, ts, iw), ((0, 0), (0, LANES - ts), (0, 0)))
    wi_s = jnp.pad(wi_s.reshape(bs, ts, LANES), ((0, 0), (0, LANES - ts), (0, 0)))
    sel_s = _select_mask(qi_s.reshape(bs * LANES, iw), 0, wi_s.reshape(bs * LANES, LANES), ki_in,
                         batch=bs, q_rows=LANES, n_keys=c_win, kpos0=-c_pad, qpos0=past,
                         topk=min(C_TOPK, n_keys_s // 4))
    sel_s = sel_s[..., :ts].transpose(0, 3, 1, 2).reshape(bs, ts, c_win)
    rel_s = (jnp.arange(c_win)[None, :] - c_pad) - (past + jnp.arange(ts)[:, None])
    pad_ok = (jnp.arange(c_win) >= c_pad)[None, None, :]
    bias_s_t5 = jnp.where(pad_ok, _t5_bias(t5_table, rel_s), NEG_INF)
    att_s = _band_attention(qkv_s, 0, k_in, 0, v_in, 0, bias_s_t5[perm], batch=bs, q_rows=ts,
                            k_rows=c_win, n_kv_pairs=n_kvp, n_pairs=gq, win=c_win, n_past=0,
                            sel=sel_s)
    c_k_prompt = c_k_p.reshape(bp, seq, KV_HEADS, HEAD_DIM)
    c_v_prompt = c_v_p.reshape(bp, seq, KV_HEADS, HEAD_DIM)
    c_kidx_prompt = kidx_p.reshape(bp, seq, C_IDX_DIM)
    c_k_sample = c_k_s.reshape(bs, ts, KV_HEADS, HEAD_DIM)
    c_v_sample = c_v_s.reshape(bs, ts, KV_HEADS, HEAD_DIM)
    c_kidx_sample = kidx_s.reshape(bs, ts, C_IDX_DIM)
    w_o = _perm_rows(c_w_o, perm).astype(BF16)
    xp, xs, conv_p2, conv_s2 = _ffn_layer(xp, xs, att_p, att_s, w_o, layer, bp, bs, seq_tiles, ts, norm_ffn,
                                          ffn_w_in, ffn_conv_w, ffn_conv_b, ffn_w_out,
                                          state_ffn_conv, None)

    layer = 3
    lambda_init = 0.8 - 0.6 * math.exp(-0.3 * layer)
    w = d_w_qkv.astype(BF16)
    lam_vecs = jnp.stack([d_lambda_q1, d_lambda_k1, d_lambda_q2, d_lambda_k2]).astype(F32)
    secs = cast_dests([
        (0, aw, lambda c: [("cast", 0, c, scale)]),
        (aw, 2 * aw, lambda c: [("cast", 0, c, None), ("cast", 1, c - aw, None)]),
        (2 * aw, 3 * aw, lambda c: [("cast", 0, c, None), ("cast", 2, c - 2 * aw, None)]),
    ])

    def d_outs(n, tm):
        return [(n, 3 * aw, BF16, tm, "all"), (n, aw, F32, tm, "all"), (n, aw, F32, tm, "all")]

    secs_p = cast_dests([
        (0, aw, lambda c: [("cast", 0, c, scale)]),
        (aw, 2 * aw, lambda c: [("cast", 0, c, None),
                                ("heads", 1, (c - aw) // HEAD_DIM, HEAD_DIM, 2 * D_HEADS, None)]),
        (2 * aw, 3 * aw, lambda c: [("xpose", 3, c - 2 * aw),
                                    ("heads", 2, (c - 2 * aw) // LANES, LANES, D_HEADS, None)]),
    ])
    qkv, d_k_p, d_v_p, vt_p = _norm_proj(
        xp, norm_mix[layer], w, none_aux, secs_p,
        [(n_p, 2 * aw, BF16, ROW_TILE, "all"),
         (n_p * 2 * D_HEADS, HEAD_DIM, F32, ROW_TILE * 2 * D_HEADS, "all"),
         (n_p * D_HEADS, LANES, F32, ROW_TILE * D_HEADS, "all"),
         (aw, n_p, BF16, aw, "cols")], seq_tiles, ROW_TILE)
    qkv_s, d_k_s, d_v_s = _norm_proj(xs, norm_mix[layer], w, none_aux, secs, d_outs(n_s, n_s), 1,
                                     n_s)
    bias_d = _flash_bias(_t5_bias(t5_table, _flash_rel()))
    d_grp = 4
    att_p = _flash_attention(qkv, 0, qkv, aw // (d_grp * LANES), vt_p, 0,
                             bias_d, batch=bp, seq=seq, n_kv_pairs=D_HEADS, n_groups=d_grp,
                             n_pairs=1, mode="diff",
                             lam_vecs=lam_vecs, subln=d_subln.astype(F32), lambda_init=lambda_init)
    k_all, k_in = _pad_keys(cache_d_k, d_k_s.reshape(bs, ts, aw), c_pad, aw)
    v_all, v_in = _pad_keys(cache_d_v, d_v_s.reshape(bs, ts, aw), c_pad, aw)
    att_s = _band_attention(qkv_s, 0, k_in, 0, v_in, 0, bias_s_t5, batch=bs, q_rows=ts,
                            k_rows=c_win, n_kv_pairs=D_HEADS, n_pairs=1, win=c_win, n_past=0,
                            mode="diff", lam_vecs=lam_vecs, subln=d_subln.astype(F32),
                            lambda_init=lambda_init)
    d_k_prompt = d_k_p.reshape(bp, seq, 2 * D_HEADS, HEAD_DIM)
    d_v_prompt = d_v_p.reshape(bp, seq, D_HEADS, 2 * HEAD_DIM)
    d_k_sample = d_k_s.reshape(bs, ts, 2 * D_HEADS, HEAD_DIM)
    d_v_sample = d_v_s.reshape(bs, ts, D_HEADS, 2 * HEAD_DIM)
    w_o = d_w_o.astype(BF16)
    xp, xs, conv_p3, conv_s3 = _ffn_layer(xp, xs, att_p, att_s, w_o, layer, bp, bs, seq_tiles, ts, norm_ffn,
                                          ffn_w_in, ffn_conv_w, ffn_conv_b, ffn_w_out,
                                          state_ffn_conv, norm_final)
    assert depth == 4

    y_prompt = xp.reshape(bp, seq, d)
    y_sample = xs.reshape(bs, ts, d)
    ffn_conv_prompt = jnp.stack([conv_p0, conv_p1, conv_p2, conv_p3])
    ffn_conv_sample = jnp.stack([conv_s0, conv_s1, conv_s2, conv_s3])
    return (y_prompt, y_sample,
            a_k_prompt, a_v_prompt, a_k_sample, a_v_sample,
            b_k_prompt, b_v_prompt, b_k_sample, b_v_sample,
            c_k_prompt, c_v_prompt, c_kidx_prompt, c_k_sample, c_v_sample, c_kidx_sample,
            d_k_prompt, d_v_prompt, d_k_sample, d_v_sample,
            ffn_conv_prompt, ffn_conv_sample)


def _ffn_layer(xp, xs, att_p, att_s, w_o, layer, bp, bs, seq_tiles, ts, norm_ffn, ffn_w_in,
               ffn_conv_w, ffn_conv_b, ffn_w_out, state, final_gain):
    w_in = ffn_w_in[layer].astype(BF16)
    w_out = ffn_w_out[layer].astype(BF16)
    cw, cb, gain = ffn_conv_w[layer], ffn_conv_b[layer], norm_ffn[layer]
    xp, tail = _block_ffn(xp, att_p, w_o, gain, w_in, cw, cb, w_out, tm=ROW_TILE,
                          seq_tiles=seq_tiles, seq_rows=ROW_TILE * seq_tiles,
                          final_gain=final_gain)
    conv_p = tail.reshape(bp, SUBLANES, D_FF)[:, SUBLANES - (CONV_W - 1):, :]
    n_s = bs * ts
    st = state[layer]
    zeros = jnp.zeros((bs, ts - 2, D_FF), F32)
    fix1 = jnp.concatenate([st[:, 1:2], jnp.zeros((bs, 1, D_FF), F32), zeros], axis=1)
    fix2 = jnp.concatenate([st, zeros], axis=1)
    xs, a_s = _block_ffn(xs, att_s, w_o, gain, w_in, cw, cb, w_out, tm=n_s, seq_tiles=1,
                         seq_rows=ts, fixes=(fix1.reshape(n_s, D_FF), fix2.reshape(n_s, D_FF)),
                         final_gain=final_gain)
    conv_s = a_s.reshape(bs, ts, D_FF)[:, ts - (CONV_W - 1):, :]
    return xp, xs, conv_p, conv_s
```

```python
import functools
import math

import jax
import jax.numpy as jnp
from jax import lax
from jax.experimental import pallas as pl
from jax.experimental.pallas import tpu as pltpu

F32 = jnp.float32
BF16 = jnp.bfloat16

D_MODEL = 1024
CHUNK = 64
N_HEADS = 16
HEAD_DIM = 64
ATTN_WIDTH = N_HEADS * HEAD_DIM
NUM_BUCKETS = 32
T5_MAX_DISTANCE = 128
A_PAST = 512
A_CLIP = 64
B_WINDOW = 128
KV_HEADS = 4
C_TOPK = 256
C_IDX_HEADS = 8
C_IDX_DIM = 64
D_HEADS = 8
D_FF = 2816
CONV_W = 3
RMS_EPS = 1e-6
NEG_INF = -1e30
LOG2E = math.log2(math.e)

LANES = 128
SUBLANES = 8
ROW_TILE = 512
FLASH_TILE = 256
BAND_UNROLL = 4
SELECT_ROWS = FLASH_TILE
FF_CHUNK = 256
PROJ_CHUNK = 512
VMEM_LIMIT = 56 * 1024 * 1024

GQA_PERM = tuple(8 * j + 4 * half + t for j in range(2) for t in range(4) for half in range(2))


def _params(n_grid_dims):
    return pltpu.CompilerParams(
        dimension_semantics=("arbitrary",) * n_grid_dims, vmem_limit_bytes=VMEM_LIMIT)


def _resident(shape, index_map):
    return pl.BlockSpec(shape, index_map, pipeline_mode=pl.Buffered(1))


def _rms(x, gain):
    ms = jnp.mean(x * x, axis=-1, keepdims=True)
    return (x * lax.rsqrt(ms + RMS_EPS)) * gain


def _dot_nt(a, b):
    return lax.dot_general(a, b, (((1,), (1,)), ((), ())), preferred_element_type=F32)


def _norm_proj_body(sections, seq_tiles, n_out, x_ref, g_ref, w_ref, aux_ref, *refs):
    out_refs, scratch_refs = refs[:n_out], refs[n_out:]
    tm = x_ref.shape[0]
    h = _rms(x_ref[...], g_ref[...]).astype(BF16)
    tail_heads = []
    for col0, width, dests in sections:
        y = jnp.dot(h, w_ref[:, col0:col0 + width], preferred_element_type=F32)
        for dest in dests:
            kind = dest[0]
            if kind == "cast":
                _, o, oc, scale = dest
                val = y if scale is None else y * scale
                out_refs[o][:, oc:oc + width] = val.astype(out_refs[o].dtype)
            elif kind == "tail":
                _, o, oc, nrows = dest
                out_refs[o][:, oc:oc + width] = y[tm - nrows:, :]
            elif kind == "tail_heads":
                _, o, oc, nrows, dh, nh = dest
                scratch_refs[0][:, oc:oc + width] = y[tm - nrows:, :]
                tail_heads.append((o, oc, width, nrows, dh, nh))
            elif kind == "xpose":
                _, o, orow = dest
                out_refs[o][orow:orow + width, :] = y.T.astype(BF16)
            elif kind == "heads":
                _, o, head0, dh, nh, nrows = dest
                src = y if nrows is None else y[tm - nrows:, :]
                for hh in range(width // dh):
                    out_refs[o][pl.ds(head0 + hh, src.shape[0], stride=nh), :] = (
                        src[:, hh * dh:(hh + 1) * dh])
            elif kind == "kidx":
                _, o_dup, o_f32 = dest
                yn = _rms(y, aux_ref[...])
                out_refs[o_dup][...] = yn.astype(BF16)
                out_refs[o_f32][...] = yn[:, :C_IDX_DIM]
            else:
                raise ValueError(kind)
    if tail_heads:
        @pl.when((pl.program_id(0) % seq_tiles) == seq_tiles - 1)
        def _():
            for o, oc, width, nrows, dh, nh in tail_heads:
                for hh in range(width // dh):
                    c0 = oc + hh * dh
                    out_refs[o][pl.ds((c0 % (nh * dh)) // dh, nrows, stride=nh), :] = (
                        scratch_refs[0][:, c0:c0 + dh])


def _norm_proj(x, gain, w, aux, sections, outs, seq_tiles, tm, tail_scratch=None):
    n = x.shape[0]
    grid = (n // tm,)
    out_shape, out_specs = [], []
    for rows, width, dtype, blk_rows, mode in outs:
        out_shape.append(jax.ShapeDtypeStruct((rows, width), dtype))
        if mode == "all":
            out_specs.append(pl.BlockSpec((blk_rows, width), lambda i: (i, 0)))
        elif mode == "cols":
            out_specs.append(pl.BlockSpec((rows, tm), lambda i: (0, i)))
        else:
            out_specs.append(pl.BlockSpec((blk_rows, width), lambda i: (i // seq_tiles, 0)))
    body = functools.partial(_norm_proj_body, sections, seq_tiles, len(outs))
    return pl.pallas_call(
        body,
        grid=grid,
        in_specs=[
            pl.BlockSpec((tm, D_MODEL), lambda i: (i, 0)),
            _resident((1, D_MODEL), lambda i: (0, 0)),
            _resident(w.shape, lambda i: (0, 0)),
            _resident(aux.shape, lambda i: (0, 0)),
        ],
        out_specs=out_specs,
        out_shape=out_shape,
        scratch_shapes=[pltpu.VMEM(tail_scratch, F32)] if tail_scratch else [],
        compiler_params=_params(1),
        name="norm_proj",
    )(x, gain.reshape(1, D_MODEL), w, aux)


def _softmax_rows(s, sink, want_sum):
    m = jnp.max(s, axis=-1, keepdims=True)
    if sink is not None:
        m = jnp.maximum(m, sink)
    p = jnp.exp2(s - m)
    l = jnp.exp2(sink - m) if sink is not None else None
    if want_sum:
        total = jnp.sum(p, axis=-1, keepdims=True)
        l = total if l is None else total + l
    return p, l


def _band_body(cfg, *refs):
    n_pairs, win, n_past, n_chunks, unroll, mode, lambda_init, has_sel = cfg
    it = iter(refs)
    q_ref, k_ref, v_ref, bias_ref = next(it), next(it), next(it), next(it)
    sel_ref = next(it) if has_sel else None
    sink_ref = next(it) if mode == "sink" else None
    if mode == "diff":
        lam_ref, subln_ref = next(it), next(it)
    o_ref = next(it)
    if n_past:
        kp_ref, vp_ref = next(it), next(it)
        kp_ref[0:n_past, :] = jnp.zeros((n_past, LANES), BF16)
        vp_ref[0:n_past, :] = jnp.zeros((n_past, LANES), BF16)
        kp_ref[n_past:, :] = k_ref[...]
        vp_ref[n_past:, :] = v_ref[...]
    else:
        kp_ref, vp_ref = k_ref, v_ref

    n_heads = 2 * n_pairs
    rows = n_heads * CHUNK
    lane = lax.broadcasted_iota(jnp.int32, (CHUNK, LANES), 1)
    col = lax.broadcasted_iota(jnp.int32, (rows, win), 1)
    low = lane < HEAD_DIM
    if mode == "diff":
        lv = lam_ref[...]
        lam = (jnp.exp(jnp.sum(lv[0:1] * lv[1:2], axis=-1, keepdims=True))
               - jnp.exp(jnp.sum(lv[2:3] * lv[3:4], axis=-1, keepdims=True)) + lambda_init)

    def logits(c):
        r0 = pl.multiple_of(c * CHUNK, CHUNK)
        kwin = kp_ref[pl.ds(r0, win), :]
        qs = []
        for p in range(n_pairs):
            qp = q_ref[pl.ds(r0, CHUNK), p * LANES:(p + 1) * LANES]
            qs += [jnp.where(low, qp, jnp.zeros_like(qp)), jnp.where(low, jnp.zeros_like(qp), qp)]
        s = _dot_nt(jnp.concatenate(qs, axis=0), kwin) + bias_ref[...].reshape(rows, win)
        if n_past:
            s = jnp.where(col >= n_past - c * CHUNK, s, NEG_INF)
        if has_sel:
            s = (s.reshape(n_heads, CHUNK, win) + sel_ref[...].astype(F32)[None]).reshape(rows, win)
        return s

    def attend(c, pr, l):
        r0 = pl.multiple_of(c * CHUNK, CHUNK)
        vwin = vp_ref[pl.ds(r0, win), :]
        if mode == "diff":
            pn = pr / l
            a = (pn[0:CHUNK] - lam * pn[CHUNK:2 * CHUNK]).astype(BF16)
            o = jnp.dot(a, vwin, preferred_element_type=F32)
            o_ref[pl.ds(r0, CHUNK), :] = (_rms(o, subln_ref[...]) * (1.0 - lambda_init)).astype(BF16)
            return
        v_ones = jnp.concatenate([vwin, jnp.ones((win, LANES), BF16)], axis=1)
        ol = jnp.dot(pr.astype(BF16), v_ones, preferred_element_type=F32)
        denom = ol[:, LANES:] if l is None else ol[:, LANES:] + l
        o = ol[:, :LANES] / denom
        for p in range(n_pairs):
            lo = o[2 * p * CHUNK:(2 * p + 1) * CHUNK]
            hi = o[(2 * p + 1) * CHUNK:(2 * p + 2) * CHUNK]
            o_ref[pl.ds(r0, CHUNK), p * LANES:(p + 1) * LANES] = jnp.where(low, lo, hi).astype(BF16)

    sink = sink_ref[...] if mode == "sink" else None

    def chunks(cc, carry):
        ids = [cc * unroll + u for u in range(unroll)]
        ss = [logits(c) for c in ids]
        pls = [_softmax_rows(s, sink, mode == "diff") for s in ss]
        for c, (pr, l) in zip(ids, pls):
            attend(c, pr, l)
        return carry

    if n_chunks == unroll:
        chunks(0, 0)
    else:
        lax.fori_loop(0, n_chunks // unroll, chunks, 0)


def _band_attention(q_arr, q_blk0, k_arr, k_blk0, v_arr, v_blk0, bias, *, batch, q_rows, k_rows,
                    n_kv_pairs, n_pairs, win, n_past, mode="plain", sel=None, sinks=None,
                    lam_vecs=None, subln=None, lambda_init=0.0, chunks_in_flight=BAND_UNROLL):
    n_chunks = q_rows // CHUNK
    qw = n_pairs * LANES
    rows = 2 * n_pairs * CHUNK
    unroll = math.gcd(n_chunks, chunks_in_flight)
    cfg = (n_pairs, win, n_past, n_chunks, unroll, mode, lambda_init, sel is not None)
    in_specs = [
        pl.BlockSpec((q_rows, qw), lambda b, j: (b, q_blk0 + j)),
        pl.BlockSpec((k_rows, LANES), lambda b, j: (b, k_blk0 + j)),
        pl.BlockSpec((k_rows, LANES), lambda b, j: (b, v_blk0 + j)),
        pl.BlockSpec((2 * n_pairs, CHUNK, win), lambda b, j: (j, 0, 0)),
    ]
    args = [q_arr, k_arr, v_arr, bias]
    if sel is not None:
        in_specs.append(pl.BlockSpec((None, CHUNK, win), lambda b, j: (b, 0, 0)))
        args.append(sel)
    if mode == "sink":
        in_specs.append(pl.BlockSpec((rows, 1), lambda b, j: (j, 0)))
        args.append(jnp.repeat(sinks, CHUNK).reshape(n_kv_pairs * rows, 1))
    if mode == "diff":
        in_specs.append(pl.BlockSpec((4, HEAD_DIM), lambda b, j: (0, 0)))
        in_specs.append(pl.BlockSpec((1, LANES), lambda b, j: (0, 0)))
        args += [lam_vecs, subln.reshape(1, LANES)]
    scratch = []
    if n_past:
        scratch = [pltpu.VMEM((n_past + k_rows, LANES), BF16)] * 2
    return pl.pallas_call(
        functools.partial(_band_body, cfg),
        grid=(batch, n_kv_pairs),
        in_specs=in_specs,
        out_specs=pl.BlockSpec((q_rows, qw), lambda b, j: (b, j)),
        out_shape=jax.ShapeDtypeStruct((batch * q_rows, n_kv_pairs * qw), BF16),
        scratch_shapes=scratch,
        compiler_params=_params(2),
        name="band_attention",
    )(*args)


def _flash_t_body(cfg, *refs):
    n_groups, n_pairs, mode, lambda_init, has_sel = cfg
    it = iter(refs)
    q_ref, k_ref, vt_in_ref, bias_ref = next(it), next(it), next(it), next(it)
    sel_ref = next(it) if has_sel else None
    if mode == "diff":
        lam_ref, subln_ref = next(it), next(it)
    o_ref = next(it)
    qs_ref, m_ref, acc_ref = next(it), next(it), next(it)
    sa_ref, sb_ref, pa_ref, pb_ref, ala_ref, alb_ref = (next(it), next(it), next(it), next(it),
                                                        next(it), next(it))
    t = FLASH_TILE
    n_heads = 2 * n_pairs
    rows = n_heads * t
    i = pl.program_id(2)
    lane = lax.broadcasted_iota(jnp.int32, (t, LANES), 1)
    low = lane < HEAD_DIM

    for g in range(n_groups):
        qs = []
        for p in range(n_pairs):
            qp = q_ref[:, (g * n_pairs + p) * LANES:(g * n_pairs + p + 1) * LANES]
            qs += [jnp.where(low, qp, jnp.zeros_like(qp)), jnp.where(low, jnp.zeros_like(qp), qp)]
        qs_ref[g] = jnp.concatenate(qs, axis=0)
    m_ref[...] = jnp.full(m_ref.shape, NEG_INF, F32)
    acc_ref[...] = jnp.zeros(acc_ref.shape, F32)

    def logits_into(s_ref, jt):
        jc = jnp.minimum(jt, i)
        kind = jnp.where(jt > i, 3, jnp.minimum(i - jt, 2))
        r0 = pl.multiple_of(jc * t, t)
        for g in range(n_groups):
            kb = k_ref[pl.ds(r0, t), g * LANES:(g + 1) * LANES]
            s = _dot_nt(kb, qs_ref[g]) + bias_ref[kind, :, g * rows:(g + 1) * rows]
            if has_sel:
                s = s + jnp.concatenate([sel_ref[jc].astype(F32)] * n_heads, axis=1)
            s_ref[g] = s

    ones_rows = jnp.ones((FLASH_SUM_ROWS, t), BF16)

    def attend(p_ref, al_ref, jt):
        r0 = pl.multiple_of(jnp.clip(jt, 0, i) * t, t)
        for g in range(n_groups):
            vt = jnp.concatenate([vt_in_ref[g * LANES:(g + 1) * LANES, pl.ds(r0, t)], ones_rows],
                                 axis=0)
            pv = jnp.dot(vt, p_ref[g], preferred_element_type=F32)
            acc_ref[g] = al_ref[g] * acc_ref[g] + pv

    def softmax_from(s_ref, p_ref, al_ref):
        for g in range(n_groups):
            s = s_ref[g]
            m_old = m_ref[g]
            m_new = jnp.maximum(m_old, jnp.max(s, axis=0, keepdims=True))
            m_ref[g] = m_new
            p_ref[g] = jnp.exp2(s - m_new).astype(BF16)
            al_ref[g] = jnp.exp2(m_old - m_new)

    logits_into(sa_ref, 0)
    pb_ref[...] = jnp.zeros(pb_ref.shape, BF16)
    alb_ref[...] = jnp.ones(alb_ref.shape, F32)

    def two_tiles(u, carry):
        k0 = 2 * u
        logits_into(sb_ref, k0 + 1)
        attend(pb_ref, alb_ref, k0 - 1)
        softmax_from(sa_ref, pa_ref, ala_ref)
        logits_into(sa_ref, k0 + 2)
        attend(pa_ref, ala_ref, k0)
        softmax_from(sb_ref, pb_ref, alb_ref)
        return carry

    n_trips = lax.shift_right_logical(i + 1, 1)
    lax.fori_loop(0, n_trips, two_tiles, 0)
    odd_count = ((i + 1) & 1) == 1

    @pl.when(odd_count)
    def _():
        softmax_from(sa_ref, pa_ref, ala_ref)
        attend(pb_ref, alb_ref, i - 1)
        attend(pa_ref, ala_ref, i)

    @pl.when(jnp.logical_not(odd_count))
    def _():
        attend(pb_ref, alb_ref, i)

    if mode == "diff":
        lv = lam_ref[...]
        lam = (jnp.exp(jnp.sum(lv[0:1] * lv[1:2], axis=-1, keepdims=True))
               - jnp.exp(jnp.sum(lv[2:3] * lv[3:4], axis=-1, keepdims=True)) + lambda_init)
    drow_low = lax.broadcasted_iota(jnp.int32, (LANES, t), 0) < HEAD_DIM
    for g in range(n_groups):
        acc = acc_ref[g]
        res = acc[:LANES] / acc[LANES:LANES + 1]
        for p in range(n_pairs):
            lo = res[:, 2 * p * t:(2 * p + 1) * t]
            hi = res[:, (2 * p + 1) * t:(2 * p + 2) * t]
            if mode == "diff":
                d = lo - lam * hi
                ms = jnp.mean(d * d, axis=0, keepdims=True)
                o = ((d * lax.rsqrt(ms + RMS_EPS)) * subln_ref[...] * (1.0 - lambda_init)).T
            else:
                o = jnp.where(drow_low, lo, hi).T
            o_ref[:, (g * n_pairs + p) * LANES:(g * n_pairs + p + 1) * LANES] = o.astype(BF16)


def _flash_attention(q_arr, q_blk0, k_arr, k_blk0, vt_arr, v_blk0, bias, *, batch, seq, n_kv_pairs,
                     n_groups, n_pairs, mode="plain", sel=None, lam_vecs=None, subln=None,
                     lambda_init=0.0):
    t = FLASH_TILE
    nq = seq // t
    qw = n_groups * n_pairs * LANES
    kw = n_groups * LANES
    rows = 2 * n_pairs * t
    n_steps = n_kv_pairs // n_groups
    cfg = (n_groups, n_pairs, mode, lambda_init, sel is not None)
    in_specs = [
        pl.BlockSpec((t, qw), lambda b, j, i: (b * nq + i, q_blk0 + j)),
        pl.BlockSpec((seq, kw), lambda b, j, i: (b, k_blk0 + j)),
        pl.BlockSpec((kw, seq), lambda b, j, i: (v_blk0 + j, b)),
        pl.BlockSpec((4, t, n_groups * rows), lambda b, j, i: (0, 0, j)),
    ]
    args = [q_arr, k_arr, vt_arr, bias]
    if sel is not None:
        in_specs.append(pl.BlockSpec((None, nq, t, t), lambda b, j, i: (b * nq + i, 0, 0, 0)))
        args.append(sel)
    if mode == "diff":
        in_specs.append(pl.BlockSpec((4, HEAD_DIM), lambda b, j, i: (0, 0)))
        in_specs.append(pl.BlockSpec((LANES, t), lambda b, j, i: (0, 0)))
        args += [lam_vecs, jnp.broadcast_to(subln.reshape(LANES, 1), (LANES, t))]
    return pl.pallas_call(
        functools.partial(_flash_t_body, cfg),
        grid=(batch, n_steps, nq),
        in_specs=in_specs,
        out_specs=pl.BlockSpec((t, qw), lambda b, j, i: (b * nq + i, j)),
        out_shape=jax.ShapeDtypeStruct((batch * seq, n_steps * qw), BF16),
        scratch_shapes=[pltpu.VMEM((n_groups, rows, LANES), BF16),
                        pltpu.VMEM((n_groups, 1, rows), F32),
                        pltpu.VMEM((n_groups, LANES + FLASH_SUM_ROWS, rows), F32),
                        pltpu.VMEM((n_groups, t, rows), F32),
                        pltpu.VMEM((n_groups, t, rows), F32),
                        pltpu.VMEM((n_groups, t, rows), BF16),
                        pltpu.VMEM((n_groups, t, rows), BF16),
                        pltpu.VMEM((n_groups, 1, rows), F32),
                        pltpu.VMEM((n_groups, 1, rows), F32)],
        compiler_params=_params(3),
        name="flash_attention",
    )(*args)


def _select_t_body(cfg, qi_ref, wi_ref, ki_ref, o_ref, key_ref, qs_ref, cut_ref):
    n_keys, kpos0, qpos0, topk = cfg
    r = qi_ref.shape[0]
    t = FLASH_TILE
    n_tiles = n_keys // t
    qrow0 = qpos0 + pl.program_id(1) * r
    last_kpos = (lax.shift_right_arithmetic(qrow0 + r - 1, 6) + 1) * CHUNK - 1
    n_adm = jnp.minimum(n_tiles,
                        lax.shift_right_arithmetic(last_kpos - kpos0, t.bit_length() - 1) + 1)
    lane = lax.broadcasted_iota(jnp.int32, (r, LANES), 1)
    low = lane < C_IDX_DIM
    wit = wi_ref[...].T
    qchunk = lax.shift_right_arithmetic(qrow0 + lax.broadcasted_iota(jnp.int32, (t, r), 1), 6)
    krow = lax.broadcasted_iota(jnp.int32, (t, r), 0)

    def admissible(kb):
        kpos = kpos0 + kb * t + krow
        return jnp.logical_and(kpos >= 0, lax.shift_right_arithmetic(kpos, 6) <= qchunk)

    qs = []
    for h in range(C_IDX_HEADS):
        qp = qi_ref[:, (h // 2) * LANES:(h // 2 + 1) * LANES]
        zero = jnp.zeros_like(qp)
        qs.append(jnp.where(low, qp, zero) if h % 2 == 0 else jnp.where(low, zero, qp))
    qs_ref[...] = jnp.concatenate(qs, axis=0)

    def score_tile(kb, carry):
        r0 = pl.multiple_of(kb * t, t)
        dots = jnp.maximum(_dot_nt(ki_ref[pl.ds(r0, t), :], qs_ref[...]), 0.0)
        score = jnp.zeros((t, r), F32)
        for h in range(C_IDX_HEADS):
            score = score + wit[h:h + 1, :] * dots[:, h * r:(h + 1) * r]
        score = jnp.where(admissible(kb), score, NEG_INF)
        bits = pltpu.bitcast(score, jnp.int32)
        key_ref[kb] = jnp.where(bits >= 0, bits, bits ^ jnp.int32(0x7FFFFFFF))
        return carry

    lax.fori_loop(0, n_adm, score_tile, 0)
    key_ref[n_adm] = jnp.full((t, r), jnp.int32(-2 ** 31), jnp.int32)

    def count(pred):
        def two_tiles(u, acc):
            for kb in (2 * u, 2 * u + 1):
                hit = jnp.where(pred(key_ref[kb], kb), 1.0, 0.0)
                acc = acc + jnp.sum(hit.reshape(t // SUBLANES, SUBLANES, r), axis=0)
            return acc
        acc = lax.fori_loop(0, lax.shift_right_logical(n_adm + 1, 1), two_tiles,
                            jnp.zeros((SUBLANES, r), F32))
        return jnp.sum(acc, axis=0, keepdims=True)

    thr = jnp.full((1, r), jnp.int32(-2 ** 31), jnp.int32)
    cand0 = jnp.zeros((1, r), jnp.int32)
    thr = jnp.where(count(lambda k, kb: k >= cand0) >= topk, cand0, thr)

    def value_bit(it, thr):
        cand = thr | lax.shift_left(jnp.int32(1), 30 - it)
        return jnp.where(count(lambda k, kb: k >= cand) >= topk, cand, thr)

    thr = lax.fori_loop(0, 31, value_bit, thr)

    n_ge = count(lambda k, kb: k >= thr)
    cut_ref[...] = jnp.full((1, r), n_keys, jnp.int32)
    n_bits = max(1, (n_keys - 1).bit_length())

    @pl.when(jnp.max(n_ge) > topk)
    def _():
        ties_wanted = topk - count(lambda k, kb: k > thr)

        def index_bit(it, cut):
            cand = cut | lax.shift_left(jnp.int32(1), n_bits - 1 - it)
            before = count(lambda k, kb: jnp.logical_and(k == thr, kb * t + krow < cand))
            return jnp.where(before <= ties_wanted - 1.0, cand, cut)

        cut_ref[...] = lax.fori_loop(0, n_bits, index_bit, jnp.zeros((1, r), jnp.int32))

    cut = cut_ref[...]
    for kb in range(n_tiles):
        @pl.when(kb < n_adm)
        def _():
            kk = key_ref[kb]
            chosen = jnp.logical_or(kk > thr, jnp.logical_and(kk == thr, kb * t + krow <= cut))
            valid = jnp.logical_and(chosen, admissible(kb))
            o_ref[kb] = jnp.where(valid, 0.0, NEG_INF).astype(BF16)

        @pl.when(kb >= n_adm)
        def _():
            o_ref[kb] = jnp.full((t, r), NEG_INF, BF16)


def _select_mask(qi_arr, qi_blk, wi_arr, ki_arr, *, batch, q_rows, n_keys, kpos0, qpos0, topk):
    r = min(SELECT_ROWS, q_rows)
    t = FLASH_TILE
    nq = q_rows // r
    cfg = (n_keys, kpos0, qpos0, topk)
    return pl.pallas_call(
        functools.partial(_select_t_body, cfg),
        grid=(batch, nq),
        in_specs=[
            pl.BlockSpec((r, C_IDX_HEADS * C_IDX_DIM), lambda b, i: (b * nq + i, qi_blk)),
            pl.BlockSpec((r, LANES), lambda b, i: (b * nq + i, 0)),
            pl.BlockSpec((n_keys, LANES), lambda b, i: (b, 0)),
        ],
        out_specs=pl.BlockSpec((None, n_keys // t, t, r), lambda b, i: (b * nq + i, 0, 0, 0)),
        out_shape=jax.ShapeDtypeStruct((batch * nq, n_keys // t, t, r), BF16),
        scratch_shapes=[pltpu.VMEM((n_keys // t + 1, t, r), jnp.int32),
                        pltpu.VMEM((C_IDX_HEADS * r, LANES), BF16),
                        pltpu.VMEM((1, r), jnp.int32)],
        compiler_params=_params(2),
        name="select_mask",
    )(qi_arr, wi_arr, ki_arr)


FLASH_SUM_ROWS = 16
FFN_PAD = 16


def _block_ffn_body(cfg, *refs):
    seq_tiles, seq_rows, stream, final_norm = cfg
    it = iter(refs)
    x_ref, att_ref = next(it), next(it)
    if stream:
        xprev_ref, attprev_ref = next(it), next(it)
    else:
        fix1_ref, fix2_ref = next(it), next(it)
    wo_ref, g_ref, win_ref, cw_ref, cb_ref, wout_ref = (next(it), next(it), next(it), next(it),
                                                        next(it), next(it))
    gfin_ref = next(it) if final_norm else None
    o_ref, st_ref = next(it), next(it)
    hext_ref, act_ref = next(it), next(it)
    tm = x_ref.shape[0]
    pad = FFN_PAD
    gain = g_ref[...]
    if stream:
        aext_ref = next(it)
        aext_ref[0:pad, :] = attprev_ref[...]
        aext_ref[pad:, :] = att_ref[...]
        xext = jnp.concatenate([xprev_ref[...], x_ref[...]], axis=0)
        x1 = xext + jnp.dot(aext_ref[...], wo_ref[...], preferred_element_type=F32)
        o_ref[...] = x1[pad:, :]
        not_start = (pl.program_id(0) % seq_tiles) != 0
        rowi = lax.broadcasted_iota(jnp.int32, (tm + pad, 1), 0)
        keep = jnp.logical_or(rowi >= pad, not_start)
        hext_ref[...] = jnp.where(keep, _rms(x1, gain), 0.0).astype(BF16)
    else:
        x1 = x_ref[...] + jnp.dot(att_ref[...], wo_ref[...], preferred_element_type=F32)
        o_ref[...] = x1
        hext_ref[pad:, :] = _rms(x1, gain).astype(BF16)
        hext_ref[0:pad, :] = jnp.zeros((pad, D_MODEL), BF16)
        assert seq_rows & (seq_rows - 1) == 0
        rmod = lax.broadcasted_iota(jnp.int32, (tm, FF_CHUNK), 0) & (seq_rows - 1)
    hext = hext_ref[...]
    for c0 in range(0, D_FF, FF_CHUNK):
        w_ag = jnp.concatenate([win_ref[:, c0:c0 + FF_CHUNK],
                                win_ref[:, D_FF + c0:D_FF + c0 + FF_CHUNK]], axis=1)
        ag = jnp.dot(hext, w_ag, preferred_element_type=F32)
        a_ext = ag[:, :FF_CHUNK]
        gate = ag[pad:, FF_CHUNK:]
        a = a_ext[pad:, :]
        a1 = pltpu.roll(a_ext, 1, 0)[pad:, :]
        a2 = pltpu.roll(a_ext, 2, 0)[pad:, :]
        if not stream:
            a1 = jnp.where(rmod == 0, fix1_ref[:, c0:c0 + FF_CHUNK], a1)
            a2 = jnp.where(rmod <= 1, fix2_ref[:, c0:c0 + FF_CHUNK], a2)
        cw = cw_ref[:, c0:c0 + FF_CHUNK]
        u = cb_ref[:, c0:c0 + FF_CHUNK] + (cw[0:1] * a2 + cw[1:2] * a1 + cw[2:3] * a)
        act_ref[:, c0:c0 + FF_CHUNK] = (jax.nn.silu(u) * gate).astype(BF16)
        st_ref[:, c0:c0 + FF_CHUNK] = a[tm - SUBLANES:, :] if stream else a
    y = o_ref[...] + jnp.dot(act_ref[...], wout_ref[...], preferred_element_type=F32)
    if final_norm:
        y = _rms(y, gfin_ref[...])
    o_ref[...] = y


def _block_ffn(x, attn, w_o, gain, w_in, conv_w, conv_b, w_out, *, tm, seq_tiles, seq_rows,
               fixes=None, final_gain=None):
    n = x.shape[0]
    stream = fixes is None
    final_norm = final_gain is not None
    cfg = (seq_tiles, seq_rows, stream, final_norm)
    in_specs = [pl.BlockSpec((tm, D_MODEL), lambda i: (i, 0)),
                pl.BlockSpec((tm, ATTN_WIDTH), lambda i: (i, 0))]
    args = [x, attn]
    if stream:
        per = tm // FFN_PAD

        def prev(i):
            return (jnp.maximum(i * per - 1, 0), 0)
        in_specs += [pl.BlockSpec((FFN_PAD, D_MODEL), prev), pl.BlockSpec((FFN_PAD, ATTN_WIDTH), prev)]
        args += [x, attn]
    else:
        in_specs += [pl.BlockSpec((tm, D_FF), lambda i: (i, 0))] * 2
        args += list(fixes)
    in_specs += [
        _resident((ATTN_WIDTH, D_MODEL), lambda i: (0, 0)),
        _resident((1, D_MODEL), lambda i: (0, 0)),
        _resident((D_MODEL, 2 * D_FF), lambda i: (0, 0)),
        _resident((CONV_W, D_FF), lambda i: (0, 0)),
        _resident((1, D_FF), lambda i: (0, 0)),
        _resident((D_FF, D_MODEL), lambda i: (0, 0)),
    ]
    args += [w_o, gain.reshape(1, D_MODEL), w_in, conv_w, conv_b.reshape(1, D_FF), w_out]
    if final_norm:
        in_specs.append(_resident((1, D_MODEL), lambda i: (0, 0)))
        args.append(final_gain.reshape(1, D_MODEL))
    if stream:
        st_shape = jax.ShapeDtypeStruct((n // (tm * seq_tiles) * SUBLANES, D_FF), F32)
        st_spec = pl.BlockSpec((SUBLANES, D_FF), lambda i: (i // seq_tiles, 0))
    else:
        st_shape = jax.ShapeDtypeStruct((n, D_FF), F32)
        st_spec = pl.BlockSpec((tm, D_FF), lambda i: (i, 0))
    return pl.pallas_call(
        functools.partial(_block_ffn_body, cfg),
        grid=(n // tm,),
        in_specs=in_specs,
        out_specs=[pl.BlockSpec((tm, D_MODEL), lambda i: (i, 0)), st_spec],
        out_shape=[jax.ShapeDtypeStruct((n, D_MODEL), F32), st_shape],
        scratch_shapes=[pltpu.VMEM((tm + FFN_PAD, D_MODEL), BF16), pltpu.VMEM((tm, D_FF), BF16)]
        + ([pltpu.VMEM((tm + FFN_PAD, ATTN_WIDTH), BF16)] if stream else []),
        compiler_params=_params(1),
        name="block_ffn",
    )(*args)


def _t5_bucket(rel):
    half = NUM_BUCKETS // 2
    max_exact = half // 2
    base = jnp.where(rel > 0, half, 0)
    n = jnp.abs(rel)
    nf = jnp.maximum(n, 1).astype(F32)
    large = max_exact + (jnp.log(nf / max_exact) / math.log(T5_MAX_DISTANCE / max_exact)
                         * (half - max_exact)).astype(jnp.int32)
    large = jnp.minimum(large, half - 1)
    return base + jnp.where(n < max_exact, n, large)


def _lookup_heads(table, idx):
    onehot = (idx[..., None] == jnp.arange(table.shape[0])).astype(F32)
    return jnp.einsum("...n,nh->h...", onehot, table.astype(F32), precision=lax.Precision.HIGHEST)


def _t5_bias(table, rel):
    return _lookup_heads(table, _t5_bucket(rel))


def _band_rel(win, n_past_eff):
    i = jnp.arange(CHUNK)[:, None]
    j = jnp.arange(win)[None, :]
    return (j - n_past_eff) - i


def _flash_rel():
    t = FLASH_TILE
    r = jnp.arange(t)[:, None]
    c = jnp.arange(t)[None, :]
    return jnp.stack([c - r - d * t for d in range(3)])


def _flash_bias(bias):
    h, kinds, tq, tk = bias.shape
    ok = (jnp.arange(tk)[None, :] // CHUNK) <= (jnp.arange(tq)[:, None] // CHUNK)
    bias = bias.at[:, 0].set(jnp.where(ok[None], bias[:, 0], NEG_INF))
    bias = jnp.concatenate([bias, jnp.full((h, 1, tq, tk), NEG_INF, F32)], axis=1)
    return bias.transpose(1, 3, 0, 2).reshape(kinds + 1, tk, h * tq)


def _perm_cols(w, perm):
    return w.reshape(w.shape[0], len(perm), HEAD_DIM)[:, perm, :].reshape(w.shape[0], -1)


def _perm_rows(w, perm):
    return w.reshape(len(perm), HEAD_DIM, w.shape[1])[perm, :, :].reshape(-1, w.shape[1])


def _pad_keys(cache, new, pad, width):
    b = cache.shape[0]
    allk = jnp.concatenate([cache.reshape(b, -1, width), new.reshape(b, -1, width)], axis=1)
    padded = jnp.pad(allk, ((0, 0), (pad, 0), (0, 0))).astype(BF16)
    return allk, padded.reshape(-1, width)


def kernel(x_prompt, x_sample, cache_a_k, cache_a_v, cache_b_k, cache_b_v, cache_c_k, cache_c_v,
           cache_c_kidx, cache_d_k, cache_d_v, state_ffn_conv, t5_table, norm_mix, norm_ffn,
           norm_final, a_w_qkv, a_w_o, a_rel_bias, b_w_qkv, b_w_o, b_sinks, c_w_qkv, c_w_o,
           c_w_idx_q, c_w_idx_k, c_idx_k_norm, c_w_idx_w, d_w_qkv, d_w_o, d_lambda_q1,
           d_lambda_k1, d_lambda_q2, d_lambda_k2, d_subln, ffn_w_in, ffn_conv_w, ffn_conv_b,
           ffn_w_out):
    bp, seq, d = x_prompt.shape
    bs, ts, _ = x_sample.shape
    past = cache_c_k.shape[1]
    assert d == D_MODEL and ts == CHUNK and seq % ROW_TILE == 0 and past % CHUNK == 0
    n_p, n_s = bp * seq, bs * ts
    seq_tiles = seq // ROW_TILE
    depth = norm_mix.shape[0]
    scale = HEAD_DIM ** -0.5 * LOG2E
    t5_table = t5_table.astype(F32) * LOG2E
    perm = jnp.array(GQA_PERM)
    none_aux = jnp.zeros((1, LANES), F32)
    kvw = KV_HEADS * HEAD_DIM

    xp = x_prompt.reshape(n_p, d)
    xs = x_sample.reshape(n_s, d)

    def cast_dests(col_dests):
        secs = []
        for c0, c1, fn in col_dests:
            for c in range(c0, c1, PROJ_CHUNK):
                secs.append((c, min(PROJ_CHUNK, c1 - c), fn(c)))
        return secs

    layer = 0
    w = a_w_qkv.astype(BF16)
    aw = ATTN_WIDTH
    a_keep = min(A_PAST, seq)
    secs_p = cast_dests([
        (0, aw, lambda c: [("cast", 0, c, scale)]),
        (aw, 2 * aw, lambda c: [("cast", 0, c, None),
                                ("tail_heads", 1, c - aw, a_keep, HEAD_DIM, N_HEADS)]),
        (2 * aw, 3 * aw, lambda c: [("cast", 0, c, None),
                                    ("tail_heads", 2, c - aw, a_keep, HEAD_DIM, N_HEADS)]),
    ])
    a_cache_out = (bp * a_keep * N_HEADS, HEAD_DIM, F32, a_keep * N_HEADS, "tail")
    qkv, a_k_p, a_v_p = _norm_proj(
        xp, norm_mix[layer], w, none_aux, secs_p,
        [(n_p, 3 * aw, BF16, ROW_TILE, "all"), a_cache_out, a_cache_out], seq_tiles, ROW_TILE,
        tail_scratch=(a_keep, 2 * aw))
    secs_s = cast_dests([
        (0, aw, lambda c: [("cast", 0, c, scale)]),
        (aw, 3 * aw, lambda c: [("cast", 1, c - aw, None)]),
    ])
    q_s, kv_s = _norm_proj(xs, norm_mix[layer], w, none_aux, secs_s,
                           [(n_s, aw, BF16, n_s, "all"), (n_s, 2 * aw, F32, n_s, "all")], 1, n_s)
    a_pad = CHUNK
    a_win = A_PAST + CHUNK + a_pad
    rel = _band_rel(a_win, A_PAST + a_pad)
    bias_a = _lookup_heads(a_rel_bias.astype(F32) * LOG2E,jnp.clip(rel, -A_CLIP, A_CLIP) + A_CLIP)
    bias_a = jnp.where((jnp.arange(a_win) >= a_pad)[None, None, :], bias_a, NEG_INF)
    att_p = _band_attention(qkv, 0, qkv, aw // LANES, qkv, 2 * aw // LANES, bias_a, batch=bp,
                            q_rows=seq, k_rows=seq, n_kv_pairs=N_HEADS // 2, n_pairs=1,
                            win=a_win, n_past=A_PAST + a_pad, chunks_in_flight=16)
    ks = kv_s[:, :aw].reshape(bs, ts, aw)
    vs = kv_s[:, aw:].reshape(bs, ts, aw)
    k_all, k_in = _pad_keys(cache_a_k, ks, a_pad, aw)
    v_all, v_in = _pad_keys(cache_a_v, vs, a_pad, aw)
    assert k_all.shape[1] + a_pad == a_win
    att_s = _band_attention(q_s, 0, k_in, 0, v_in, 0, bias_a, batch=bs, q_rows=ts, k_rows=a_win,
                            n_kv_pairs=N_HEADS // 2, n_pairs=1, win=a_win, n_past=0)
    a_keep_s = min(A_PAST, k_all.shape[1])
    a_k_prompt = a_k_p.reshape(bp, a_keep, N_HEADS, HEAD_DIM)
    a_v_prompt = a_v_p.reshape(bp, a_keep, N_HEADS, HEAD_DIM)
    a_k_sample = k_all[:, -a_keep_s:].reshape(bs, a_keep_s, N_HEADS, HEAD_DIM)
    a_v_sample = v_all[:, -a_keep_s:].reshape(bs, a_keep_s, N_HEADS, HEAD_DIM)
    w_o = a_w_o.astype(BF16)
    xp, xs, conv_p0, conv_s0 = _ffn_layer(xp, xs, att_p, att_s, w_o, layer, bp, bs, seq_tiles, ts, norm_ffn,
                                          ffn_w_in, ffn_conv_w, ffn_conv_b, ffn_w_out,
                                          state_ffn_conv, None)

    layer = 1
    w = jnp.concatenate([_perm_cols(b_w_qkv[:, :aw], perm), b_w_qkv[:, aw:]], axis=1).astype(BF16)
    b_keep = min(B_WINDOW, seq)
    secs_p = cast_dests([
        (0, aw, lambda c: [("cast", 0, c, scale)]),
        (aw, aw + kvw, lambda c: [("cast", 0, c, None), ("tail", 1, c - aw, b_keep)]),
        (aw + kvw, aw + 2 * kvw, lambda c: [("cast", 0, c, None), ("tail", 2, c - aw - kvw, b_keep)]),
    ])
    qkv, b_k_p, b_v_p = _norm_proj(
        xp, norm_mix[layer], w, none_aux, secs_p,
        [(n_p, aw + 2 * kvw, BF16, ROW_TILE, "all"), (bp * b_keep, kvw, F32, b_keep, "tail"),
         (bp * b_keep, kvw, F32, b_keep, "tail")], seq_tiles, ROW_TILE)
    secs_s = cast_dests([
        (0, aw, lambda c: [("cast", 0, c, scale)]),
        (aw, aw + 2 * kvw, lambda c: [("cast", 1, c - aw, None)]),
    ])
    q_s, kv_s = _norm_proj(xs, norm_mix[layer], w, none_aux, secs_s,
                           [(n_s, aw, BF16, n_s, "all"), (n_s, 2 * kvw, F32, n_s, "all")], 1, n_s)
    b_pad = CHUNK
    b_win = B_WINDOW + CHUNK + b_pad
    bias_b = _t5_bias(t5_table, _band_rel(b_win, B_WINDOW + b_pad))[perm]
    bias_b = jnp.where((jnp.arange(b_win) >= b_pad)[None, None, :], bias_b, NEG_INF)
    sinks = b_sinks.astype(F32)[perm] * LOG2E
    n_kvp = KV_HEADS // 2
    gq = N_HEADS // KV_HEADS
    att_p = _band_attention(qkv, 0, qkv, aw // LANES, qkv, (aw + kvw) // LANES, bias_b, batch=bp,
                            q_rows=seq, k_rows=seq, n_kv_pairs=n_kvp, n_pairs=gq, win=b_win,
                            n_past=B_WINDOW + b_pad, mode="sink", sinks=sinks, chunks_in_flight=8)
    ks = kv_s[:, :kvw].reshape(bs, ts, kvw)
    vs = kv_s[:, kvw:].reshape(bs, ts, kvw)
    k_all, k_in = _pad_keys(cache_b_k, ks, b_pad, kvw)
    v_all, v_in = _pad_keys(cache_b_v, vs, b_pad, kvw)
    assert k_all.shape[1] + b_pad == b_win
    att_s = _band_attention(q_s, 0, k_in, 0, v_in, 0, bias_b, batch=bs, q_rows=ts, k_rows=b_win,
                            n_kv_pairs=n_kvp, n_pairs=gq, win=b_win, n_past=0, mode="sink",
                            sinks=sinks)
    b_keep_s = min(B_WINDOW, k_all.shape[1])
    b_k_prompt = b_k_p.reshape(bp, b_keep, KV_HEADS, HEAD_DIM)
    b_v_prompt = b_v_p.reshape(bp, b_keep, KV_HEADS, HEAD_DIM)
    b_k_sample = k_all[:, -b_keep_s:].reshape(bs, b_keep_s, KV_HEADS, HEAD_DIM)
    b_v_sample = v_all[:, -b_keep_s:].reshape(bs, b_keep_s, KV_HEADS, HEAD_DIM)
    w_o = _perm_rows(b_w_o, perm).astype(BF16)
    xp, xs, conv_p1, conv_s1 = _ffn_layer(xp, xs, att_p, att_s, w_o, layer, bp, bs, seq_tiles, ts, norm_ffn,
                                          ffn_w_in, ffn_conv_w, ffn_conv_b, ffn_w_out,
                                          state_ffn_conv, None)

    layer = 2
    iw = C_IDX_HEADS * C_IDX_DIM
    w_idx_w = jnp.pad(c_w_idx_w, ((0, 0), (0, LANES - C_IDX_HEADS)))
    w = jnp.concatenate([_perm_cols(c_w_qkv[:, :aw], perm), c_w_qkv[:, aw:], c_w_idx_q,
                         c_w_idx_k, c_w_idx_k, w_idx_w], axis=1).astype(BF16)
    c_qkv_w = aw + 2 * kvw
    col_ki = c_qkv_w + iw
    col_wi = col_ki + LANES
    knorm = jnp.concatenate([c_idx_k_norm, c_idx_k_norm]).astype(F32).reshape(1, LANES)
    wi_scale = C_IDX_HEADS ** -0.5

    def c_sections(k_out, v_out, qkv_out, by_head):
        def cache(o, c0):
            if by_head:
                return lambda c: ("heads", o, (c - c0) // HEAD_DIM, HEAD_DIM, KV_HEADS, None)
            return lambda c: ("cast", o, c - c0, None)
        k_dest, v_dest = cache(k_out, aw), cache(v_out, aw + kvw)

        def v_dests(c):
            dests = [("cast", qkv_out, c, None), v_dest(c)]
            return dests + [("xpose", 6, c - aw - kvw)] if by_head else dests
        secs = cast_dests([
            (0, aw, lambda c: [("cast", qkv_out, c, scale)]),
            (aw, aw + kvw, lambda c: [("cast", qkv_out, c, None), k_dest(c)]),
            (aw + kvw, c_qkv_w, v_dests),
            (c_qkv_w, col_ki, lambda c: [("cast", qkv_out, c, C_IDX_DIM ** -0.5)]),
        ])
        secs.append((col_ki, LANES, [("kidx", 3, 4)]))
        secs.append((col_wi, LANES, [("cast", 5, 0, wi_scale)]))
        return secs

    def c_outs(n, tm, by_head):
        cache = ((n * KV_HEADS, HEAD_DIM, F32, tm * KV_HEADS, "all") if by_head
                 else (n, kvw, F32, tm, "all"))
        outs = [(n, c_qkv_w + iw, BF16, tm, "all"), cache, cache, (n, LANES, BF16, tm, "all"),
                (n, C_IDX_DIM, F32, tm, "all"), (n, LANES, F32, tm, "all")]
        return outs + [(kvw, n, BF16, kvw, "cols")] if by_head else outs

    qkv, c_k_p, c_v_p, ki_p, kidx_p, wi_p, vt_p = _norm_proj(
        xp, norm_mix[layer], w, knorm, c_sections(1, 2, 0, True), c_outs(n_p, ROW_TILE, True),
        seq_tiles, ROW_TILE)
    qkv_s, c_k_s, c_v_s, ki_s, kidx_s, wi_s = _norm_proj(
        xs, norm_mix[layer], w, knorm, c_sections(1, 2, 0, False), c_outs(n_s, n_s, False), 1, n_s)
    qi_blk = c_qkv_w // iw
    assert qi_blk * iw == c_qkv_w
    t = FLASH_TILE
    assert SELECT_ROWS == t
    sel_p = _select_mask(qkv, qi_blk, wi_p, ki_p, batch=bp, q_rows=seq, n_keys=seq, kpos0=0,
                         qpos0=0, topk=min(C_TOPK, seq // 4))
    bias_c = _flash_bias(_t5_bias(t5_table, _flash_rel())[perm])
    att_p = _flash_attention(qkv, 0, qkv, aw // LANES, vt_p, 0, bias_c, batch=bp,
                             seq=seq, n_kv_pairs=n_kvp, n_groups=1, n_pairs=gq, sel=sel_p)
    n_keys_s = past + ts
    c_pad = (-n_keys_s) % t
    c_win = n_keys_s + c_pad
    k_all, k_in = _pad_keys(cache_c_k, c_k_s.reshape(bs, ts, kvw), c_pad, kvw)
    v_all, v_in = _pad_keys(cache_c_v, c_v_s.reshape(bs, ts, kvw), c_pad, kvw)
    del ki_s
    _, ki_in = _pad_keys(cache_c_kidx, kidx_s.reshape(bs, ts, C_IDX_DIM), c_pad, C_IDX_DIM)
    ki_in = jnp.concatenate([ki_in, ki_in], axis=1)
    qi_s = jnp.pad(qkv_s[:, c_qkv_w:].reshape(bs, ts, iw), ((0, 0), (0, LANES - ts), (0, 0)))
    wi_s = jnp.pad(wi_s.reshape(bs, ts, LANES), ((0, 0), (0, LANES - ts), (0, 0)))
    sel_s = _select_mask(qi_s.reshape(bs * LANES, iw), 0, wi_s.reshape(bs * LANES, LANES), ki_in,
                         batch=bs, q_rows=LANES, n_keys=c_win, kpos0=-c_pad, qpos0=past,
                         topk=min(C_TOPK, n_keys_s // 4))
    sel_s = sel_s[..., :ts].transpose(0, 3, 1, 2).reshape(bs, ts, c_win)
    rel_s = (jnp.arange(c_win)[None, :] - c_pad) - (past + jnp.arange(ts)[:, None])
    pad_ok = (jnp.arange(c_win) >= c_pad)[None, None, :]
    bias_s_t5 = jnp.where(pad_ok, _t5_bias(t5_table, rel_s), NEG_INF)
    att_s = _band_attention(qkv_s, 0, k_in, 0, v_in, 0, bias_s_t5[perm], batch=bs, q_rows=ts,
                            k_rows=c_win, n_kv_pairs=n_kvp, n_pairs=gq, win=c_win, n_past=0,
                            sel=sel_s)
    c_k_prompt = c_k_p.reshape(bp, seq, KV_HEADS, HEAD_DIM)
    c_v_prompt = c_v_p.reshape(bp, seq, KV_HEADS, HEAD_DIM)
    c_kidx_prompt = kidx_p.reshape(bp, seq, C_IDX_DIM)
    c_k_sample = c_k_s.reshape(bs, ts, KV_HEADS, HEAD_DIM)
    c_v_sample = c_v_s.reshape(bs, ts, KV_HEADS, HEAD_DIM)
    c_kidx_sample = kidx_s.reshape(bs, ts, C_IDX_DIM)
    w_o = _perm_rows(c_w_o, perm).astype(BF16)
    xp, xs, conv_p2, conv_s2 = _ffn_layer(xp, xs, att_p, att_s, w_o, layer, bp, bs, seq_tiles, ts, norm_ffn,
                                          ffn_w_in, ffn_conv_w, ffn_conv_b, ffn_w_out,
                                          state_ffn_conv, None)

    layer = 3
    lambda_init = 0.8 - 0.6 * math.exp(-0.3 * layer)
    w = d_w_qkv.astype(BF16)
    lam_vecs = jnp.stack([d_lambda_q1, d_lambda_k1, d_lambda_q2, d_lambda_k2]).astype(F32)
    secs = cast_dests([
        (0, aw, lambda c: [("cast", 0, c, scale)]),
        (aw, 2 * aw, lambda c: [("cast", 0, c, None), ("cast", 1, c - aw, None)]),
        (2 * aw, 3 * aw, lambda c: [("cast", 0, c, None), ("cast", 2, c - 2 * aw, None)]),
    ])

    def d_outs(n, tm):
        return [(n, 3 * aw, BF16, tm, "all"), (n, aw, F32, tm, "all"), (n, aw, F32, tm, "all")]

    secs_p = cast_dests([
        (0, aw, lambda c: [("cast", 0, c, scale)]),
        (aw, 2 * aw, lambda c: [("cast", 0, c, None),
                                ("heads", 1, (c - aw) // HEAD_DIM, HEAD_DIM, 2 * D_HEADS, None)]),
        (2 * aw, 3 * aw, lambda c: [("xpose", 3, c - 2 * aw),
                                    ("heads", 2, (c - 2 * aw) // LANES, LANES, D_HEADS, None)]),
    ])
    qkv, d_k_p, d_v_p, vt_p = _norm_proj(
        xp, norm_mix[layer], w, none_aux, secs_p,
        [(n_p, 2 * aw, BF16, ROW_TILE, "all"),
         (n_p * 2 * D_HEADS, HEAD_DIM, F32, ROW_TILE * 2 * D_HEADS, "all"),
         (n_p * D_HEADS, LANES, F32, ROW_TILE * D_HEADS, "all"),
         (aw, n_p, BF16, aw, "cols")], seq_tiles, ROW_TILE)
    qkv_s, d_k_s, d_v_s = _norm_proj(xs, norm_mix[layer], w, none_aux, secs, d_outs(n_s, n_s), 1,
                                     n_s)
    bias_d = _flash_bias(_t5_bias(t5_table, _flash_rel()))
    d_grp = 4
    att_p = _flash_attention(qkv, 0, qkv, aw // (d_grp * LANES), vt_p, 0,
                             bias_d, batch=bp, seq=seq, n_kv_pairs=D_HEADS, n_groups=d_grp,
                             n_pairs=1, mode="diff",
                             lam_vecs=lam_vecs, subln=d_subln.astype(F32), lambda_init=lambda_init)
    k_all, k_in = _pad_keys(cache_d_k, d_k_s.reshape(bs, ts, aw), c_pad, aw)
    v_all, v_in = _pad_keys(cache_d_v, d_v_s.reshape(bs, ts, aw), c_pad, aw)
    att_s = _band_attention(qkv_s, 0, k_in, 0, v_in, 0, bias_s_t5, batch=bs, q_rows=ts,
                            k_rows=c_win, n_kv_pairs=D_HEADS, n_pairs=1, win=c_win, n_past=0,
                            mode="diff", lam_vecs=lam_vecs, subln=d_subln.astype(F32),
                            lambda_init=lambda_init)
    d_k_prompt = d_k_p.reshape(bp, seq, 2 * D_HEADS, HEAD_DIM)
    d_v_prompt = d_v_p.reshape(bp, seq, D_HEADS, 2 * HEAD_DIM)
    d_k_sample = d_k_s.reshape(bs, ts, 2 * D_HEADS, HEAD_DIM)
    d_v_sample = d_v_s.reshape(bs, ts, D_HEADS, 2 * HEAD_DIM)
    w_o = d_w_o.astype(BF16)
    xp, xs, conv_p3, conv_s3 = _ffn_layer(xp, xs, att_p, att_s, w_o, layer, bp, bs, seq_tiles, ts, norm_ffn,
                                          ffn_w_in, ffn_conv_w, ffn_conv_b, ffn_w_out,
                                          state_ffn_conv, norm_final)
    assert depth == 4

    y_prompt = xp.reshape(bp, seq, d)
    y_sample = xs.reshape(bs, ts, d)
    ffn_conv_prompt = jnp.stack([conv_p0, conv_p1, conv_p2, conv_p3])
    ffn_conv_sample = jnp.stack([conv_s0, conv_s1, conv_s2, conv_s3])
    return (y_prompt, y_sample,
            a_k_prompt, a_v_prompt, a_k_sample, a_v_sample,
            b_k_prompt, b_v_prompt, b_k_sample, b_v_sample,
            c_k_prompt, c_v_prompt, c_kidx_prompt, c_k_sample, c_v_sample, c_kidx_sample,
            d_k_prompt, d_v_prompt, d_k_sample, d_v_sample,
            ffn_conv_prompt, ffn_conv_sample)


def _ffn_layer(xp, xs, att_p, att_s, w_o, layer, bp, bs, seq_tiles, ts, norm_ffn, ffn_w_in,
               ffn_conv_w, ffn_conv_b, ffn_w_out, state, final_gain):
    w_in = ffn_w_in[layer].astype(BF16)
    w_out = ffn_w_out[layer].astype(BF16)
    cw, cb, gain = ffn_conv_w[layer], ffn_conv_b[layer], norm_ffn[layer]
    xp, tail = _block_ffn(xp, att_p, w_o, gain, w_in, cw, cb, w_out, tm=ROW_TILE,
                          seq_tiles=seq_tiles, seq_rows=ROW_TILE * seq_tiles,
                          final_gain=final_gain)
    conv_p = tail.reshape(bp, SUBLANES, D_FF)[:, SUBLANES - (CONV_W - 1):, :]
    n_s = bs * ts
    st = state[layer]
    zeros = jnp.zeros((bs, ts - 2, D_FF), F32)
    fix1 = jnp.concatenate([st[:, 1:2], jnp.zeros((bs, 1, D_FF), F32), zeros], axis=1)
    fix2 = jnp.concatenate([st, zeros], axis=1)
    xs, a_s = _block_ffn(xs, att_s, w_o, gain, w_in, cw, cb, w_out, tm=n_s, seq_tiles=1,
                         seq_rows=ts, fixes=(fix1.reshape(n_s, D_FF), fix2.reshape(n_s, D_FF)),
                         final_gain=final_gain)
    conv_s = a_s.reshape(bs, ts, D_FF)[:, ts - (CONV_W - 1):, :]
    return xp, xs, conv_p, conv_s
```
